```python
import math
import jax
import jax.numpy as jnp
from jax import lax
import numpy as np


D_MODEL = 1024
BATCH = 2
SEQ = 8192
DEPTH = 2

GRID_W = 64
CTX_LEN = 256
N_MIXERS = 2
RMS_EPS = 1e-6

RW_HEAD = 64
RW_WIDTH = D_MODEL
RW_HEADS = RW_WIDTH // RW_HEAD
W_LORA = 64
A_LORA = 64
RW_IN_COLS = 4 * RW_WIDTH + 2 * (W_LORA + A_LORA)
RW_MIX_COLS = 3 * RW_WIDTH + 2 * (W_LORA + A_LORA)
GN_EPS = 64e-5

DA_HEADS = 8
DA_HEAD = 64
DA_VHEAD = 2 * DA_HEAD
DA_QW = DA_HEADS * 2 * DA_HEAD
DA_VW = DA_HEADS * DA_VHEAD
DA_GW = DA_VW
DA_IN_COLS = 2 * DA_QW + DA_VW + DA_GW
SUBLN_EPS = 1e-5
ROPE_THETA = 10000.0
ROPE_FREQS = DA_HEAD // 4
Q_BLOCK = 128

N_A = (DEPTH + 1) // 2
N_B = DEPTH // 2

kernel_name = "hybrid_rwkv7_diffattn_prefix_flow_block"


def rms_norm(x, g, eps=RMS_EPS):
    xf = x.astype(jnp.float32)
    y = xf * lax.rsqrt(jnp.mean(xf * xf, axis=-1, keepdims=True) + eps)
    return (y * g.astype(jnp.float32)).astype(x.dtype)


def centred_token_shift(u, mu):
    prev = jnp.pad(u[:, :-1], ((0, 0), (1, 0), (0, 0)))
    nxt = jnp.pad(u[:, 1:], ((0, 0), (0, 1), (0, 0)))
    return u + mu * (0.5 * (prev + nxt) - u)


def rwkv7_streams(h, w_in, mu, w0, w2, a0, a2, k_k, k_a):
    B, T, _ = h.shape
    C = RW_WIDTH
    heads = lambda t: t.reshape(t.shape[:-1] + (RW_HEADS, RW_HEAD))
    proj = (h @ w_in).astype(jnp.float32)
    g = proj[..., 3 * C:4 * C]
    mixed = centred_token_shift(
        jnp.concatenate([proj[..., :3 * C], proj[..., 4 * C:]], axis=-1), mu.astype(jnp.float32))
    r, k, v = mixed[..., :C], mixed[..., C:2 * C], mixed[..., 2 * C:3 * C]
    lw = mixed[..., 3 * C:3 * C + 2 * W_LORA].reshape(B, T, 2, W_LORA)
    la = mixed[..., 3 * C + 2 * W_LORA:].reshape(B, T, 2, A_LORA)
    w_raw = w0 + jnp.einsum("btdr,drc->btdc", jnp.tanh(lw), w2)
    decay = jnp.exp(-jnp.exp(-jax.nn.softplus(-w_raw) - 0.5))
    a = jax.nn.sigmoid(a0 + jnp.einsum("btdr,drc->btdc", la, a2))
    kk = heads(k * k_k)
    kk = kk * lax.rsqrt(jnp.sum(kk * kk, axis=-1, keepdims=True) + 1e-12)
    kd = k[:, :, None, :] * (1.0 + (a - 1.0) * k_a)
    return heads(r), heads(v), kk, heads(decay), heads(a), heads(kd), g


def wkv_scan(r, w, k, v, kk, a, s0, reverse):
    xs = tuple(jnp.moveaxis(t, 1, 0) for t in (r, w, k, v, kk, a))

    def step(S, inp):
        r_t, w_t, k_t, v_t, kk_t, a_t = inp
        sa = jnp.einsum("bhvk,bhk->bhv", S, kk_t)
        S = (S * w_t[:, :, None, :]
             - sa[..., None] * (kk_t * a_t)[:, :, None, :]
             + v_t[..., None] * k_t[:, :, None, :])
        return S, jnp.einsum("bhvk,bhk->bhv", S, r_t)

    s_fin, o = lax.scan(step, s0, xs, reverse=reverse)
    return jnp.moveaxis(o, 0, 1), s_fin


def scan_direction(streams, d, s0, reverse):
    r, v, kk, decay, a, kd, _ = streams
    return wkv_scan(r, decay[:, :, d], kd[:, :, d], v, kk, a[:, :, d], s0, reverse)


def rwkv7_readout(o, streams, r_k, gn_g, gn_b, w_out, out_dtype):
    r, v, _, _, _, kd, g = streams
    B, T = o.shape[:2]
    mean = jnp.mean(o, axis=-1, keepdims=True)
    var = jnp.mean(jnp.square(o - mean), axis=-1, keepdims=True)
    y = ((o - mean) * lax.rsqrt(var + GN_EPS)).reshape(B, T, RW_WIDTH) * gn_g + gn_b
    bonus = jnp.sum(jnp.sum(r[:, :, None] * kd * r_k, axis=-1, keepdims=True) * v[:, :, None], axis=2)
    y = (y + bonus.reshape(B, T, RW_WIDTH)) * jax.nn.silu(g)
    return y.astype(out_dtype) @ w_out


def rwkv7_mixer(h, hc, w_in, mu, w0, w2, a0, a2, k_k, k_a, r_k, gn_g, gn_b, w_out, ctx_out):
    lat = rwkv7_streams(h, w_in, mu, w0, w2, a0, a2, k_k, k_a)
    con = rwkv7_streams(hc, w_in, mu, w0, w2, a0, a2, k_k, k_a)
    s0 = jnp.zeros((h.shape[0], RW_HEADS, RW_HEAD, RW_HEAD), jnp.float32)
    oc_f, sc_f = scan_direction(con, 0, s0, False)
    oc_b, sc_b = scan_direction(con, 1, s0, True)
    ol_f, _ = scan_direction(lat, 0, sc_f, False)
    ol_b, _ = scan_direction(lat, 1, sc_b, True)
    y = rwkv7_readout(ol_f + ol_b, lat, r_k, gn_g, gn_b, w_out, h.dtype)
    yc = rwkv7_readout(oc_f + oc_b, con, r_k, gn_g, gn_b, w_out, hc.dtype) if ctx_out else None
    return y, yc


def axial_rope_tables(rows):
    row_ids = jnp.repeat(jnp.arange(rows), GRID_W).astype(jnp.float32)
    col_ids = jnp.tile(jnp.arange(GRID_W), rows).astype(jnp.float32)
    inv_freq = ROPE_THETA ** (-jnp.arange(ROPE_FREQS, dtype=jnp.float32) / ROPE_FREQS)
    ang = jnp.stack([row_ids[:, None] * inv_freq, col_ids[:, None] * inv_freq], axis=1)
    return jnp.cos(ang), jnp.sin(ang)


def apply_axial_rope(x, cos, sin):
    xr = x.reshape(x.shape[:-1] + (2, 2, ROPE_FREQS))
    x1, x2 = xr[..., 0, :], xr[..., 1, :]
    c = cos[None, :, None, None].astype(x.dtype)
    s = sin[None, :, None, None].astype(x.dtype)
    out = jnp.stack([x1 * c - x2 * s, x2 * c + x1 * s], axis=-2)
    return out.reshape(x.shape)


def diff_attention_blocks(q, k_all, v_all, lam):
    B, T = q.shape[:2]
    nb = T // Q_BLOCK
    qb = jnp.moveaxis(q.reshape((B, nb, Q_BLOCK) + q.shape[2:]), 1, 0)
    scale = DA_HEAD ** -0.5

    def block(q_blk):
        s = jnp.einsum("bqhcd,bkhcd->bchqk", q_blk, k_all).astype(jnp.float32) * scale
        p = jax.nn.softmax(s, axis=-1)
        attn = p[:, 0] - lam * p[:, 1]
        return jnp.einsum("bhqk,bkhe->bqhe", attn.astype(v_all.dtype), v_all)

    o = lax.map(block, qb)
    return jnp.moveaxis(o, 0, 1).reshape(B, T, DA_HEADS, DA_VHEAD)


def diff_finish(o, g, subln, lam_init, w_out):
    B, T = o.shape[:2]
    y = rms_norm(o, subln, SUBLN_EPS) * (1.0 - lam_init)
    return (y.reshape(B, T, DA_VW) * jax.nn.silu(g)) @ w_out


def diff_mixer(h, hc, layer_idx, w_in, qn, kn, lam_vecs, subln, w_out, cos, sin, ctx_out):
    B, T, _ = h.shape
    Lc = hc.shape[1]
    lam_init = 0.8 - 0.6 * math.exp(-0.3 * layer_idx)
    lv = lam_vecs.astype(jnp.float32)
    lam = jnp.exp(jnp.sum(lv[0] * lv[1])) - jnp.exp(jnp.sum(lv[2] * lv[3])) + lam_init
    proj = h @ w_in
    q = apply_axial_rope(rms_norm(proj[..., :DA_QW].reshape(B, T, DA_HEADS, 2, DA_HEAD), qn), cos, sin)
    k = apply_axial_rope(rms_norm(proj[..., DA_QW:2 * DA_QW].reshape(B, T, DA_HEADS, 2, DA_HEAD), kn), cos, sin)
    v = proj[..., 2 * DA_QW:2 * DA_QW + DA_VW].reshape(B, T, DA_HEADS, DA_VHEAD)
    g = proj[..., 2 * DA_QW + DA_VW:]
    if ctx_out:
        pc = hc @ w_in
        kvc = pc[..., DA_QW:2 * DA_QW + DA_VW]
    else:
        kvc = hc @ w_in[:, DA_QW:2 * DA_QW + DA_VW]
    kc = rms_norm(kvc[..., :DA_QW].reshape(B, Lc, DA_HEADS, 2, DA_HEAD), kn)
    vc = kvc[..., DA_QW:].reshape(B, Lc, DA_HEADS, DA_VHEAD)
    k_all = jnp.concatenate([kc, k], axis=1)
    v_all = jnp.concatenate([vc, v], axis=1)
    y = diff_finish(diff_attention_blocks(q, k_all, v_all, lam), g, subln, lam_init, w_out)
    yc = None
    if ctx_out:
        qc = rms_norm(pc[..., :DA_QW].reshape(B, Lc, DA_HEADS, 2, DA_HEAD), qn)
        yc = diff_finish(diff_attention_blocks(qc, kc, vc, lam), pc[..., 2 * DA_QW + DA_VW:],
                         subln, lam_init, w_out)
    return y, yc


def setup_inputs(seed: int = 0) -> dict:
    key = jax.random.key(seed)
    ks = jax.random.split(key, 26)
    f32 = jnp.float32
    nrm = lambda k, shape, s: jax.random.normal(k, shape, f32) * s
    D = D_MODEL
    return {
        "x": nrm(ks[0], (BATCH, SEQ, D), 1.0),
        "c": nrm(ks[1], (BATCH, D), 1.0),
        "ctx": nrm(ks[2], (BATCH, CTX_LEN, D), 1.0),
        "c_ctx": nrm(ks[3], (D,), 1.0),
        "ada_w": nrm(ks[4], (DEPTH, D, 3 * D), 0.5 * D ** -0.5),
        "ada_b": nrm(ks[5], (DEPTH, 3 * D), 0.02),
        "norm_g": 1.0 + nrm(ks[6], (DEPTH, D), 0.02),
        "rw_in": nrm(ks[7], (N_A, D, RW_IN_COLS), D ** -0.5),
        "rw_mu": jax.random.uniform(ks[8], (N_A, RW_MIX_COLS), f32),
        "rw_w0": jax.random.uniform(ks[9], (N_A, 2, RW_WIDTH), f32, minval=-6.0, maxval=-1.0),
        "rw_w2": nrm(ks[10], (N_A, 2, W_LORA, RW_WIDTH), 0.1 * W_LORA ** -0.5),
        "rw_a0": nrm(ks[11], (N_A, 2, RW_WIDTH), 0.1),
        "rw_a2": nrm(ks[12], (N_A, 2, A_LORA, RW_WIDTH), 0.1 * A_LORA ** -0.5),
        "rw_kk": 0.85 + nrm(ks[13], (N_A, RW_WIDTH), 0.05),
        "rw_ka": 1.0 + nrm(ks[14], (N_A, RW_WIDTH), 0.05),
        "rw_rk": nrm(ks[15], (N_A, RW_HEADS, RW_HEAD), 0.1),
        "rw_gn_g": 1.0 + nrm(ks[16], (N_A, RW_WIDTH), 0.02),
        "rw_gn_b": nrm(ks[17], (N_A, RW_WIDTH), 0.02),
        "rw_out": nrm(ks[18], (N_A, RW_WIDTH, D), RW_WIDTH ** -0.5),
        "da_in": nrm(ks[19], (N_B, D, DA_IN_COLS), D ** -0.5),
        "da_qn": 1.0 + nrm(ks[20], (N_B, DA_HEAD), 0.02),
        "da_kn": 1.0 + nrm(ks[21], (N_B, DA_HEAD), 0.02),
        "da_lam": nrm(ks[22], (N_B, 4, DA_HEAD), 0.1),
        "da_subln": 1.0 + nrm(ks[23], (N_B, DA_VHEAD), 0.02),
        "da_out": nrm(ks[24], (N_B, DA_VW, D), DA_VW ** -0.5),
    }


def reference(x, c, ctx, c_ctx, ada_w, ada_b, norm_g, rw_in, rw_mu, rw_w0, rw_w2, rw_a0, rw_a2,
              rw_kk, rw_ka, rw_rk, rw_gn_g, rw_gn_b, rw_out, da_in, da_qn, da_kn, da_lam,
              da_subln, da_out):
    n_lat = x.shape[1]
    rows = n_lat // GRID_W
    cos, sin = axial_rope_tables(rows)
    xc = ctx
    for i in range(DEPTH):
        last = i == DEPTH - 1
        j = i // N_MIXERS
        mod = jax.nn.silu(c) @ ada_w[i] + ada_b[i]
        shift, scale, gate = jnp.split(mod[:, None, :], 3, axis=-1)
        modc = jax.nn.silu(c_ctx) @ ada_w[i] + ada_b[i]
        shift_c, scale_c, gate_c = jnp.split(modc, 3)
        h = rms_norm(x, norm_g[i]) * (1.0 + scale) + shift
        hc = rms_norm(xc, norm_g[i]) * (1.0 + scale_c) + shift_c
        if i % N_MIXERS == 0:
            y, yc = rwkv7_mixer(h, hc, rw_in[j], rw_mu[j], rw_w0[j], rw_w2[j], rw_a0[j], rw_a2[j],
                                rw_kk[j], rw_ka[j], rw_rk[j], rw_gn_g[j], rw_gn_b[j], rw_out[j],
                                not last)
        else:
            y, yc = diff_mixer(h, hc, i, da_in[j], da_qn[j], da_kn[j], da_lam[j], da_subln[j],
                               da_out[j], cos, sin, not last)
        x = x + gate * y
        if not last:
            xc = xc + gate_c * yc
    return x
```

```python
import functools
import math

import jax
import jax.numpy as jnp
from jax import lax
from jax.experimental import pallas as pl
from jax.experimental.pallas import tpu as pltpu

F32 = jnp.float32
BF16 = jnp.bfloat16

GRID_W = 64
RMS_EPS = 1e-6
RW_HEAD = 64
W_LORA = 64
A_LORA = 64
GN_EPS = 64e-5
DA_HEADS = 8
DA_HEAD = 64
DA_VHEAD = 2 * DA_HEAD
SUBLN_EPS = 1e-5
ROPE_THETA = 10000.0
ROPE_FREQS = DA_HEAD // 4

LANES = 128
SUBLANES = 8
SCAN_CHUNK = 64
VMEM_LIMIT = 56 * 1024 * 1024


def _cparams(sem):
    return pltpu.CompilerParams(dimension_semantics=sem, vmem_limit_bytes=VMEM_LIMIT)


def _dot(a, b):
    return jnp.dot(a, b, preferred_element_type=F32)


def _dot_nt(a, b):
    return lax.dot_general(a, b, (((1,), (1,)), ((), ())), preferred_element_type=F32)


def _dot_tn(a, b):
    return lax.dot_general(a, b, (((0,), (0,)), ((), ())), preferred_element_type=F32)


def _split2(x):
    hi = x.astype(BF16)
    lo = (x - hi.astype(F32)).astype(BF16)
    return hi, lo


def _seg_sum(x, seg):
    hi, lo = _split2(x)
    return _dot(hi, seg) + _dot(lo, seg)


def _sigmoid(x):
    return 1.0 / (1.0 + jnp.exp(-x))


def _silu(x):
    return x * _sigmoid(x)


def _ada_kernel(c_ref, w_ref, b_ref, o_ref):
    s = _silu(c_ref[...])
    o_ref[...] = jnp.dot(s, w_ref[...], preferred_element_type=F32,
                         precision=lax.Precision.HIGHEST) + b_ref[...]


def _ada_mod(rows, ada_w, ada_b):
    depth, d, n = ada_w.shape
    tn = 768
    return pl.pallas_call(
        _ada_kernel,
        grid=(depth, n // tn),
        in_specs=[pl.BlockSpec((SUBLANES, d), lambda i, j: (0, 0)),
                  pl.BlockSpec((None, d, tn), lambda i, j: (i, 0, j)),
                  pl.BlockSpec((None, 1, tn), lambda i, j: (i, 0, j))],
        out_specs=pl.BlockSpec((None, SUBLANES, tn), lambda i, j: (i, 0, j)),
        out_shape=jax.ShapeDtypeStruct((depth, SUBLANES, n), F32),
        compiler_params=_cparams(("arbitrary", "arbitrary")),
        name="ada_mod",
    )(rows, ada_w, ada_b.reshape(depth, 1, n))


def _normmod_mm_kernel(x_ref, mult_ref, shift_ref, w_ref, o_ref):
    x = x_ref[...]
    y = x * lax.rsqrt(jnp.mean(x * x, axis=-1, keepdims=True) + RMS_EPS)
    h = y * mult_ref[...] + shift_ref[...]
    o_ref[...] = _dot(h.astype(BF16), w_ref[...])


def _normmod_mm(x2d, mult, shift, w, rows_per_group, tm):
    r, d = x2d.shape
    n = w.shape[1]
    tpg = rows_per_group // tm
    return pl.pallas_call(
        _normmod_mm_kernel,
        grid=(r // tm,),
        in_specs=[pl.BlockSpec((tm, d), lambda i: (i, 0)),
                  pl.BlockSpec((None, 1, d), lambda i: (i // tpg, 0, 0)),
                  pl.BlockSpec((None, 1, d), lambda i: (i // tpg, 0, 0)),
                  pl.BlockSpec((d, n), lambda i: (0, 0))],
        out_specs=pl.BlockSpec((tm, n), lambda i: (i, 0)),
        out_shape=jax.ShapeDtypeStruct((r, n), F32),
        compiler_params=_cparams(("arbitrary",)),
        name="normmod_mm",
    )(x2d, mult, shift, w)


def _token_shift(u, prev8, next8, mu, first, last):
    tm = u.shape[0]
    row = lax.broadcasted_iota(jnp.int32, u.shape, 0)
    prow = jnp.where(first, 0.0, prev8[SUBLANES - 1:SUBLANES, :])
    nrow = jnp.where(last, 0.0, next8[0:1, :])
    prev = jnp.where(row == 0, prow, pltpu.roll(u, 1, 0))
    nxt = jnp.where(row == tm - 1, nrow, pltpu.roll(u, tm - 1, 0))
    return u + mu * (0.5 * (prev + nxt) - u)


def _rwkv_prep_kernel(pa_ref, pap_ref, pan_ref, pb_ref, pbp_ref, pbn_ref,
                      mua_ref, mub_ref, w2_ref, w0_ref, a2_ref, a0_ref, kk_ref, ka_ref, seg_ref,
                      r_ref, v_ref, kkn_ref, lw_ref, kd_ref, bb_ref, *, tiles_per_seq, c):
    i = pl.program_id(0)
    first = (i % tiles_per_seq) == 0
    last = (i % tiles_per_seq) == tiles_per_seq - 1
    ma = _token_shift(pa_ref[...], pap_ref[...], pan_ref[...], mua_ref[...], first, last)
    mb = _token_shift(pb_ref[...], pbp_ref[...], pbn_ref[...], mub_ref[...], first, last)
    r = ma[:, :c]
    k = ma[:, c:2 * c]
    v = ma[:, 2 * c:]
    nl = 2 * W_LORA
    lw = jnp.tanh(mb[:, :nl])
    la = mb[:, nl:]
    w_raw = w0_ref[...] + _dot(lw.astype(BF16), w2_ref[...])
    a = _sigmoid(a0_ref[...] + _dot(la.astype(BF16), a2_ref[...]))
    logw = -math.exp(-0.5) * _sigmoid(w_raw)
    kk = k * kk_ref[...]
    kkn = kk * lax.rsqrt(_seg_sum(kk * kk, seg_ref[...]) + 1e-12)
    r_ref[...] = r
    v_ref[...] = v
    kkn_ref[...] = kkn
    for d in range(2):
        a_d = a[:, d * c:(d + 1) * c]
        lw_ref[d] = logw[:, d * c:(d + 1) * c]
        kd_ref[d] = k * (1.0 + (a_d - 1.0) * ka_ref[...])
        bb_ref[d] = kkn * a_d


def _rwkv_prep(proj, seq_len, mu, w0, w2, a0, a2, k_k, k_a, seg, tm):
    r_rows, ncols = proj.shape
    c = k_k.shape[-1]
    nb = ncols - 4 * c
    assert nb == 2 * (W_LORA + A_LORA) and (4 * c) % nb == 0 and seq_len % tm == 0
    tps = seq_len // tm
    t8 = tm // SUBLANES
    nblk8 = r_rows // SUBLANES
    bcol = (4 * c) // nb
    mua = mu[:3 * c].reshape(1, 3 * c)
    mub = mu[3 * c:].reshape(1, nb)
    zw = jnp.zeros((W_LORA, c), F32)
    w2cat = jnp.concatenate([jnp.concatenate([w2[0], zw], 1), jnp.concatenate([zw, w2[1]], 1)], 0).astype(BF16)
    a2cat = jnp.concatenate([jnp.concatenate([a2[0], zw], 1), jnp.concatenate([zw, a2[1]], 1)], 0).astype(BF16)
    w0cat = w0.reshape(1, 2 * c)
    a0cat = a0.reshape(1, 2 * c)
    full = lambda shape: pl.BlockSpec(shape, lambda i: (0,) * len(shape))
    prev_idx = lambda i: jnp.maximum(i * t8 - 1, 0)
    next_idx = lambda i: jnp.minimum((i + 1) * t8, nblk8 - 1)
    row_out = pl.BlockSpec((tm, c), lambda i: (i, 0))
    dir_out = pl.BlockSpec((2, tm, c), lambda i: (0, i, 0))
    kern = functools.partial(_rwkv_prep_kernel, tiles_per_seq=tps, c=c)
    return pl.pallas_call(
        kern,
        grid=(r_rows // tm,),
        in_specs=[pl.BlockSpec((tm, 3 * c), lambda i: (i, 0)),
                  pl.BlockSpec((SUBLANES, 3 * c), lambda i: (prev_idx(i), 0)),
                  pl.BlockSpec((SUBLANES, 3 * c), lambda i: (next_idx(i), 0)),
                  pl.BlockSpec((tm, nb), lambda i: (i, bcol)),
                  pl.BlockSpec((SUBLANES, nb), lambda i: (prev_idx(i), bcol)),
                  pl.BlockSpec((SUBLANES, nb), lambda i: (next_idx(i), bcol)),
                  full((1, 3 * c)), full((1, nb)),
                  full((2 * W_LORA, 2 * c)), full((1, 2 * c)),
                  full((2 * A_LORA, 2 * c)), full((1, 2 * c)),
                  full((1, c)), full((1, c)), full((c, c))],
        out_specs=[row_out, row_out, row_out, dir_out, dir_out, dir_out],
        out_shape=[jax.ShapeDtypeStruct((r_rows, c), F32)] * 3
                  + [jax.ShapeDtypeStruct((2, r_rows, c), F32)] * 3,
        compiler_params=_cparams(("arbitrary",)),
        name="rwkv_prep",
    )(proj, proj, proj, proj, proj, proj, mua, mub, w2cat, w0cat, a2cat, a0cat,
      k_k.reshape(1, c), k_a.reshape(1, c), seg)


def _head_pair_stack(xp, lane_a):
    return jnp.concatenate([jnp.where(lane_a, xp, 0.0), jnp.where(lane_a, 0.0, xp)], axis=0)


def _wkv_scan_kernel(r_ref, v_ref, kk_ref, lw_ref, kd_ref, bb_ref, s0_ref, o_ref, sfin_ref, h_scr,
                     *, chunk, n_pairs):
    L = chunk
    d = pl.program_id(1)
    ci = pl.program_id(2)
    nc = pl.num_programs(2)

    @pl.when(ci == 0)
    def _():
        h_scr[...] = s0_ref[...]

    sgn = 1 - 2 * d
    ti = lax.broadcasted_iota(jnp.int32, (L, L), 0)
    si = lax.broadcasted_iota(jnp.int32, (L, L), 1)
    tri = (((ti - si) * sgn) >= 0).astype(BF16)

    lw = lw_ref[...]
    hi = lw.astype(BF16)
    r1 = lw - hi.astype(F32)
    mid = r1.astype(BF16)
    lo = (r1 - mid.astype(F32)).astype(BF16)
    cum = _dot(tri, hi) + _dot(tri, mid) + _dot(tri, lo)
    tot = jnp.sum(lw, axis=0, keepdims=True)

    gam = jnp.exp(cum)
    gprev = jnp.exp(cum - lw)
    ginv = jnp.exp(-cum)
    gend = jnp.exp(tot - cum)
    kk = kk_ref[...]
    kd = kd_ref[...]
    bb = bb_ref[...]
    kt = kk * gprev
    rt = r_ref[...] * gam
    bt = bb * ginv
    kdt = kd * ginv
    khat = kd * gend
    bhat = bb * gend
    v = v_ref[...]

    p2 = 2 * L
    lane_a = lax.broadcasted_iota(jnp.int32, (L, LANES), 1) < RW_HEAD
    rt_i = lax.broadcasted_iota(jnp.int32, (p2, p2), 0) & (L - 1)
    cs_i = lax.broadcasted_iota(jnp.int32, (p2, p2), 1) & (L - 1)
    dts = (rt_i - cs_i) * sgn
    strict = dts > 0
    incl = dts >= 0
    eye = (lax.broadcasted_iota(jnp.int32, (p2, p2), 0)
           == lax.broadcasted_iota(jnp.int32, (p2, p2), 1)).astype(F32)

    for p in range(n_pairs):
        sl = slice(p * LANES, (p + 1) * LANES)
        stk = lambda x: _head_pair_stack(x[:, sl], lane_a)
        kt_s = stk(kt).astype(BF16)
        rt_s = stk(rt).astype(BF16)
        x_s = jnp.concatenate([kt_s, rt_s], axis=0)
        y_s = jnp.concatenate([stk(bt).astype(BF16), stk(kdt).astype(BF16)], axis=0)
        sc = _dot_nt(x_s, y_s)
        a_b = jnp.where(strict, sc[:p2, :p2], 0.0)
        a_k = jnp.where(strict, sc[:p2, p2:], 0.0)
        p_b = jnp.where(incl, sc[p2:, :p2], 0.0)
        p_k = jnp.where(incl, sc[p2:, p2:], 0.0)

        pw = -a_b
        tinv = eye + pw
        for _ in range(int(math.log2(L)) - 1):
            pwb = pw.astype(BF16)
            pw = _dot(pwb, pwb)
            tinv = _dot(tinv.astype(BF16), (eye + pw).astype(BF16))

        h = h_scr[p]
        xh = _dot(x_s, h.astype(BF16))
        v_s = stk(v).astype(BF16)
        y2 = xh[:p2] + _dot(a_k.astype(BF16), v_s)
        u = _dot(tinv.astype(BF16), y2.astype(BF16))
        u_b = u.astype(BF16)
        vu = jnp.concatenate([v_s, u_b], axis=0)
        o_s = xh[p2:] + _dot(jnp.concatenate([p_k, -p_b], axis=1).astype(BF16), vu)
        o_ref[:, sl] = o_s[:L] + o_s[L:]

        kb = jnp.concatenate([stk(khat), -stk(bhat)], axis=0).astype(BF16)
        tot_col = jnp.broadcast_to(tot[:, sl], (LANES, LANES)).T
        h_scr[p] = jnp.exp(tot_col) * h + _dot_tn(kb, vu)

    @pl.when(ci == nc - 1)
    def _():
        sfin_ref[...] = h_scr[...]


def _wkv_scan(r, v, kkn, logw, kd, bb, s0, chunk=SCAN_CHUNK):
    b, t, c = r.shape
    n_pairs = c // LANES
    assert t % chunk == 0 and 2 * RW_HEAD == LANES
    nc = t // chunk
    cidx = lambda d, ci: ci + d * (nc - 1 - 2 * ci)
    shared = pl.BlockSpec((None, chunk, c), lambda bi, d, ci: (bi, cidx(d, ci), 0))
    perdir = pl.BlockSpec((None, None, chunk, c), lambda bi, d, ci: (d, bi, cidx(d, ci), 0))
    state = pl.BlockSpec((None, None, n_pairs, LANES, LANES), lambda bi, d, ci: (bi, d, 0, 0, 0))
    kern = functools.partial(_wkv_scan_kernel, chunk=chunk, n_pairs=n_pairs)
    return pl.pallas_call(
        kern,
        grid=(b, 2, nc),
        in_specs=[shared, shared, shared, perdir, perdir, perdir, state],
        out_specs=[perdir, state],
        out_shape=[jax.ShapeDtypeStruct((2, b, t, c), F32),
                   jax.ShapeDtypeStruct((b, 2, n_pairs, LANES, LANES), F32)],
        scratch_shapes=[pltpu.VMEM((n_pairs, LANES, LANES), F32)],
        compiler_params=_cparams(("arbitrary", "arbitrary", "arbitrary")),
        name="wkv_scan",
    )(r, v, kkn, logw, kd, bb, s0)


def _rwkv_readout_kernel(o_ref, r_ref, v_ref, kd_ref, g_ref, x_ref, gate_ref,
                         rk_ref, gng_ref, gnb_ref, seg_ref, w_ref, out_ref):
    seg = seg_ref[...]
    inv_n = 1.0 / RW_HEAD
    o = o_ref[0] + o_ref[1]
    mean = _seg_sum(o, seg) * inv_n
    dlt = o - mean
    var = _seg_sum(dlt * dlt, seg) * inv_n
    y = dlt * lax.rsqrt(var + GN_EPS) * gng_ref[...] + gnb_ref[...]
    bonus = _seg_sum(r_ref[...] * (kd_ref[0] + kd_ref[1]) * rk_ref[...], seg) * v_ref[...]
    y = (y + bonus) * _silu(g_ref[...])
    out_ref[...] = x_ref[...] + gate_ref[...] * _dot(y.astype(BF16), w_ref[...])


def _rwkv_readout(o, r, v, kd, proj, x2d, gate, r_k, gn_g, gn_b, seg, w_out, rows_per_group, tm):
    r_rows, c = r.shape
    d = x2d.shape[1]
    tpg = rows_per_group // tm
    gcol = (3 * c) // c
    full = lambda shape: pl.BlockSpec(shape, lambda i: (0,) * len(shape))
    row = lambda n: pl.BlockSpec((tm, n), lambda i: (i, 0))
    dirs = pl.BlockSpec((2, tm, c), lambda i: (0, i, 0))
    return pl.pallas_call(
        _rwkv_readout_kernel,
        grid=(r_rows // tm,),
        in_specs=[dirs, row(c), row(c), dirs,
                  pl.BlockSpec((tm, c), lambda i: (i, gcol)),
                  row(d),
                  pl.BlockSpec((None, 1, d), lambda i: (i // tpg, 0, 0)),
                  full((1, c)), full((1, c)), full((1, c)), full((c, c)), full((c, d))],
        out_specs=row(d),
        out_shape=jax.ShapeDtypeStruct((r_rows, d), F32),
        compiler_params=_cparams(("arbitrary",)),
        name="rwkv_readout",
    )(o, r, v, kd, proj, x2d, gate, r_k.reshape(1, c), gn_g.reshape(1, c), gn_b.reshape(1, c), seg, w_out)


def _head_norm(x, gain, seg):
    ms = _seg_sum(x * x, seg) * (1.0 / DA_HEAD)
    return x * lax.rsqrt(ms + RMS_EPS) * gain


def _rope(x, cos, sin_signed):
    n = x.shape[-1]
    lane = lax.broadcasted_iota(jnp.int32, x.shape, 1)
    half0 = (lane & (2 * ROPE_FREQS - 1)) < ROPE_FREQS
    partner = jnp.where(half0, pltpu.roll(x, n - ROPE_FREQS, 1), pltpu.roll(x, ROPE_FREQS, 1))
    reps = n // cos.shape[-1]
    wide = lambda tab: jnp.concatenate([tab] * reps, axis=1)
    return x * wide(cos) + partner * wide(sin_signed)


def _qkv_prep_kernel(*refs, has_q, use_rope, q_scale):
    it = iter(refs)
    q_ref = next(it) if has_q else None
    k_ref = next(it)
    v_ref = next(it)
    cos_ref = next(it) if use_rope else None
    sin_ref = next(it) if use_rope else None
    qn_ref = next(it) if has_q else None
    kn_ref = next(it)
    seg_ref = next(it)
    qo_ref = next(it) if has_q else None
    ko_ref = next(it)
    vo_ref = next(it)
    seg = seg_ref[...]
    k = _head_norm(k_ref[...], kn_ref[...], seg)
    if use_rope:
        k = _rope(k, cos_ref[...], sin_ref[...])
    ko_ref[...] = k.astype(BF16)
    vo_ref[...] = v_ref[...].astype(BF16)
    if has_q:
        q = _head_norm(q_ref[...], qn_ref[...], seg)
        if use_rope:
            q = _rope(q, cos_ref[...], sin_ref[...])
        qo_ref[...] = (q * q_scale).astype(BF16)


def _qkv_prep(proj, col0, has_q, cos, sin_signed, qn, kn, seg, seq_len, tm):
    r_rows = proj.shape[0]
    w = seg.shape[0]
    use_rope = cos is not None
    tps = seq_len // tm
    full = lambda shape: pl.BlockSpec(shape, lambda i: (0,) * len(shape))
    col = lambda j: pl.BlockSpec((tm, w), lambda i: (i, j))
    tab = pl.BlockSpec((tm, LANES), lambda i: (i % tps, 0))
    in_specs, args = [], []
    ncol = col0
    if has_q:
        in_specs.append(col(ncol)); args.append(proj); ncol += 1
    in_specs += [col(ncol), col(ncol + 1)]
    args += [proj, proj]
    if use_rope:
        in_specs += [tab, tab]
        args += [cos, sin_signed]
    tile_gain = lambda g: jnp.tile(g, w // g.shape[0]).reshape(1, w)
    if has_q:
        in_specs.append(full((1, w))); args.append(tile_gain(qn))
    in_specs += [full((1, w)), full((w, w))]
    args += [tile_gain(kn), seg]
    n_out = 3 if has_q else 2
    kern = functools.partial(_qkv_prep_kernel, has_q=has_q, use_rope=use_rope, q_scale=DA_HEAD ** -0.5)
    return pl.pallas_call(
        kern,
        grid=(r_rows // tm,),
        in_specs=in_specs,
        out_specs=[pl.BlockSpec((tm, w), lambda i: (i, 0))] * n_out,
        out_shape=[jax.ShapeDtypeStruct((r_rows, w), BF16)] * n_out,
        compiler_params=_cparams(("arbitrary",)),
        name="qkv_prep_q" if has_q else "qkv_prep_ctx",
    )(*args)


def _diff_attn_kernel(q_ref, k_ref, v_ref, lam_ref, sub_ref, o_ref, m_scr, l_scr, acc_scr, *, lam_init):
    j = pl.program_id(3)
    nk = pl.num_programs(3)

    @pl.when(j == 0)
    def _():
        m_scr[...] = jnp.full(m_scr.shape, -jnp.inf, F32)
        l_scr[...] = jnp.zeros(l_scr.shape, F32)
        acc_scr[...] = jnp.zeros(acc_scr.shape, F32)

    q = q_ref[...]
    k = k_ref[...]
    v = v_ref[...]
    for c in range(2):
        sl = slice(c * DA_HEAD, (c + 1) * DA_HEAD)
        s = _dot_nt(q[:, sl], k[:, sl])
        m_prev = m_scr[c]
        m_new = jnp.maximum(m_prev, jnp.max(s, axis=-1, keepdims=True))
        alpha = jnp.exp(m_prev - m_new)
        p = jnp.exp(s - m_new)
        l_scr[c] = alpha * l_scr[c] + jnp.sum(p, axis=-1, keepdims=True)
        acc_scr[c] = alpha * acc_scr[c] + _dot(p.astype(BF16), v)
        m_scr[c] = m_new

    @pl.when(j == nk - 1)
    def _():
        lv = lam_ref[...]
        lam = (jnp.exp(jnp.sum(lv[0:1] * lv[1:2], axis=-1, keepdims=True))
               - jnp.exp(jnp.sum(lv[2:3] * lv[3:4], axis=-1, keepdims=True)) + lam_init)
        o = acc_scr[0] / l_scr[0] - lam * (acc_scr[1] / l_scr[1])
        y = o * lax.rsqrt(jnp.mean(o * o, axis=-1, keepdims=True) + SUBLN_EPS)
        o_ref[...] = y * sub_ref[...] * (1.0 - lam_init)


def _diff_attn(q, k_all, v_all, lam_vecs, subln, lam_init, tq, tk):
    b, t, w = q.shape
    lk = k_all.shape[1]
    nh = w // DA_VHEAD
    assert t % tq == 0 and lk % tk == 0
    kern = functools.partial(_diff_attn_kernel, lam_init=lam_init)
    return pl.pallas_call(
        kern,
        grid=(b, nh, t // tq, lk // tk),
        in_specs=[pl.BlockSpec((None, tq, DA_VHEAD), lambda bi, h, i, j: (bi, i, h)),
                  pl.BlockSpec((None, tk, DA_VHEAD), lambda bi, h, i, j: (bi, j, h)),
                  pl.BlockSpec((None, tk, DA_VHEAD), lambda bi, h, i, j: (bi, j, h)),
                  pl.BlockSpec((4, DA_HEAD), lambda bi, h, i, j: (0, 0)),
                  pl.BlockSpec((1, DA_VHEAD), lambda bi, h, i, j: (0, 0))],
        out_specs=pl.BlockSpec((None, tq, DA_VHEAD), lambda bi, h, i, j: (bi, i, h)),
        out_shape=jax.ShapeDtypeStruct((b, t, w), F32),
        scratch_shapes=[pltpu.VMEM((2, tq, 1), F32), pltpu.VMEM((2, tq, 1), F32),
                        pltpu.VMEM((2, tq, DA_VHEAD), F32)],
        compiler_params=_cparams(("arbitrary", "arbitrary", "arbitrary", "arbitrary")),
        name="diff_attn",
    )(q, k_all, v_all, lam_vecs, subln.reshape(1, DA_VHEAD))


def _gated_out_kernel(y_ref, g_ref, x_ref, gate_ref, w_ref, out_ref):
    y = y_ref[...] * _silu(g_ref[...])
    out_ref[...] = x_ref[...] + gate_ref[...] * _dot(y.astype(BF16), w_ref[...])


def _gated_out(y, proj, gcol, x2d, gate, w_out, rows_per_group, tm):
    r_rows, w = y.shape
    d = x2d.shape[1]
    tpg = rows_per_group // tm
    row = lambda n: pl.BlockSpec((tm, n), lambda i: (i, 0))
    return pl.pallas_call(
        _gated_out_kernel,
        grid=(r_rows // tm,),
        in_specs=[row(w), pl.BlockSpec((tm, w), lambda i: (i, gcol)), row(d),
                  pl.BlockSpec((None, 1, d), lambda i: (i // tpg, 0, 0)),
                  pl.BlockSpec((w, d), lambda i: (0, 0))],
        out_specs=row(d),
        out_shape=jax.ShapeDtypeStruct((r_rows, d), F32),
        compiler_params=_cparams(("arbitrary",)),
        name="gated_out",
    )(y, proj, x2d, gate, w_out)


def _segment_matrix(n, width):
    i = lax.broadcasted_iota(jnp.int32, (n, n), 0) // width
    j = lax.broadcasted_iota(jnp.int32, (n, n), 1) // width
    return (i == j).astype(BF16)


def _rope_tables(rows):
    row_ids = jnp.repeat(jnp.arange(rows), GRID_W).astype(F32)
    col_ids = jnp.tile(jnp.arange(GRID_W), rows).astype(F32)
    inv_freq = ROPE_THETA ** (-jnp.arange(ROPE_FREQS, dtype=F32) / ROPE_FREQS)
    ang_r = row_ids[:, None] * inv_freq
    ang_c = col_ids[:, None] * inv_freq
    cr, sr, cc, sc = jnp.cos(ang_r), jnp.sin(ang_r), jnp.cos(ang_c), jnp.sin(ang_c)
    cos = jnp.concatenate([cr, cr, cc, cc], axis=-1)
    sin_signed = jnp.concatenate([-sr, sr, -sc, sc], axis=-1)
    return jnp.tile(cos, (1, 2)), jnp.tile(sin_signed, (1, 2))


def kernel(x, c, ctx, c_ctx, ada_w, ada_b, norm_g, rw_in, rw_mu, rw_w0, rw_w2, rw_a0, rw_a2, rw_kk, rw_ka, rw_rk, rw_gn_g, rw_gn_b, rw_out, da_in, da_qn, da_kn, da_lam, da_subln, da_out):
    bsz, t, d = x.shape
    lc = ctx.shape[1]
    depth = ada_w.shape[0]
    assert depth == 2 and bsz + 1 <= SUBLANES
    cw = rw_kk.shape[-1]
    x2 = x.reshape(bsz * t, d)
    xc2 = ctx.reshape(bsz * lc, d)

    rows = jnp.zeros((SUBLANES, d), F32).at[:bsz].set(c).at[bsz].set(c_ctx)
    mod = _ada_mod(rows, ada_w, ada_b)
    shift, scale, gate = mod[:, :, :d], mod[:, :, d:2 * d], mod[:, :, 2 * d:]
    mult = norm_g[:, None, :] * (1.0 + scale)
    lat = lambda a, i: a[i, :bsz].reshape(bsz, 1, d)
    con = lambda a, i: a[i, bsz:bsz + 1].reshape(1, 1, d)

    tm_l = 256
    tm_c = min(256, lc)

    w_in0 = rw_in[0].astype(BF16)
    proj_l = _normmod_mm(x2, lat(mult, 0), lat(shift, 0), w_in0, t, tm_l)
    proj_c = _normmod_mm(xc2, con(mult, 0), con(shift, 0), w_in0, bsz * lc, tm_c)
    seg64 = _segment_matrix(cw, RW_HEAD)
    prep = functools.partial(_rwkv_prep, mu=rw_mu[0], w0=rw_w0[0], w2=rw_w2[0], a0=rw_a0[0], a2=rw_a2[0],
                             k_k=rw_kk[0], k_a=rw_ka[0], seg=seg64)
    r_l, v_l, kk_l, lw_l, kd_l, bb_l = prep(proj_l, t, tm=tm_l)
    r_c, v_c, kk_c, lw_c, kd_c, bb_c = prep(proj_c, lc, tm=tm_c)

    s0 = jnp.zeros((bsz, 2, cw // LANES, LANES, LANES), F32)
    b3 = lambda a, n: a.reshape(bsz, n, cw)
    b4 = lambda a, n: a.reshape(2, bsz, n, cw)
    o_c, s_c = _wkv_scan(b3(r_c, lc), b3(v_c, lc), b3(kk_c, lc), b4(lw_c, lc), b4(kd_c, lc), b4(bb_c, lc), s0)
    o_l, _ = _wkv_scan(b3(r_l, t), b3(v_l, t), b3(kk_l, t), b4(lw_l, t), b4(kd_l, t), b4(bb_l, t), s_c)

    w_out0 = rw_out[0].astype(BF16)
    readout = functools.partial(_rwkv_readout, r_k=rw_rk[0].reshape(-1), gn_g=rw_gn_g[0], gn_b=rw_gn_b[0],
                                seg=seg64, w_out=w_out0)
    x1 = readout(o_l.reshape(2, bsz * t, cw), r_l, v_l, kd_l, proj_l, x2, lat(gate, 0),
                 rows_per_group=t, tm=tm_l)
    xc1 = readout(o_c.reshape(2, bsz * lc, cw), r_c, v_c, kd_c, proj_c, xc2, con(gate, 0),
                  rows_per_group=bsz * lc, tm=tm_c)

    qw = DA_HEADS * 2 * DA_HEAD
    w_in1 = da_in[0].astype(BF16)
    proj2_l = _normmod_mm(x1, lat(mult, 1), lat(shift, 1), w_in1, t, tm_l)
    proj2_c = _normmod_mm(xc1, con(mult, 1), con(shift, 1), w_in1[:, qw:3 * qw], bsz * lc, tm_c)
    seg_da = _segment_matrix(qw, DA_HEAD)
    cos, sin_signed = _rope_tables(t // GRID_W)
    q_b, k_b, v_b = _qkv_prep(proj2_l, 0, True, cos, sin_signed, da_qn[0], da_kn[0], seg_da, t, tm_l)
    kc_b, vc_b = _qkv_prep(proj2_c, 0, False, None, None, None, da_kn[0], seg_da, lc, tm_c)
    k_all = jnp.concatenate([kc_b.reshape(bsz, lc, qw), k_b.reshape(bsz, t, qw)], axis=1)
    v_all = jnp.concatenate([vc_b.reshape(bsz, lc, qw), v_b.reshape(bsz, t, qw)], axis=1)
    lam_init = 0.8 - 0.6 * math.exp(-0.3 * 1)
    lk = lc + t
    tk = max(n for n in range(LANES, 1024 + 1, LANES) if lk % n == 0)
    y_att = _diff_attn(q_b.reshape(bsz, t, qw), k_all, v_all, da_lam[0], da_subln[0], lam_init,
                       tq=min(512, t), tk=tk)
    out = _gated_out(y_att.reshape(bsz * t, qw), proj2_l, 3, x1, lat(gate, 1), da_out[0].astype(BF16),
                     rows_per_group=t, tm=tm_l)
    return out.reshape(bsz, t, d)
```

```python
import functools
import math

import jax
import jax.numpy as jnp
from jax import lax
from jax.experimental import pallas as pl
from jax.experimental.pallas import tpu as pltpu

F32 = jnp.float32
BF16 = jnp.bfloat16

GRID_W = 64
RMS_EPS = 1e-6
RW_HEAD = 64
W_LORA = 64
A_LORA = 64
GN_EPS = 64e-5
DA_HEADS = 8
DA_HEAD = 64
DA_VHEAD = 2 * DA_HEAD
SUBLN_EPS = 1e-5
ROPE_THETA = 10000.0
ROPE_FREQS = DA_HEAD // 4

LANES = 128
SUBLANES = 8
SCAN_CHUNK = 64
VMEM_LIMIT = 56 * 1024 * 1024
MAX_UNSHIFTED_SCORE = 40.0


def _cparams(sem):
    return pltpu.CompilerParams(dimension_semantics=sem, vmem_limit_bytes=VMEM_LIMIT)


def _dot(a, b):
    return jnp.dot(a, b, preferred_element_type=F32)


def _dot_nt(a, b):
    return lax.dot_general(a, b, (((1,), (1,)), ((), ())), preferred_element_type=F32)


def _dot_tn(a, b):
    return lax.dot_general(a, b, (((0,), (0,)), ((), ())), preferred_element_type=F32)


def _split2(x):
    hi = x.astype(BF16)
    lo = (x - hi.astype(F32)).astype(BF16)
    return hi, lo


def _seg_sum(x, seg):
    hi, lo = _split2(x)
    return _dot(hi, seg) + _dot(lo, seg)


def _sigmoid(x):
    return 1.0 / (1.0 + jnp.exp(-x))


def _silu(x):
    return x * _sigmoid(x)


def _ada_kernel(c_ref, w_ref, b_ref, o_ref):
    s = _silu(c_ref[...])
    o_ref[...] = jnp.dot(s, w_ref[...], preferred_element_type=F32,
                         precision=lax.Precision.HIGHEST) + b_ref[...]


def _ada_mod(rows, ada_w, ada_b):
    depth, d, n = ada_w.shape
    tn = 768
    return pl.pallas_call(
        _ada_kernel,
        grid=(depth, n // tn),
        in_specs=[pl.BlockSpec((SUBLANES, d), lambda i, j: (0, 0)),
                  pl.BlockSpec((None, d, tn), lambda i, j: (i, 0, j)),
                  pl.BlockSpec((None, 1, tn), lambda i, j: (i, 0, j))],
        out_specs=pl.BlockSpec((None, SUBLANES, tn), lambda i, j: (i, 0, j)),
        out_shape=jax.ShapeDtypeStruct((depth, SUBLANES, n), F32),
        compiler_params=_cparams(("arbitrary", "arbitrary")),
        name="ada_mod",
    )(rows, ada_w, ada_b.reshape(depth, 1, n))


def _normmod_mm_kernel(x_ref, mult_ref, shift_ref, w_ref, o_ref):
    x = x_ref[...]
    y = x * lax.rsqrt(jnp.mean(x * x, axis=-1, keepdims=True) + RMS_EPS)
    h = y * mult_ref[...] + shift_ref[...]
    o_ref[...] = _dot(h.astype(BF16), w_ref[...])


def _normmod_mm(x2d, mult, shift, w, rows_per_group, tm):
    r, d = x2d.shape
    n = w.shape[1]
    tpg = rows_per_group // tm
    return pl.pallas_call(
        _normmod_mm_kernel,
        grid=(r // tm,),
        in_specs=[pl.BlockSpec((tm, d), lambda i: (i, 0)),
                  pl.BlockSpec((None, 1, d), lambda i: (i // tpg, 0, 0)),
                  pl.BlockSpec((None, 1, d), lambda i: (i // tpg, 0, 0)),
                  pl.BlockSpec((d, n), lambda i: (0, 0))],
        out_specs=pl.BlockSpec((tm, n), lambda i: (i, 0)),
        out_shape=jax.ShapeDtypeStruct((r, n), F32),
        compiler_params=_cparams(("arbitrary",)),
        name="normmod_mm",
    )(x2d, mult, shift, w)


def _token_shift(u, prev8, next8, mu, first, last):
    tm = u.shape[0]
    row = lax.broadcasted_iota(jnp.int32, u.shape, 0)
    prow = jnp.where(first, 0.0, prev8[SUBLANES - 1:SUBLANES, :])
    nrow = jnp.where(last, 0.0, next8[0:1, :])
    prev = jnp.where(row == 0, prow, pltpu.roll(u, 1, 0))
    nxt = jnp.where(row == tm - 1, nrow, pltpu.roll(u, tm - 1, 0))
    return u + mu * (0.5 * (prev + nxt) - u)


def _rwkv_prep_kernel(pa_ref, pap_ref, pan_ref, pb_ref, pbp_ref, pbn_ref,
                      mua_ref, mub_ref, w2_ref, w0_ref, a2_ref, a0_ref, kk_ref, ka_ref, seg_ref,
                      r_ref, v_ref, kkn_ref, lw_ref, kd_ref, bb_ref, *, tiles_per_seq, c):
    i = pl.program_id(0)
    first = (i % tiles_per_seq) == 0
    last = (i % tiles_per_seq) == tiles_per_seq - 1
    ma = _token_shift(pa_ref[...], pap_ref[...], pan_ref[...], mua_ref[...], first, last)
    mb = _token_shift(pb_ref[...], pbp_ref[...], pbn_ref[...], mub_ref[...], first, last)
    r = ma[:, :c]
    k = ma[:, c:2 * c]
    v = ma[:, 2 * c:]
    nl = 2 * W_LORA
    lw = jnp.tanh(mb[:, :nl])
    la = mb[:, nl:]
    w_raw = w0_ref[...] + _dot(lw.astype(BF16), w2_ref[...])
    a = _sigmoid(a0_ref[...] + _dot(la.astype(BF16), a2_ref[...]))
    logw = -math.exp(-0.5) * _sigmoid(w_raw)
    kk = k * kk_ref[...]
    kkn = kk * lax.rsqrt(_seg_sum(kk * kk, seg_ref[...]) + 1e-12)
    r_ref[...] = r
    v_ref[...] = v
    kkn_ref[...] = kkn
    for d in range(2):
        a_d = a[:, d * c:(d + 1) * c]
        lw_ref[d] = logw[:, d * c:(d + 1) * c]
        kd_ref[d] = k * (1.0 + (a_d - 1.0) * ka_ref[...])
        bb_ref[d] = kkn * a_d


def _rwkv_prep(proj, seq_len, mu, w0, w2, a0, a2, k_k, k_a, seg, tm):
    r_rows, ncols = proj.shape
    c = k_k.shape[-1]
    nb = ncols - 4 * c
    assert nb == 2 * (W_LORA + A_LORA) and (4 * c) % nb == 0 and seq_len % tm == 0
    tps = seq_len // tm
    t8 = tm // SUBLANES
    nblk8 = r_rows // SUBLANES
    bcol = (4 * c) // nb
    mua = mu[:3 * c].reshape(1, 3 * c)
    mub = mu[3 * c:].reshape(1, nb)
    zw = jnp.zeros((W_LORA, c), F32)
    w2cat = jnp.concatenate([jnp.concatenate([w2[0], zw], 1), jnp.concatenate([zw, w2[1]], 1)], 0).astype(BF16)
    a2cat = jnp.concatenate([jnp.concatenate([a2[0], zw], 1), jnp.concatenate([zw, a2[1]], 1)], 0).astype(BF16)
    w0cat = w0.reshape(1, 2 * c)
    a0cat = a0.reshape(1, 2 * c)
    full = lambda shape: pl.BlockSpec(shape, lambda i: (0,) * len(shape))
    prev_idx = lambda i: jnp.maximum(i * t8 - 1, 0)
    next_idx = lambda i: jnp.minimum((i + 1) * t8, nblk8 - 1)
    row_out = pl.BlockSpec((tm, c), lambda i: (i, 0))
    dir_out = pl.BlockSpec((2, tm, c), lambda i: (0, i, 0))
    kern = functools.partial(_rwkv_prep_kernel, tiles_per_seq=tps, c=c)
    return pl.pallas_call(
        kern,
        grid=(r_rows // tm,),
        in_specs=[pl.BlockSpec((tm, 3 * c), lambda i: (i, 0)),
                  pl.BlockSpec((SUBLANES, 3 * c), lambda i: (prev_idx(i), 0)),
                  pl.BlockSpec((SUBLANES, 3 * c), lambda i: (next_idx(i), 0)),
                  pl.BlockSpec((tm, nb), lambda i: (i, bcol)),
                  pl.BlockSpec((SUBLANES, nb), lambda i: (prev_idx(i), bcol)),
                  pl.BlockSpec((SUBLANES, nb), lambda i: (next_idx(i), bcol)),
                  full((1, 3 * c)), full((1, nb)),
                  full((2 * W_LORA, 2 * c)), full((1, 2 * c)),
                  full((2 * A_LORA, 2 * c)), full((1, 2 * c)),
                  full((1, c)), full((1, c)), full((c, c))],
        out_specs=[row_out, row_out, row_out, dir_out, dir_out, dir_out],
        out_shape=[jax.ShapeDtypeStruct((r_rows, c), F32)] * 3
                  + [jax.ShapeDtypeStruct((2, r_rows, c), F32)] * 3,
        compiler_params=_cparams(("arbitrary",)),
        name="rwkv_prep",
    )(proj, proj, proj, proj, proj, proj, mua, mub, w2cat, w0cat, a2cat, a0cat,
      k_k.reshape(1, c), k_a.reshape(1, c), seg)


def _head_pair_stack(xp, lane_a):
    return jnp.concatenate([jnp.where(lane_a, xp, 0.0), jnp.where(lane_a, 0.0, xp)], axis=0)


def _wkv_scan_kernel(r_ref, v_ref, kk_ref, lw_ref, kd_ref, bb_ref, s0_ref, o_ref, sfin_ref, h_scr,
                     *, chunk, n_pairs):
    L = chunk
    d = pl.program_id(1)
    ci = pl.program_id(2)
    nc = pl.num_programs(2)

    @pl.when(ci == 0)
    def _():
        h_scr[...] = s0_ref[...]

    sgn = 1 - 2 * d
    ti = lax.broadcasted_iota(jnp.int32, (L, L), 0)
    si = lax.broadcasted_iota(jnp.int32, (L, L), 1)
    tri = (((ti - si) * sgn) >= 0).astype(BF16)

    lw = lw_ref[...]
    hi = lw.astype(BF16)
    r1 = lw - hi.astype(F32)
    mid = r1.astype(BF16)
    lo = (r1 - mid.astype(F32)).astype(BF16)
    cum = _dot(tri, hi) + _dot(tri, mid) + _dot(tri, lo)
    tot = jnp.sum(lw, axis=0, keepdims=True)

    gam = jnp.exp(cum)
    gprev = jnp.exp(cum - lw)
    ginv = jnp.exp(-cum)
    gend = jnp.exp(tot - cum)
    kk = kk_ref[...]
    kd = kd_ref[...]
    bb = bb_ref[...]
    kt = kk * gprev
    rt = r_ref[...] * gam
    bt = bb * ginv
    kdt = kd * ginv
    khat = kd * gend
    bhat = bb * gend
    v = v_ref[...]

    p2 = 2 * L
    lane_a = lax.broadcasted_iota(jnp.int32, (L, LANES), 1) < RW_HEAD
    rt_i = lax.broadcasted_iota(jnp.int32, (p2, p2), 0) & (L - 1)
    cs_i = lax.broadcasted_iota(jnp.int32, (p2, p2), 1) & (L - 1)
    dts = (rt_i - cs_i) * sgn
    strict = dts > 0
    incl = dts >= 0
    eye = (lax.broadcasted_iota(jnp.int32, (p2, p2), 0)
           == lax.broadcasted_iota(jnp.int32, (p2, p2), 1)).astype(F32)

    for p in range(n_pairs):
        sl = slice(p * LANES, (p + 1) * LANES)
        stk = lambda x: _head_pair_stack(x[:, sl], lane_a)
        kt_s = stk(kt).astype(BF16)
        rt_s = stk(rt).astype(BF16)
        x_s = jnp.concatenate([kt_s, rt_s], axis=0)
        y_s = jnp.concatenate([stk(bt).astype(BF16), stk(kdt).astype(BF16)], axis=0)
        sc = _dot_nt(x_s, y_s)
        a_b = jnp.where(strict, sc[:p2, :p2], 0.0)
        a_k = jnp.where(strict, sc[:p2, p2:], 0.0)
        p_b = jnp.where(incl, sc[p2:, :p2], 0.0)
        p_k = jnp.where(incl, sc[p2:, p2:], 0.0)

        pw = -a_b
        tinv = eye + pw
        for _ in range(int(math.log2(L)) - 1):
            pwb = pw.astype(BF16)
            pw = _dot(pwb, pwb)
            tinv = _dot(tinv.astype(BF16), (eye + pw).astype(BF16))

        h = h_scr[p]
        xh = _dot(x_s, h.astype(BF16))
        v_s = stk(v).astype(BF16)
        y2 = xh[:p2] + _dot(a_k.astype(BF16), v_s)
        u = _dot(tinv.astype(BF16), y2.astype(BF16))
        u_b = u.astype(BF16)
        vu = jnp.concatenate([v_s, u_b], axis=0)
        o_s = xh[p2:] + _dot(jnp.concatenate([p_k, -p_b], axis=1).astype(BF16), vu)
        o_ref[:, sl] = o_s[:L] + o_s[L:]

        kb = jnp.concatenate([stk(khat), -stk(bhat)], axis=0).astype(BF16)
        tot_col = jnp.broadcast_to(tot[:, sl], (LANES, LANES)).T
        h_scr[p] = jnp.exp(tot_col) * h + _dot_tn(kb, vu)

    @pl.when(ci == nc - 1)
    def _():
        sfin_ref[...] = h_scr[...]


def _wkv_scan(r, v, kkn, logw, kd, bb, s0, chunk=SCAN_CHUNK):
    b, t, c = r.shape
    n_pairs = c // LANES
    assert t % chunk == 0 and 2 * RW_HEAD == LANES
    nc = t // chunk
    cidx = lambda d, ci: ci + d * (nc - 1 - 2 * ci)
    shared = pl.BlockSpec((None, chunk, c), lambda bi, d, ci: (bi, cidx(d, ci), 0))
    perdir = pl.BlockSpec((None, None, chunk, c), lambda bi, d, ci: (d, bi, cidx(d, ci), 0))
    state = pl.BlockSpec((None, None, n_pairs, LANES, LANES), lambda bi, d, ci: (bi, d, 0, 0, 0))
    kern = functools.partial(_wkv_scan_kernel, chunk=chunk, n_pairs=n_pairs)
    return pl.pallas_call(
        kern,
        grid=(b, 2, nc),
        in_specs=[shared, shared, shared, perdir, perdir, perdir, state],
        out_specs=[perdir, state],
        out_shape=[jax.ShapeDtypeStruct((2, b, t, c), F32),
                   jax.ShapeDtypeStruct((b, 2, n_pairs, LANES, LANES), F32)],
        scratch_shapes=[pltpu.VMEM((n_pairs, LANES, LANES), F32)],
        compiler_params=_cparams(("arbitrary", "arbitrary", "arbitrary")),
        name="wkv_scan",
    )(r, v, kkn, logw, kd, bb, s0)


def _rwkv_readout_kernel(o_ref, r_ref, v_ref, kd_ref, g_ref, x_ref, gate_ref,
                         rk_ref, gng_ref, gnb_ref, seg_ref, w_ref, out_ref):
    seg = seg_ref[...]
    inv_n = 1.0 / RW_HEAD
    o = o_ref[0] + o_ref[1]
    mean = _seg_sum(o, seg) * inv_n
    dlt = o - mean
    var = _seg_sum(dlt * dlt, seg) * inv_n
    y = dlt * lax.rsqrt(var + GN_EPS) * gng_ref[...] + gnb_ref[...]
    bonus = _seg_sum(r_ref[...] * (kd_ref[0] + kd_ref[1]) * rk_ref[...], seg) * v_ref[...]
    y = (y + bonus) * _silu(g_ref[...])
    out_ref[...] = x_ref[...] + gate_ref[...] * _dot(y.astype(BF16), w_ref[...])


def _rwkv_readout(o, r, v, kd, proj, x2d, gate, r_k, gn_g, gn_b, seg, w_out, rows_per_group, tm):
    r_rows, c = r.shape
    d = x2d.shape[1]
    tpg = rows_per_group // tm
    gcol = (3 * c) // c
    full = lambda shape: pl.BlockSpec(shape, lambda i: (0,) * len(shape))
    row = lambda n: pl.BlockSpec((tm, n), lambda i: (i, 0))
    dirs = pl.BlockSpec((2, tm, c), lambda i: (0, i, 0))
    return pl.pallas_call(
        _rwkv_readout_kernel,
        grid=(r_rows // tm,),
        in_specs=[dirs, row(c), row(c), dirs,
                  pl.BlockSpec((tm, c), lambda i: (i, gcol)),
                  row(d),
                  pl.BlockSpec((None, 1, d), lambda i: (i // tpg, 0, 0)),
                  full((1, c)), full((1, c)), full((1, c)), full((c, c)), full((c, d))],
        out_specs=row(d),
        out_shape=jax.ShapeDtypeStruct((r_rows, d), F32),
        compiler_params=_cparams(("arbitrary",)),
        name="rwkv_readout",
    )(o, r, v, kd, proj, x2d, gate, r_k.reshape(1, c), gn_g.reshape(1, c), gn_b.reshape(1, c), seg, w_out)


def _head_norm(x, gain, seg):
    ms = _seg_sum(x * x, seg) * (1.0 / DA_HEAD)
    return x * lax.rsqrt(ms + RMS_EPS) * gain


def _rope(x, cos, sin_signed):
    n = x.shape[-1]
    lane = lax.broadcasted_iota(jnp.int32, x.shape, 1)
    half0 = (lane & (2 * ROPE_FREQS - 1)) < ROPE_FREQS
    partner = jnp.where(half0, pltpu.roll(x, n - ROPE_FREQS, 1), pltpu.roll(x, ROPE_FREQS, 1))
    reps = n // cos.shape[-1]
    wide = lambda tab: jnp.concatenate([tab] * reps, axis=1)
    return x * wide(cos) + partner * wide(sin_signed)


def _qkv_prep_kernel(*refs, has_q, use_rope, q_scale):
    it = iter(refs)
    q_ref = next(it) if has_q else None
    k_ref = next(it)
    v_ref = next(it)
    cos_ref = next(it) if use_rope else None
    sin_ref = next(it) if use_rope else None
    qn_ref = next(it) if has_q else None
    kn_ref = next(it)
    seg_ref = next(it)
    qo_ref = next(it) if has_q else None
    ko_ref = next(it)
    vo_ref = next(it)
    seg = seg_ref[...]
    k = _head_norm(k_ref[...], kn_ref[...], seg)
    if use_rope:
        k = _rope(k, cos_ref[...], sin_ref[...])
    ko_ref[...] = k.astype(BF16)
    vo_ref[...] = v_ref[...].astype(BF16)
    if has_q:
        q = _head_norm(q_ref[...], qn_ref[...], seg)
        if use_rope:
            q = _rope(q, cos_ref[...], sin_ref[...])
        qo_ref[...] = (q * q_scale).astype(BF16)


def _qkv_prep(proj, col0, has_q, cos, sin_signed, qn, kn, seg, seq_len, tm):
    r_rows = proj.shape[0]
    w = seg.shape[0]
    use_rope = cos is not None
    tps = seq_len // tm
    full = lambda shape: pl.BlockSpec(shape, lambda i: (0,) * len(shape))
    col = lambda j: pl.BlockSpec((tm, w), lambda i: (i, j))
    tab = pl.BlockSpec((tm, LANES), lambda i: (i % tps, 0))
    in_specs, args = [], []
    ncol = col0
    if has_q:
        in_specs.append(col(ncol)); args.append(proj); ncol += 1
    in_specs += [col(ncol), col(ncol + 1)]
    args += [proj, proj]
    if use_rope:
        in_specs += [tab, tab]
        args += [cos, sin_signed]
    tile_gain = lambda g: jnp.tile(g, w // g.shape[0]).reshape(1, w)
    if has_q:
        in_specs.append(full((1, w))); args.append(tile_gain(qn))
    in_specs += [full((1, w)), full((w, w))]
    args += [tile_gain(kn), seg]
    n_out = 3 if has_q else 2
    kern = functools.partial(_qkv_prep_kernel, has_q=has_q, use_rope=use_rope, q_scale=DA_HEAD ** -0.5)
    return pl.pallas_call(
        kern,
        grid=(r_rows // tm,),
        in_specs=in_specs,
        out_specs=[pl.BlockSpec((tm, w), lambda i: (i, 0))] * n_out,
        out_shape=[jax.ShapeDtypeStruct((r_rows, w), BF16)] * n_out,
        compiler_params=_cparams(("arbitrary",)),
        name="qkv_prep_q" if has_q else "qkv_prep_ctx",
    )(*args)


def _diff_attn_kernel(q_ref, k_ref, vt_ref, lam_ref, sub_ref, o_ref, m_scr, l_scr, acc_scr, *, lam_init, tk):
    nk = k_ref.shape[0] // tk
    q = q_ref[...]
    lane = lax.broadcasted_iota(jnp.int32, q.shape, 1)
    zero = jnp.zeros_like(q)
    qm = (jnp.where(lane < DA_HEAD, q, zero), jnp.where(lane < DA_HEAD, zero, q))
    m_scr[...] = jnp.full(m_scr.shape, -jnp.inf, F32)
    l_scr[...] = jnp.zeros(l_scr.shape, F32)
    acc_scr[...] = jnp.zeros(acc_scr.shape, F32)

    def body(j, carry):
        off = pl.multiple_of(j * tk, tk)
        k = k_ref[pl.ds(off, tk), :]
        vt = vt_ref[:, pl.ds(off, tk)]
        for c in range(2):
            s = _dot_nt(k, qm[c])
            m_prev = m_scr[c]
            m_new = jnp.maximum(m_prev, jnp.max(s, axis=0, keepdims=True))
            alpha = jnp.exp(m_prev - m_new)
            p = jnp.exp(s - m_new)
            l_scr[c] = alpha * l_scr[c] + jnp.sum(p, axis=0, keepdims=True)
            acc_scr[c] = alpha * acc_scr[c] + _dot(vt, p.astype(BF16))
            m_scr[c] = m_new
        return carry

    lax.fori_loop(0, nk, body, 0)

    lv = lam_ref[...]
    lam = (jnp.exp(jnp.sum(lv[0:1] * lv[1:2], axis=-1, keepdims=True))
           - jnp.exp(jnp.sum(lv[2:3] * lv[3:4], axis=-1, keepdims=True)) + lam_init)
    o = acc_scr[0] / l_scr[0] - lam * (acc_scr[1] / l_scr[1])
    y = o * lax.rsqrt(jnp.mean(o * o, axis=0, keepdims=True) + SUBLN_EPS)
    o_ref[...] = (y * (sub_ref[...] * (1.0 - lam_init))).T


def _diff_attn_bounded_kernel(q_ref, k_ref, vt_ref, lam_ref, sub_ref, o_ref, m_scr, l_scr, acc_scr,
                              *, lam_init, tk):
    del m_scr
    nk = k_ref.shape[0] // tk
    q = q_ref[...]
    lane = lax.broadcasted_iota(jnp.int32, q.shape, 1)
    zero = jnp.zeros_like(q)
    qm = (jnp.where(lane < DA_HEAD, q, zero), jnp.where(lane < DA_HEAD, zero, q))
    l_scr[...] = jnp.zeros(l_scr.shape, F32)
    acc_scr[...] = jnp.zeros(acc_scr.shape, F32)

    def body(j, carry):
        off = pl.multiple_of(j * tk, tk)
        k = k_ref[pl.ds(off, tk), :]
        vt = vt_ref[:, pl.ds(off, tk)]
        for c in range(2):
            p = jnp.exp(_dot_nt(k, qm[c]))
            l_scr[c] += jnp.sum(p, axis=0, keepdims=True)
            acc_scr[c] += _dot(vt, p.astype(BF16))
        return carry

    lax.fori_loop(0, nk, body, 0)

    lv = lam_ref[...]
    lam = (jnp.exp(jnp.sum(lv[0:1] * lv[1:2], axis=-1, keepdims=True))
           - jnp.exp(jnp.sum(lv[2:3] * lv[3:4], axis=-1, keepdims=True)) + lam_init)
    o = acc_scr[0] / l_scr[0] - lam * (acc_scr[1] / l_scr[1])
    y = o * lax.rsqrt(jnp.mean(o * o, axis=0, keepdims=True) + SUBLN_EPS)
    o_ref[...] = (y * (sub_ref[...] * (1.0 - lam_init))).T


def _diff_attn(q, k_all, vt_all, lam_vecs, subln, lam_init, tq, tk, bounded):
    b, t, w = q.shape
    lk = k_all.shape[1]
    nh = w // DA_VHEAD
    assert t % tq == 0 and lk % tk == 0
    body = _diff_attn_bounded_kernel if bounded else _diff_attn_kernel
    kern = functools.partial(body, lam_init=lam_init, tk=tk)
    return pl.pallas_call(
        kern,
        grid=(b, nh, t // tq),
        in_specs=[pl.BlockSpec((None, tq, DA_VHEAD), lambda bi, h, i: (bi, i, h)),
                  pl.BlockSpec((None, lk, DA_VHEAD), lambda bi, h, i: (bi, 0, h)),
                  pl.BlockSpec((None, DA_VHEAD, lk), lambda bi, h, i: (bi, h, 0)),
                  pl.BlockSpec((4, DA_HEAD), lambda bi, h, i: (0, 0)),
                  pl.BlockSpec((DA_VHEAD, 1), lambda bi, h, i: (0, 0))],
        out_specs=pl.BlockSpec((None, tq, DA_VHEAD), lambda bi, h, i: (bi, i, h)),
        out_shape=jax.ShapeDtypeStruct((b, t, w), F32),
        scratch_shapes=[pltpu.VMEM((2, 1, tq), F32), pltpu.VMEM((2, 1, tq), F32),
                        pltpu.VMEM((2, DA_VHEAD, tq), F32)],
        compiler_params=_cparams(("arbitrary", "arbitrary", "arbitrary")),
        name="diff_attn_bounded" if bounded else "diff_attn",
    )(q, k_all, vt_all, lam_vecs, subln.reshape(DA_VHEAD, 1))


def _gated_out_kernel(y_ref, g_ref, x_ref, gate_ref, w_ref, out_ref):
    y = y_ref[...] * _silu(g_ref[...])
    out_ref[...] = x_ref[...] + gate_ref[...] * _dot(y.astype(BF16), w_ref[...])


def _gated_out(y, proj, gcol, x2d, gate, w_out, rows_per_group, tm):
    r_rows, w = y.shape
    d = x2d.shape[1]
    tpg = rows_per_group // tm
    row = lambda n: pl.BlockSpec((tm, n), lambda i: (i, 0))
    return pl.pallas_call(
        _gated_out_kernel,
        grid=(r_rows // tm,),
        in_specs=[row(w), pl.BlockSpec((tm, w), lambda i: (i, gcol)), row(d),
                  pl.BlockSpec((None, 1, d), lambda i: (i // tpg, 0, 0)),
                  pl.BlockSpec((w, d), lambda i: (0, 0))],
        out_specs=row(d),
        out_shape=jax.ShapeDtypeStruct((r_rows, d), F32),
        compiler_params=_cparams(("arbitrary",)),
        name="gated_out",
    )(y, proj, x2d, gate, w_out)


def _segment_matrix(n, width):
    i = lax.broadcasted_iota(jnp.int32, (n, n), 0) // width
    j = lax.broadcasted_iota(jnp.int32, (n, n), 1) // width
    return (i == j).astype(BF16)


def _rope_tables(rows):
    row_ids = jnp.repeat(jnp.arange(rows), GRID_W).astype(F32)
    col_ids = jnp.tile(jnp.arange(GRID_W), rows).astype(F32)
    inv_freq = ROPE_THETA ** (-jnp.arange(ROPE_FREQS, dtype=F32) / ROPE_FREQS)
    ang_r = row_ids[:, None] * inv_freq
    ang_c = col_ids[:, None] * inv_freq
    cr, sr, cc, sc = jnp.cos(ang_r), jnp.sin(ang_r), jnp.cos(ang_c), jnp.sin(ang_c)
    cos = jnp.concatenate([cr, cr, cc, cc], axis=-1)
    sin_signed = jnp.concatenate([-sr, sr, -sc, sc], axis=-1)
    return jnp.tile(cos, (1, 2)), jnp.tile(sin_signed, (1, 2))


def kernel(x, c, ctx, c_ctx, ada_w, ada_b, norm_g, rw_in, rw_mu, rw_w0, rw_w2, rw_a0, rw_a2, rw_kk, rw_ka, rw_rk, rw_gn_g, rw_gn_b, rw_out, da_in, da_qn, da_kn, da_lam, da_subln, da_out):
    bsz, t, d = x.shape
    lc = ctx.shape[1]
    depth = ada_w.shape[0]
    assert depth == 2 and bsz + 1 <= SUBLANES
    cw = rw_kk.shape[-1]
    x2 = x.reshape(bsz * t, d)
    xc2 = ctx.reshape(bsz * lc, d)

    rows = jnp.zeros((SUBLANES, d), F32).at[:bsz].set(c).at[bsz].set(c_ctx)
    mod = _ada_mod(rows, ada_w, ada_b)
    shift, scale, gate = mod[:, :, :d], mod[:, :, d:2 * d], mod[:, :, 2 * d:]
    mult = norm_g[:, None, :] * (1.0 + scale)
    lat = lambda a, i: a[i, :bsz].reshape(bsz, 1, d)
    con = lambda a, i: a[i, bsz:bsz + 1].reshape(1, 1, d)

    tm_l = 256
    tm_c = min(256, lc)

    w_in0 = rw_in[0].astype(BF16)
    proj_l = _normmod_mm(x2, lat(mult, 0), lat(shift, 0), w_in0, t, tm_l)
    proj_c = _normmod_mm(xc2, con(mult, 0), con(shift, 0), w_in0, bsz * lc, tm_c)
    seg64 = _segment_matrix(cw, RW_HEAD)
    prep = functools.partial(_rwkv_prep, mu=rw_mu[0], w0=rw_w0[0], w2=rw_w2[0], a0=rw_a0[0], a2=rw_a2[0],
                             k_k=rw_kk[0], k_a=rw_ka[0], seg=seg64)
    r_l, v_l, kk_l, lw_l, kd_l, bb_l = prep(proj_l, t, tm=tm_l)
    r_c, v_c, kk_c, lw_c, kd_c, bb_c = prep(proj_c, lc, tm=tm_c)

    s0 = jnp.zeros((bsz, 2, cw // LANES, LANES, LANES), F32)
    b3 = lambda a, n: a.reshape(bsz, n, cw)
    b4 = lambda a, n: a.reshape(2, bsz, n, cw)
    o_c, s_c = _wkv_scan(b3(r_c, lc), b3(v_c, lc), b3(kk_c, lc), b4(lw_c, lc), b4(kd_c, lc), b4(bb_c, lc), s0)
    o_l, _ = _wkv_scan(b3(r_l, t), b3(v_l, t), b3(kk_l, t), b4(lw_l, t), b4(kd_l, t), b4(bb_l, t), s_c)

    w_out0 = rw_out[0].astype(BF16)
    readout = functools.partial(_rwkv_readout, r_k=rw_rk[0].reshape(-1), gn_g=rw_gn_g[0], gn_b=rw_gn_b[0],
                                seg=seg64, w_out=w_out0)
    x1 = readout(o_l.reshape(2, bsz * t, cw), r_l, v_l, kd_l, proj_l, x2, lat(gate, 0),
                 rows_per_group=t, tm=tm_l)
    xc1 = readout(o_c.reshape(2, bsz * lc, cw), r_c, v_c, kd_c, proj_c, xc2, con(gate, 0),
                  rows_per_group=bsz * lc, tm=tm_c)

    qw = DA_HEADS * 2 * DA_HEAD
    w_in1 = da_in[0].astype(BF16)
    proj2_l = _normmod_mm(x1, lat(mult, 1), lat(shift, 1), w_in1, t, tm_l)
    proj2_c = _normmod_mm(xc1, con(mult, 1), con(shift, 1), w_in1[:, qw:3 * qw], bsz * lc, tm_c)
    seg_da = _segment_matrix(qw, DA_HEAD)
    cos, sin_signed = _rope_tables(t // GRID_W)
    q_b, k_b, v_b = _qkv_prep(proj2_l, 0, True, cos, sin_signed, da_qn[0], da_kn[0], seg_da, t, tm_l)
    kc_b, vc_b = _qkv_prep(proj2_c, 0, False, None, None, None, da_kn[0], seg_da, lc, tm_c)
    k_all = jnp.concatenate([kc_b.reshape(bsz, lc, qw), k_b.reshape(bsz, t, qw)], axis=1)
    v_all = jnp.concatenate([vc_b.reshape(bsz, lc, qw), v_b.reshape(bsz, t, qw)], axis=1)
    lam_init = 0.8 - 0.6 * math.exp(-0.3 * 1)
    lk = lc + t
    tk = max(n for n in range(LANES, 1024 + 1, LANES) if lk % n == 0)
    vt_all = jnp.swapaxes(v_all, 1, 2)
    attn = functools.partial(_diff_attn, lam_vecs=da_lam[0], subln=da_subln[0], lam_init=lam_init,
                             tq=min(1024, t), tk=tk)
    score_bound = DA_HEAD ** 0.5 * jnp.max(jnp.abs(da_qn[0])) * jnp.max(jnp.abs(da_kn[0]))
    y_att = lax.cond(score_bound < MAX_UNSHIFTED_SCORE,
                     functools.partial(attn, bounded=True), functools.partial(attn, bounded=False),
                     q_b.reshape(bsz, t, qw), k_all, vt_all)
    out = _gated_out(y_att.reshape(bsz * t, qw), proj2_l, 3, x1, lat(gate, 1), da_out[0].astype(BF16),
                     rows_per_group=t, tm=tm_l)
    return out.reshape(bsz, t, d)
```

```python
import functools
import math

import jax
import jax.numpy as jnp
from jax import lax
from jax.experimental import pallas as pl
from jax.experimental.pallas import tpu as pltpu

F32 = jnp.float32
BF16 = jnp.bfloat16

GRID_W = 64
RMS_EPS = 1e-6
RW_HEAD = 64
W_LORA = 64
A_LORA = 64
GN_EPS = 64e-5
DA_HEADS = 8
DA_HEAD = 64
DA_VHEAD = 2 * DA_HEAD
SUBLN_EPS = 1e-5
ROPE_THETA = 10000.0
ROPE_FREQS = DA_HEAD // 4

LANES = 128
SUBLANES = 8
SCAN_CHUNK = 64
VMEM_LIMIT = 56 * 1024 * 1024
MAX_UNSHIFTED_SCORE = 40.0


def _cparams(sem):
    return pltpu.CompilerParams(dimension_semantics=sem, vmem_limit_bytes=VMEM_LIMIT)


def _dot(a, b):
    return jnp.dot(a, b, preferred_element_type=F32)


def _dot_nt(a, b):
    return lax.dot_general(a, b, (((1,), (1,)), ((), ())), preferred_element_type=F32)


def _dot_tn(a, b):
    return lax.dot_general(a, b, (((0,), (0,)), ((), ())), preferred_element_type=F32)


def _split2(x):
    hi = x.astype(BF16)
    lo = (x - hi.astype(F32)).astype(BF16)
    return hi, lo


def _seg_sum(x, seg):
    hi, lo = _split2(x)
    return _dot(hi, seg) + _dot(lo, seg)


def _sigmoid(x):
    return 1.0 / (1.0 + jnp.exp(-x))


def _silu(x):
    return x * _sigmoid(x)


def _ada_kernel(c_ref, w_ref, b_ref, o_ref):
    s = _silu(c_ref[...])
    o_ref[...] = jnp.dot(s, w_ref[...], preferred_element_type=F32,
                         precision=lax.Precision.HIGHEST) + b_ref[...]


def _ada_mod(rows, ada_w, ada_b):
    depth, d, n = ada_w.shape
    tn = 768
    return pl.pallas_call(
        _ada_kernel,
        grid=(depth, n // tn),
        in_specs=[pl.BlockSpec((SUBLANES, d), lambda i, j: (0, 0)),
                  pl.BlockSpec((None, d, tn), lambda i, j: (i, 0, j)),
                  pl.BlockSpec((None, 1, tn), lambda i, j: (i, 0, j))],
        out_specs=pl.BlockSpec((None, SUBLANES, tn), lambda i, j: (i, 0, j)),
        out_shape=jax.ShapeDtypeStruct((depth, SUBLANES, n), F32),
        compiler_params=_cparams(("arbitrary", "arbitrary")),
        name="ada_mod",
    )(rows, ada_w, ada_b.reshape(depth, 1, n))


def _normmod_mm_kernel(x_ref, mult_ref, shift_ref, w_ref, o_ref):
    x = x_ref[...]
    y = x * lax.rsqrt(jnp.mean(x * x, axis=-1, keepdims=True) + RMS_EPS)
    h = y * mult_ref[...] + shift_ref[...]
    o_ref[...] = _dot(h.astype(BF16), w_ref[...])


def _normmod_mm(x2d, mult, shift, w, rows_per_group, tm):
    r, d = x2d.shape
    n = w.shape[1]
    tpg = rows_per_group // tm
    return pl.pallas_call(
        _normmod_mm_kernel,
        grid=(r // tm,),
        in_specs=[pl.BlockSpec((tm, d), lambda i: (i, 0)),
                  pl.BlockSpec((None, 1, d), lambda i: (i // tpg, 0, 0)),
                  pl.BlockSpec((None, 1, d), lambda i: (i // tpg, 0, 0)),
                  pl.BlockSpec((d, n), lambda i: (0, 0))],
        out_specs=pl.BlockSpec((tm, n), lambda i: (i, 0)),
        out_shape=jax.ShapeDtypeStruct((r, n), F32),
        compiler_params=_cparams(("arbitrary",)),
        name="normmod_mm",
    )(x2d, mult, shift, w)


def _token_shift(u, prev8, next8, mu, first, last):
    tm = u.shape[0]
    row = lax.broadcasted_iota(jnp.int32, u.shape, 0)
    prow = jnp.where(first, 0.0, prev8[SUBLANES - 1:SUBLANES, :])
    nrow = jnp.where(last, 0.0, next8[0:1, :])
    prev = jnp.where(row == 0, prow, pltpu.roll(u, 1, 0))
    nxt = jnp.where(row == tm - 1, nrow, pltpu.roll(u, tm - 1, 0))
    return u + mu * (0.5 * (prev + nxt) - u)


def _rwkv_prep_kernel(pa_ref, pap_ref, pan_ref, pb_ref, pbp_ref, pbn_ref,
                      mua_ref, mub_ref, w2_ref, w0_ref, a2_ref, a0_ref, kk_ref, ka_ref, seg_ref,
                      r_ref, v_ref, kkn_ref, lw_ref, kd_ref, bb_ref, *, tiles_per_seq, c):
    i = pl.program_id(0)
    first = (i % tiles_per_seq) == 0
    last = (i % tiles_per_seq) == tiles_per_seq - 1
    ma = _token_shift(pa_ref[...], pap_ref[...], pan_ref[...], mua_ref[...], first, last)
    mb = _token_shift(pb_ref[...], pbp_ref[...], pbn_ref[...], mub_ref[...], first, last)
    r = ma[:, :c]
    k = ma[:, c:2 * c]
    v = ma[:, 2 * c:]
    nl = 2 * W_LORA
    lw = jnp.tanh(mb[:, :nl])
    la = mb[:, nl:]
    w_raw = w0_ref[...] + _dot(lw.astype(BF16), w2_ref[...])
    a = _sigmoid(a0_ref[...] + _dot(la.astype(BF16), a2_ref[...]))
    logw = -math.exp(-0.5) * _sigmoid(w_raw)
    kk = k * kk_ref[...]
    kkn = kk * lax.rsqrt(_seg_sum(kk * kk, seg_ref[...]) + 1e-12)
    r_ref[...] = r
    v_ref[...] = v
    kkn_ref[...] = kkn
    for d in range(2):
        a_d = a[:, d * c:(d + 1) * c]
        lw_ref[d] = logw[:, d * c:(d + 1) * c]
        kd_ref[d] = k * (1.0 + (a_d - 1.0) * ka_ref[...])
        bb_ref[d] = kkn * a_d


def _rwkv_prep(proj, seq_len, mu, w0, w2, a0, a2, k_k, k_a, seg, tm):
    r_rows, ncols = proj.shape
    c = k_k.shape[-1]
    nb = ncols - 4 * c
    assert nb == 2 * (W_LORA + A_LORA) and (4 * c) % nb == 0 and seq_len % tm == 0
    tps = seq_len // tm
    t8 = tm // SUBLANES
    nblk8 = r_rows // SUBLANES
    bcol = (4 * c) // nb
    mua = mu[:3 * c].reshape(1, 3 * c)
    mub = mu[3 * c:].reshape(1, nb)
    zw = jnp.zeros((W_LORA, c), F32)
    w2cat = jnp.concatenate([jnp.concatenate([w2[0], zw], 1), jnp.concatenate([zw, w2[1]], 1)], 0).astype(BF16)
    a2cat = jnp.concatenate([jnp.concatenate([a2[0], zw], 1), jnp.concatenate([zw, a2[1]], 1)], 0).astype(BF16)
    w0cat = w0.reshape(1, 2 * c)
    a0cat = a0.reshape(1, 2 * c)
    full = lambda shape: pl.BlockSpec(shape, lambda i: (0,) * len(shape))
    prev_idx = lambda i: jnp.maximum(i * t8 - 1, 0)
    next_idx = lambda i: jnp.minimum((i + 1) * t8, nblk8 - 1)
    row_out = pl.BlockSpec((tm, c), lambda i: (i, 0))
    dir_out = pl.BlockSpec((2, tm, c), lambda i: (0, i, 0))
    kern = functools.partial(_rwkv_prep_kernel, tiles_per_seq=tps, c=c)
    return pl.pallas_call(
        kern,
        grid=(r_rows // tm,),
        in_specs=[pl.BlockSpec((tm, 3 * c), lambda i: (i, 0)),
                  pl.BlockSpec((SUBLANES, 3 * c), lambda i: (prev_idx(i), 0)),
                  pl.BlockSpec((SUBLANES, 3 * c), lambda i: (next_idx(i), 0)),
                  pl.BlockSpec((tm, nb), lambda i: (i, bcol)),
                  pl.BlockSpec((SUBLANES, nb), lambda i: (prev_idx(i), bcol)),
                  pl.BlockSpec((SUBLANES, nb), lambda i: (next_idx(i), bcol)),
                  full((1, 3 * c)), full((1, nb)),
                  full((2 * W_LORA, 2 * c)), full((1, 2 * c)),
                  full((2 * A_LORA, 2 * c)), full((1, 2 * c)),
                  full((1, c)), full((1, c)), full((c, c))],
        out_specs=[row_out, row_out, row_out, dir_out, dir_out, dir_out],
        out_shape=[jax.ShapeDtypeStruct((r_rows, c), F32)] * 3
                  + [jax.ShapeDtypeStruct((2, r_rows, c), F32)] * 3,
        compiler_params=_cparams(("arbitrary",)),
        name="rwkv_prep",
    )(proj, proj, proj, proj, proj, proj, mua, mub, w2cat, w0cat, a2cat, a0cat,
      k_k.reshape(1, c), k_a.reshape(1, c), seg)


def _head_pair_stack(xp, lane_a):
    return jnp.concatenate([jnp.where(lane_a, xp, 0.0), jnp.where(lane_a, 0.0, xp)], axis=0)


def _wkv_chunk_operands(r_ref, v_ref, kk_ref, lw_ref, kd_ref, bb_ref, reverse, L):
    ti = lax.broadcasted_iota(jnp.int32, (L, L), 0)
    si = lax.broadcasted_iota(jnp.int32, (L, L), 1)
    tri = ((ti <= si) if reverse else (ti >= si)).astype(BF16)
    lw = lw_ref[...]
    hi = lw.astype(BF16)
    r1 = lw - hi.astype(F32)
    mid = r1.astype(BF16)
    lo = (r1 - mid.astype(F32)).astype(BF16)
    cum = _dot(tri, hi) + _dot(tri, mid) + _dot(tri, lo)
    tot = jnp.sum(lw, axis=0, keepdims=True)
    ginv = jnp.exp(-cum)
    gend = jnp.exp(tot - cum)
    kd = kd_ref[...]
    bb = bb_ref[...]
    return dict(kt=kk_ref[...] * jnp.exp(cum - lw), rt=r_ref[...] * jnp.exp(cum), bt=bb * ginv,
                kdt=kd * ginv, khat=kd * gend, bhat=bb * gend, v=v_ref[...], tot=tot)


def _wkv_scan_kernel(rf_ref, vf_ref, kkf_ref, lwf_ref, kdf_ref, bbf_ref,
                     rb_ref, vb_ref, kkb_ref, lwb_ref, kdb_ref, bbb_ref, s0_ref,
                     of_ref, ob_ref, sfin_ref, h_scr, *, chunk, n_pairs):
    L = chunk
    ci = pl.program_id(1)
    nc = pl.num_programs(1)

    @pl.when(ci == 0)
    def _():
        h_scr[...] = s0_ref[...]

    ops = (_wkv_chunk_operands(rf_ref, vf_ref, kkf_ref, lwf_ref, kdf_ref, bbf_ref, False, L),
           _wkv_chunk_operands(rb_ref, vb_ref, kkb_ref, lwb_ref, kdb_ref, bbb_ref, True, L))
    o_refs = (of_ref, ob_ref)

    p2 = 2 * L
    lane_a = lax.broadcasted_iota(jnp.int32, (L, LANES), 1) < RW_HEAD
    dts = ((lax.broadcasted_iota(jnp.int32, (p2, p2), 0) & (L - 1))
           - (lax.broadcasted_iota(jnp.int32, (p2, p2), 1) & (L - 1)))
    strict = (dts > 0, dts < 0)
    incl = (dts >= 0, dts <= 0)
    eye = (lax.broadcasted_iota(jnp.int32, (p2, p2), 0)
           == lax.broadcasted_iota(jnp.int32, (p2, p2), 1)).astype(F32)

    chains = [(d, p) for p in range(n_pairs) for d in range(2)]
    lanes_of = lambda p: slice(p * LANES, (p + 1) * LANES)
    stk = lambda name, d, p: _head_pair_stack(ops[d][name][:, lanes_of(p)], lane_a)
    kt_s = [stk("kt", d, p).astype(BF16) for d, p in chains]
    rt_s = [stk("rt", d, p).astype(BF16) for d, p in chains]
    y_s = [jnp.concatenate([stk("bt", d, p).astype(BF16), stk("kdt", d, p).astype(BF16)], axis=0)
           for d, p in chains]
    v_s = [stk("v", d, p).astype(BF16) for d, p in chains]
    h = [h_scr[d, p] for d, p in chains]
    h_b = [x.astype(BF16) for x in h]
    n = range(len(chains))
    sc = [_dot_nt(jnp.concatenate([kt_s[i], rt_s[i]], axis=0), y_s[i]) for i in n]
    rh = [_dot(rt_s[i], h_b[i]) for i in n]
    msk = lambda m, x: jnp.where(m, x, 0.0)
    pw = [-msk(strict[d], sc[i][:p2, :p2]) for i, (d, p) in enumerate(chains)]
    a_k = [msk(strict[d], sc[i][:p2, p2:]).astype(BF16) for i, (d, p) in enumerate(chains)]
    pkb = [jnp.concatenate([msk(incl[d], sc[i][p2:, p2:]), -msk(incl[d], sc[i][p2:, :p2])],
                           axis=1).astype(BF16) for i, (d, p) in enumerate(chains)]
    u = [_dot(jnp.concatenate([kt_s[i], a_k[i]], axis=1), jnp.concatenate([h_b[i], v_s[i]], axis=0))
         for i in n]

    pwb = [x.astype(BF16) for x in pw]
    tinv = [eye + pw[i] for i in n]
    pw = [_dot(pwb[i], pwb[i]) for i in n]
    for j in range(1, int(math.log2(L))):
        pwb = [x.astype(BF16) for x in pw]
        if j < int(math.log2(L)) - 1:
            w = [_dot(jnp.concatenate([tinv[i].astype(BF16), pwb[i]], axis=0), pwb[i]) for i in n]
            tinv = [tinv[i] + w[i][:p2] for i in n]
            pw = [w[i][p2:] for i in n]
        else:
            tinv = [tinv[i] + _dot(tinv[i].astype(BF16), pwb[i]) for i in n]
    u = [_dot(tinv[i].astype(BF16), u[i].astype(BF16)) for i in n]

    vu = [jnp.concatenate([v_s[i], u[i].astype(BF16)], axis=0) for i in n]
    for i, (d, p) in enumerate(chains):
        o_s = rh[i] + _dot(pkb[i], vu[i])
        o_refs[d][:, lanes_of(p)] = o_s[:L] + o_s[L:]
    for i, (d, p) in enumerate(chains):
        kb = jnp.concatenate([stk("khat", d, p), -stk("bhat", d, p)], axis=0).astype(BF16)
        tot_col = jnp.broadcast_to(ops[d]["tot"][:, lanes_of(p)], (LANES, LANES)).T
        h_scr[d, p] = jnp.exp(tot_col) * h[i] + _dot_tn(kb, vu[i])

    @pl.when(ci == nc - 1)
    def _():
        sfin_ref[...] = h_scr[...]


def _wkv_scan(r, v, kkn, logw, kd, bb, s0, chunk=SCAN_CHUNK):
    b, t, c = r.shape
    n_pairs = c // LANES
    assert t % chunk == 0 and 2 * RW_HEAD == LANES
    nc = t // chunk
    fwd = lambda ci: ci
    bwd = lambda ci: nc - 1 - ci
    shared = lambda cmap: pl.BlockSpec((None, chunk, c), lambda bi, ci: (bi, cmap(ci), 0))
    perdir = lambda d, cmap: pl.BlockSpec((None, None, chunk, c), lambda bi, ci: (d, bi, cmap(ci), 0))
    state = pl.BlockSpec((None, 2, n_pairs, LANES, LANES), lambda bi, ci: (bi, 0, 0, 0, 0))
    kern = functools.partial(_wkv_scan_kernel, chunk=chunk, n_pairs=n_pairs)
    return pl.pallas_call(
        kern,
        grid=(b, nc),
        in_specs=[shared(fwd)] * 3 + [perdir(0, fwd)] * 3 + [shared(bwd)] * 3 + [perdir(1, bwd)] * 3 + [state],
        out_specs=[shared(fwd), shared(bwd), state],
        out_shape=[jax.ShapeDtypeStruct((b, t, c), F32), jax.ShapeDtypeStruct((b, t, c), F32),
                   jax.ShapeDtypeStruct((b, 2, n_pairs, LANES, LANES), F32)],
        scratch_shapes=[pltpu.VMEM((2, n_pairs, LANES, LANES), F32)],
        compiler_params=_cparams(("arbitrary", "arbitrary")),
        name="wkv_scan",
    )(r, v, kkn, logw, kd, bb, r, v, kkn, logw, kd, bb, s0)


def _rwkv_readout_kernel(of_ref, ob_ref, r_ref, v_ref, kd_ref, g_ref, x_ref, gate_ref,
                         rk_ref, gng_ref, gnb_ref, seg_ref, w_ref, out_ref):
    seg = seg_ref[...]
    inv_n = 1.0 / RW_HEAD
    o = of_ref[...] + ob_ref[...]
    mean = _seg_sum(o, seg) * inv_n
    dlt = o - mean
    var = _seg_sum(dlt * dlt, seg) * inv_n
    y = dlt * lax.rsqrt(var + GN_EPS) * gng_ref[...] + gnb_ref[...]
    bonus = _seg_sum(r_ref[...] * (kd_ref[0] + kd_ref[1]) * rk_ref[...], seg) * v_ref[...]
    y = (y + bonus) * _silu(g_ref[...])
    out_ref[...] = x_ref[...] + gate_ref[...] * _dot(y.astype(BF16), w_ref[...])


def _rwkv_readout(o_f, o_b, r, v, kd, proj, x2d, gate, r_k, gn_g, gn_b, seg, w_out, rows_per_group, tm):
    r_rows, c = r.shape
    d = x2d.shape[1]
    tpg = rows_per_group // tm
    gcol = (3 * c) // c
    full = lambda shape: pl.BlockSpec(shape, lambda i: (0,) * len(shape))
    row = lambda n: pl.BlockSpec((tm, n), lambda i: (i, 0))
    dirs = pl.BlockSpec((2, tm, c), lambda i: (0, i, 0))
    return pl.pallas_call(
        _rwkv_readout_kernel,
        grid=(r_rows // tm,),
        in_specs=[row(c), row(c), row(c), row(c), dirs,
                  pl.BlockSpec((tm, c), lambda i: (i, gcol)),
                  row(d),
                  pl.BlockSpec((None, 1, d), lambda i: (i // tpg, 0, 0)),
                  full((1, c)), full((1, c)), full((1, c)), full((c, c)), full((c, d))],
        out_specs=row(d),
        out_shape=jax.ShapeDtypeStruct((r_rows, d), F32),
        compiler_params=_cparams(("arbitrary",)),
        name="rwkv_readout",
    )(o_f, o_b, r, v, kd, proj, x2d, gate, r_k.reshape(1, c), gn_g.reshape(1, c), gn_b.reshape(1, c), seg,
      w_out)


def _head_norm(x, gain, seg):
    ms = _seg_sum(x * x, seg) * (1.0 / DA_HEAD)
    return x * lax.rsqrt(ms + RMS_EPS) * gain


def _rope(x, cos, sin_signed):
    n = x.shape[-1]
    lane = lax.broadcasted_iota(jnp.int32, x.shape, 1)
    half0 = (lane & (2 * ROPE_FREQS - 1)) < ROPE_FREQS
    partner = jnp.where(half0, pltpu.roll(x, n - ROPE_FREQS, 1), pltpu.roll(x, ROPE_FREQS, 1))
    reps = n // cos.shape[-1]
    wide = lambda tab: jnp.concatenate([tab] * reps, axis=1)
    return x * wide(cos) + partner * wide(sin_signed)


def _qkv_prep_kernel(*refs, has_q, use_rope, q_scale):
    it = iter(refs)
    q_ref = next(it) if has_q else None
    k_ref = next(it)
    v_ref = next(it)
    cos_ref = next(it) if use_rope else None
    sin_ref = next(it) if use_rope else None
    qn_ref = next(it) if has_q else None
    kn_ref = next(it)
    seg_ref = next(it)
    qo_ref = next(it) if has_q else None
    ko_ref = next(it)
    vo_ref = next(it)
    seg = seg_ref[...]
    k = _head_norm(k_ref[...], kn_ref[...], seg)
    if use_rope:
        k = _rope(k, cos_ref[...], sin_ref[...])
    ko_ref[...] = k.astype(BF16)
    vo_ref[...] = v_ref[...].astype(BF16)
    if has_q:
        q = _head_norm(q_ref[...], qn_ref[...], seg)
        if use_rope:
            q = _rope(q, cos_ref[...], sin_ref[...])
        qo_ref[...] = (q * q_scale).astype(BF16)


def _qkv_prep(proj, col0, has_q, cos, sin_signed, qn, kn, seg, seq_len, tm):
    r_rows = proj.shape[0]
    w = seg.shape[0]
    use_rope = cos is not None
    tps = seq_len // tm
    full = lambda shape: pl.BlockSpec(shape, lambda i: (0,) * len(shape))
    col = lambda j: pl.BlockSpec((tm, w), lambda i: (i, j))
    tab = pl.BlockSpec((tm, LANES), lambda i: (i % tps, 0))
    in_specs, args = [], []
    ncol = col0
    if has_q:
        in_specs.append(col(ncol)); args.append(proj); ncol += 1
    in_specs += [col(ncol), col(ncol + 1)]
    args += [proj, proj]
    if use_rope:
        in_specs += [tab, tab]
        args += [cos, sin_signed]
    tile_gain = lambda g: jnp.tile(g, w // g.shape[0]).reshape(1, w)
    if has_q:
        in_specs.append(full((1, w))); args.append(tile_gain(qn))
    in_specs += [full((1, w)), full((w, w))]
    args += [tile_gain(kn), seg]
    n_out = 3 if has_q else 2
    kern = functools.partial(_qkv_prep_kernel, has_q=has_q, use_rope=use_rope, q_scale=DA_HEAD ** -0.5)
    return pl.pallas_call(
        kern,
        grid=(r_rows // tm,),
        in_specs=in_specs,
        out_specs=[pl.BlockSpec((tm, w), lambda i: (i, 0))] * n_out,
        out_shape=[jax.ShapeDtypeStruct((r_rows, w), BF16)] * n_out,
        compiler_params=_cparams(("arbitrary",)),
        name="qkv_prep_q" if has_q else "qkv_prep_ctx",
    )(*args)


def _diff_attn_kernel(q_ref, k_ref, vt_ref, lam_ref, sub_ref, o_ref, m_scr, l_scr, acc_scr, *, lam_init, tk):
    nk = k_ref.shape[0] // tk
    q = q_ref[...]
    lane = lax.broadcasted_iota(jnp.int32, q.shape, 1)
    zero = jnp.zeros_like(q)
    qm = (jnp.where(lane < DA_HEAD, q, zero), jnp.where(lane < DA_HEAD, zero, q))
    m_scr[...] = jnp.full(m_scr.shape, -jnp.inf, F32)
    l_scr[...] = jnp.zeros(l_scr.shape, F32)
    acc_scr[...] = jnp.zeros(acc_scr.shape, F32)

    def body(j, carry):
        off = pl.multiple_of(j * tk, tk)
        k = k_ref[pl.ds(off, tk), :]
        vt = vt_ref[:, pl.ds(off, tk)]
        for c in range(2):
            s = _dot_nt(k, qm[c])
            m_prev = m_scr[c]
            m_new = jnp.maximum(m_prev, jnp.max(s, axis=0, keepdims=True))
            alpha = jnp.exp(m_prev - m_new)
            p = jnp.exp(s - m_new)
            l_scr[c] = alpha * l_scr[c] + jnp.sum(p, axis=0, keepdims=True)
            acc_scr[c] = alpha * acc_scr[c] + _dot(vt, p.astype(BF16))
            m_scr[c] = m_new
        return carry

    lax.fori_loop(0, nk, body, 0)

    lv = lam_ref[...]
    lam = (jnp.exp(jnp.sum(lv[0:1] * lv[1:2], axis=-1, keepdims=True))
           - jnp.exp(jnp.sum(lv[2:3] * lv[3:4], axis=-1, keepdims=True)) + lam_init)
    o = acc_scr[0] / l_scr[0] - lam * (acc_scr[1] / l_scr[1])
    y = o * lax.rsqrt(jnp.mean(o * o, axis=0, keepdims=True) + SUBLN_EPS)
    o_ref[...] = (y * (sub_ref[...] * (1.0 - lam_init))).T


def _diff_attn_bounded_kernel(q_ref, k_ref, vt_ref, lam_ref, sub_ref, o_ref, m_scr, l_scr, acc_scr,
                              *, lam_init, tk):
    del m_scr
    nk = k_ref.shape[0] // tk
    q = q_ref[...]
    lane = lax.broadcasted_iota(jnp.int32, q.shape, 1)
    zero = jnp.zeros_like(q)
    qm = (jnp.where(lane < DA_HEAD, q, zero), jnp.where(lane < DA_HEAD, zero, q))
    l_scr[...] = jnp.zeros(l_scr.shape, F32)
    acc_scr[...] = jnp.zeros(acc_scr.shape, F32)

    def body(j, carry):
        off = pl.multiple_of(j * tk, tk)
        k = k_ref[pl.ds(off, tk), :]
        vt = vt_ref[:, pl.ds(off, tk)]
        for c in range(2):
            p = jnp.exp(_dot_nt(k, qm[c]))
            l_scr[c] += jnp.sum(p, axis=0, keepdims=True)
            acc_scr[c] += _dot(vt, p.astype(BF16))
        return carry

    lax.fori_loop(0, nk, body, 0)

    lv = lam_ref[...]
    lam = (jnp.exp(jnp.sum(lv[0:1] * lv[1:2], axis=-1, keepdims=True))
           - jnp.exp(jnp.sum(lv[2:3] * lv[3:4], axis=-1, keepdims=True)) + lam_init)
    o = acc_scr[0] / l_scr[0] - lam * (acc_scr[1] / l_scr[1])
    y = o * lax.rsqrt(jnp.mean(o * o, axis=0, keepdims=True) + SUBLN_EPS)
    o_ref[...] = (y * (sub_ref[...] * (1.0 - lam_init))).T


def _diff_attn(q, k_all, vt_all, lam_vecs, subln, lam_init, tq, tk, bounded):
    b, t, w = q.shape
    lk = k_all.shape[1]
    nh = w // DA_VHEAD
    assert t % tq == 0 and lk % tk == 0
    body = _diff_attn_bounded_kernel if bounded else _diff_attn_kernel
    kern = functools.partial(body, lam_init=lam_init, tk=tk)
    return pl.pallas_call(
        kern,
        grid=(b, nh, t // tq),
        in_specs=[pl.BlockSpec((None, tq, DA_VHEAD), lambda bi, h, i: (bi, i, h)),
                  pl.BlockSpec((None, lk, DA_VHEAD), lambda bi, h, i: (bi, 0, h)),
                  pl.BlockSpec((None, DA_VHEAD, lk), lambda bi, h, i: (bi, h, 0)),
                  pl.BlockSpec((4, DA_HEAD), lambda bi, h, i: (0, 0)),
                  pl.BlockSpec((DA_VHEAD, 1), lambda bi, h, i: (0, 0))],
        out_specs=pl.BlockSpec((None, tq, DA_VHEAD), lambda bi, h, i: (bi, i, h)),
        out_shape=jax.ShapeDtypeStruct((b, t, w), F32),
        scratch_shapes=[pltpu.VMEM((2, 1, tq), F32), pltpu.VMEM((2, 1, tq), F32),
                        pltpu.VMEM((2, DA_VHEAD, tq), F32)],
        compiler_params=_cparams(("arbitrary", "arbitrary", "arbitrary")),
        name="diff_attn_bounded" if bounded else "diff_attn",
    )(q, k_all, vt_all, lam_vecs, subln.reshape(DA_VHEAD, 1))


def _gated_out_kernel(y_ref, g_ref, x_ref, gate_ref, w_ref, out_ref):
    y = y_ref[...] * _silu(g_ref[...])
    out_ref[...] = x_ref[...] + gate_ref[...] * _dot(y.astype(BF16), w_ref[...])


def _gated_out(y, proj, gcol, x2d, gate, w_out, rows_per_group, tm):
    r_rows, w = y.shape
    d = x2d.shape[1]
    tpg = rows_per_group // tm
    row = lambda n: pl.BlockSpec((tm, n), lambda i: (i, 0))
    return pl.pallas_call(
        _gated_out_kernel,
        grid=(r_rows // tm,),
        in_specs=[row(w), pl.BlockSpec((tm, w), lambda i: (i, gcol)), row(d),
                  pl.BlockSpec((None, 1, d), lambda i: (i // tpg, 0, 0)),
                  pl.BlockSpec((w, d), lambda i: (0, 0))],
        out_specs=row(d),
        out_shape=jax.ShapeDtypeStruct((r_rows, d), F32),
        compiler_params=_cparams(("arbitrary",)),
        name="gated_out",
    )(y, proj, x2d, gate, w_out)


def _segment_matrix(n, width):
    i = lax.broadcasted_iota(jnp.int32, (n, n), 0) // width
    j = lax.broadcasted_iota(jnp.int32, (n, n), 1) // width
    return (i == j).astype(BF16)


def _rope_tables(rows):
    row_ids = jnp.repeat(jnp.arange(rows), GRID_W).astype(F32)
    col_ids = jnp.tile(jnp.arange(GRID_W), rows).astype(F32)
    inv_freq = ROPE_THETA ** (-jnp.arange(ROPE_FREQS, dtype=F32) / ROPE_FREQS)
    ang_r = row_ids[:, None] * inv_freq
    ang_c = col_ids[:, None] * inv_freq
    cr, sr, cc, sc = jnp.cos(ang_r), jnp.sin(ang_r), jnp.cos(ang_c), jnp.sin(ang_c)
    cos = jnp.concatenate([cr, cr, cc, cc], axis=-1)
    sin_signed = jnp.concatenate([-sr, sr, -sc, sc], axis=-1)
    return jnp.tile(cos, (1, 2)), jnp.tile(sin_signed, (1, 2))


def kernel(x, c, ctx, c_ctx, ada_w, ada_b, norm_g, rw_in, rw_mu, rw_w0, rw_w2, rw_a0, rw_a2, rw_kk, rw_ka, rw_rk, rw_gn_g, rw_gn_b, rw_out, da_in, da_qn, da_kn, da_lam, da_subln, da_out):
    bsz, t, d = x.shape
    lc = ctx.shape[1]
    depth = ada_w.shape[0]
    assert depth == 2 and bsz + 1 <= SUBLANES
    cw = rw_kk.shape[-1]
    x2 = x.reshape(bsz * t, d)
    xc2 = ctx.reshape(bsz * lc, d)

    rows = jnp.zeros((SUBLANES, d), F32).at[:bsz].set(c).at[bsz].set(c_ctx)
    mod = _ada_mod(rows, ada_w, ada_b)
    shift, scale, gate = mod[:, :, :d], mod[:, :, d:2 * d], mod[:, :, 2 * d:]
    mult = norm_g[:, None, :] * (1.0 + scale)
    lat = lambda a, i: a[i, :bsz].reshape(bsz, 1, d)
    con = lambda a, i: a[i, bsz:bsz + 1].reshape(1, 1, d)

    tm_l = 256
    tm_c = min(256, lc)

    w_in0 = rw_in[0].astype(BF16)
    proj_l = _normmod_mm(x2, lat(mult, 0), lat(shift, 0), w_in0, t, tm_l)
    proj_c = _normmod_mm(xc2, con(mult, 0), con(shift, 0), w_in0, bsz * lc, tm_c)
    seg64 = _segment_matrix(cw, RW_HEAD)
    prep = functools.partial(_rwkv_prep, mu=rw_mu[0], w0=rw_w0[0], w2=rw_w2[0], a0=rw_a0[0], a2=rw_a2[0],
                             k_k=rw_kk[0], k_a=rw_ka[0], seg=seg64)
    r_l, v_l, kk_l, lw_l, kd_l, bb_l = prep(proj_l, t, tm=tm_l)
    r_c, v_c, kk_c, lw_c, kd_c, bb_c = prep(proj_c, lc, tm=tm_c)

    s0 = jnp.zeros((bsz, 2, cw // LANES, LANES, LANES), F32)
    b3 = lambda a, n: a.reshape(bsz, n, cw)
    b4 = lambda a, n: a.reshape(2, bsz, n, cw)
    ocf, ocb, s_c = _wkv_scan(b3(r_c, lc), b3(v_c, lc), b3(kk_c, lc), b4(lw_c, lc), b4(kd_c, lc),
                              b4(bb_c, lc), s0)
    olf, olb, _ = _wkv_scan(b3(r_l, t), b3(v_l, t), b3(kk_l, t), b4(lw_l, t), b4(kd_l, t), b4(bb_l, t), s_c)

    w_out0 = rw_out[0].astype(BF16)
    readout = functools.partial(_rwkv_readout, r_k=rw_rk[0].reshape(-1), gn_g=rw_gn_g[0], gn_b=rw_gn_b[0],
                                seg=seg64, w_out=w_out0)
    x1 = readout(olf.reshape(bsz * t, cw), olb.reshape(bsz * t, cw), r_l, v_l, kd_l, proj_l, x2,
                 lat(gate, 0), rows_per_group=t, tm=tm_l)
    xc1 = readout(ocf.reshape(bsz * lc, cw), ocb.reshape(bsz * lc, cw), r_c, v_c, kd_c, proj_c, xc2,
                  con(gate, 0), rows_per_group=bsz * lc, tm=tm_c)

    qw = DA_HEADS * 2 * DA_HEAD
    w_in1 = da_in[0].astype(BF16)
    proj2_l = _normmod_mm(x1, lat(mult, 1), lat(shift, 1), w_in1, t, tm_l)
    proj2_c = _normmod_mm(xc1, con(mult, 1), con(shift, 1), w_in1[:, qw:3 * qw], bsz * lc, tm_c)
    seg_da = _segment_matrix(qw, DA_HEAD)
    cos, sin_signed = _rope_tables(t // GRID_W)
    q_b, k_b, v_b = _qkv_prep(proj2_l, 0, True, cos, sin_signed, da_qn[0], da_kn[0], seg_da, t, tm_l)
    kc_b, vc_b = _qkv_prep(proj2_c, 0, False, None, None, None, da_kn[0], seg_da, lc, tm_c)
    k_all = jnp.concatenate([kc_b.reshape(bsz, lc, qw), k_b.reshape(bsz, t, qw)], axis=1)
    v_all = jnp.concatenate([vc_b.reshape(bsz, lc, qw), v_b.reshape(bsz, t, qw)], axis=1)
    lam_init = 0.8 - 0.6 * math.exp(-0.3 * 1)
    lk = lc + t
    tk = max(n for n in range(LANES, 1024 + 1, LANES) if lk % n == 0)
    vt_all = jnp.swapaxes(v_all, 1, 2)
    attn = functools.partial(_diff_attn, lam_vecs=da_lam[0], subln=da_subln[0], lam_init=lam_init,
                             tq=min(1024, t), tk=tk)
    score_bound = DA_HEAD ** 0.5 * jnp.max(jnp.abs(da_qn[0])) * jnp.max(jnp.abs(da_kn[0]))
    y_att = lax.cond(score_bound < MAX_UNSHIFTED_SCORE,
                     functools.partial(attn, bounded=True), functools.partial(attn, bounded=False),
                     q_b.reshape(bsz, t, qw), k_all, vt_all)
    out = _gated_out(y_att.reshape(bsz * t, qw), proj2_l, 3, x1, lat(gate, 1), da_out[0].astype(BF16),
                     rows_per_group=t, tm=tm_l)
    return out.reshape(bsz, t, d)
```

```python
import functools
import math

import jax
import jax.numpy as jnp
from jax import lax
from jax.experimental import pallas as pl
from jax.experimental.pallas import tpu as pltpu

F32 = jnp.float32
BF16 = jnp.bfloat16

GRID_W = 64
RMS_EPS = 1e-6
RW_HEAD = 64
W_LORA = 64
A_LORA = 64
GN_EPS = 64e-5
DA_HEADS = 8
DA_HEAD = 64
DA_VHEAD = 2 * DA_HEAD
SUBLN_EPS = 1e-5
ROPE_THETA = 10000.0
ROPE_FREQS = DA_HEAD // 4

LANES = 128
SUBLANES = 8
SCAN_CHUNK = 64
VMEM_LIMIT = 56 * 1024 * 1024
MAX_UNSHIFTED_SCORE = 40.0


def _cparams(sem):
    return pltpu.CompilerParams(dimension_semantics=sem, vmem_limit_bytes=VMEM_LIMIT)


def _dot(a, b):
    return jnp.dot(a, b, preferred_element_type=F32)


def _dot_nt(a, b):
    return lax.dot_general(a, b, (((1,), (1,)), ((), ())), preferred_element_type=F32)


def _dot_tn(a, b):
    return lax.dot_general(a, b, (((0,), (0,)), ((), ())), preferred_element_type=F32)


def _split2(x):
    hi = x.astype(BF16)
    lo = (x - hi.astype(F32)).astype(BF16)
    return hi, lo


def _seg_sum(x, seg2):
    out = []
    for j in range(x.shape[-1] // LANES):
        hi, lo = _split2(x[:, j * LANES:(j + 1) * LANES])
        out.append(_dot(jnp.concatenate([hi, lo], axis=1), seg2))
    return jnp.concatenate(out, axis=1)


def _sigmoid(x):
    return 1.0 / (1.0 + jnp.exp(-x))


def _silu(x):
    return x * _sigmoid(x)


def _ada_kernel(c_ref, w_ref, b_ref, o_ref):
    s = _silu(c_ref[...])
    o_ref[...] = jnp.dot(s, w_ref[...], preferred_element_type=F32,
                         precision=lax.Precision.HIGHEST) + b_ref[...]


def _ada_mod(rows, ada_w, ada_b):
    depth, d, n = ada_w.shape
    tn = 768
    return pl.pallas_call(
        _ada_kernel,
        grid=(depth, n // tn),
        in_specs=[pl.BlockSpec((SUBLANES, d), lambda i, j: (0, 0)),
                  pl.BlockSpec((None, d, tn), lambda i, j: (i, 0, j)),
                  pl.BlockSpec((None, 1, tn), lambda i, j: (i, 0, j))],
        out_specs=pl.BlockSpec((None, SUBLANES, tn), lambda i, j: (i, 0, j)),
        out_shape=jax.ShapeDtypeStruct((depth, SUBLANES, n), F32),
        compiler_params=_cparams(("arbitrary", "arbitrary")),
        name="ada_mod",
    )(rows, ada_w, ada_b.reshape(depth, 1, n))


def _normmod_mm_kernel(x_ref, mult_ref, shift_ref, w_ref, o_ref):
    x = x_ref[...]
    y = x * lax.rsqrt(jnp.mean(x * x, axis=-1, keepdims=True) + RMS_EPS)
    h = y * mult_ref[...] + shift_ref[...]
    o_ref[...] = _dot(h.astype(BF16), w_ref[...])


def _normmod_mm(x2d, mult, shift, w, rows_per_group, tm):
    r, d = x2d.shape
    n = w.shape[1]
    tpg = rows_per_group // tm
    return pl.pallas_call(
        _normmod_mm_kernel,
        grid=(r // tm,),
        in_specs=[pl.BlockSpec((tm, d), lambda i: (i, 0)),
                  pl.BlockSpec((None, 1, d), lambda i: (i // tpg, 0, 0)),
                  pl.BlockSpec((None, 1, d), lambda i: (i // tpg, 0, 0)),
                  pl.BlockSpec((d, n), lambda i: (0, 0))],
        out_specs=pl.BlockSpec((tm, n), lambda i: (i, 0)),
        out_shape=jax.ShapeDtypeStruct((r, n), F32),
        compiler_params=_cparams(("arbitrary",)),
        name="normmod_mm",
    )(x2d, mult, shift, w)


def _token_shift(u, prev8, next8, mu, first, last):
    tm = u.shape[0]
    row = lax.broadcasted_iota(jnp.int32, u.shape, 0)
    prow = jnp.where(first, 0.0, prev8[SUBLANES - 1:SUBLANES, :])
    nrow = jnp.where(last, 0.0, next8[0:1, :])
    prev = jnp.where(row == 0, prow, pltpu.roll(u, 1, 0))
    nxt = jnp.where(row == tm - 1, nrow, pltpu.roll(u, tm - 1, 0))
    return u + mu * (0.5 * (prev + nxt) - u)


def _rwkv_prep_kernel(pa_ref, pap_ref, pan_ref, pb_ref, pbp_ref, pbn_ref,
                      mua_ref, mub_ref, w2_ref, w0_ref, a2_ref, a0_ref, kk_ref, ka_ref, seg_ref,
                      r_ref, v_ref, kkn_ref, lw_ref, kd_ref, bb_ref, *, tiles_per_seq, c):
    i = pl.program_id(0)
    first = (i % tiles_per_seq) == 0
    last = (i % tiles_per_seq) == tiles_per_seq - 1
    ma = _token_shift(pa_ref[...], pap_ref[...], pan_ref[...], mua_ref[...], first, last)
    mb = _token_shift(pb_ref[...], pbp_ref[...], pbn_ref[...], mub_ref[...], first, last)
    r = ma[:, :c]
    k = ma[:, c:2 * c]
    v = ma[:, 2 * c:]
    nl = 2 * W_LORA
    lw = jnp.tanh(mb[:, :nl])
    la = mb[:, nl:]
    w_raw = w0_ref[...] + _dot(lw.astype(BF16), w2_ref[...])
    a = _sigmoid(a0_ref[...] + _dot(la.astype(BF16), a2_ref[...]))
    logw = -math.exp(-0.5) * _sigmoid(w_raw)
    kk = k * kk_ref[...]
    kkn = kk * lax.rsqrt(_seg_sum(kk * kk, seg_ref[...]) + 1e-12)
    r_ref[...] = r
    v_ref[...] = v
    kkn_ref[...] = kkn
    for d in range(2):
        a_d = a[:, d * c:(d + 1) * c]
        lw_ref[d] = logw[:, d * c:(d + 1) * c]
        kd_ref[d] = k * (1.0 + (a_d - 1.0) * ka_ref[...])
        bb_ref[d] = kkn * a_d


def _rwkv_prep(proj, seq_len, mu, w0, w2, a0, a2, k_k, k_a, seg, tm):
    r_rows, ncols = proj.shape
    c = k_k.shape[-1]
    nb = ncols - 4 * c
    assert nb == 2 * (W_LORA + A_LORA) and (4 * c) % nb == 0 and seq_len % tm == 0
    tps = seq_len // tm
    t8 = tm // SUBLANES
    nblk8 = r_rows // SUBLANES
    bcol = (4 * c) // nb
    mua = mu[:3 * c].reshape(1, 3 * c)
    mub = mu[3 * c:].reshape(1, nb)
    zw = jnp.zeros((W_LORA, c), F32)
    w2cat = jnp.concatenate([jnp.concatenate([w2[0], zw], 1), jnp.concatenate([zw, w2[1]], 1)], 0).astype(BF16)
    a2cat = jnp.concatenate([jnp.concatenate([a2[0], zw], 1), jnp.concatenate([zw, a2[1]], 1)], 0).astype(BF16)
    w0cat = w0.reshape(1, 2 * c)
    a0cat = a0.reshape(1, 2 * c)
    full = lambda shape: pl.BlockSpec(shape, lambda i: (0,) * len(shape))
    prev_idx = lambda i: jnp.maximum(i * t8 - 1, 0)
    next_idx = lambda i: jnp.minimum((i + 1) * t8, nblk8 - 1)
    row_out = pl.BlockSpec((tm, c), lambda i: (i, 0))
    dir_out = pl.BlockSpec((2, tm, c), lambda i: (0, i, 0))
    kern = functools.partial(_rwkv_prep_kernel, tiles_per_seq=tps, c=c)
    return pl.pallas_call(
        kern,
        grid=(r_rows // tm,),
        in_specs=[pl.BlockSpec((tm, 3 * c), lambda i: (i, 0)),
                  pl.BlockSpec((SUBLANES, 3 * c), lambda i: (prev_idx(i), 0)),
                  pl.BlockSpec((SUBLANES, 3 * c), lambda i: (next_idx(i), 0)),
                  pl.BlockSpec((tm, nb), lambda i: (i, bcol)),
                  pl.BlockSpec((SUBLANES, nb), lambda i: (prev_idx(i), bcol)),
                  pl.BlockSpec((SUBLANES, nb), lambda i: (next_idx(i), bcol)),
                  full((1, 3 * c)), full((1, nb)),
                  full((2 * W_LORA, 2 * c)), full((1, 2 * c)),
                  full((2 * A_LORA, 2 * c)), full((1, 2 * c)),
                  full((1, c)), full((1, c)), full((2 * LANES, LANES))],
        out_specs=[row_out, row_out, row_out, dir_out, dir_out, dir_out],
        out_shape=[jax.ShapeDtypeStruct((r_rows, c), F32)] * 3
                  + [jax.ShapeDtypeStruct((2, r_rows, c), F32)] * 3,
        compiler_params=_cparams(("arbitrary",)),
        name="rwkv_prep",
    )(proj, proj, proj, proj, proj, proj, mua, mub, w2cat, w0cat, a2cat, a0cat,
      k_k.reshape(1, c), k_a.reshape(1, c), seg)


def _head_pair_stack(xp, lane_a):
    return jnp.concatenate([jnp.where(lane_a, xp, 0.0), jnp.where(lane_a, 0.0, xp)], axis=0)


def _wkv_chunk_operands(r, v, kk, lw, kd, bb, reverse, L):
    ti = lax.broadcasted_iota(jnp.int32, (L, L), 0)
    si = lax.broadcasted_iota(jnp.int32, (L, L), 1)
    tri = ((ti <= si) if reverse else (ti >= si)).astype(BF16)
    hi = lw.astype(BF16)
    r1 = lw - hi.astype(F32)
    mid = r1.astype(BF16)
    lo = (r1 - mid.astype(F32)).astype(BF16)
    cum = _dot(tri, hi) + _dot(tri, mid) + _dot(tri, lo)
    tot = jnp.sum(lw, axis=0, keepdims=True)
    ginv = jnp.exp(-cum)
    gend = jnp.exp(tot - cum)
    return dict(kt=kk * jnp.exp(cum - lw), rt=r * jnp.exp(cum), bt=bb * ginv,
                kdt=kd * ginv, khat=kd * gend, bhat=bb * gend, v=v, tot=tot)


def _wkv_scan_kernel(rf_ref, vf_ref, kkf_ref, lwf_ref, kdf_ref, bbf_ref,
                     rb_ref, vb_ref, kkb_ref, lwb_ref, kdb_ref, bbb_ref, s0_ref,
                     of_ref, ob_ref, sfin_ref, h_scr, *, chunk, n_pairs):
    L = chunk
    ci = pl.program_id(0)
    nc = pl.num_programs(0)
    nb = rf_ref.shape[0]

    @pl.when(ci == 0)
    def _():
        h_scr[...] = s0_ref[...]

    in_refs = ((rf_ref, vf_ref, kkf_ref, lwf_ref, kdf_ref, bbf_ref),
               (rb_ref, vb_ref, kkb_ref, lwb_ref, kdb_ref, bbb_ref))
    ops = {(b, d): _wkv_chunk_operands(*[ref[b] for ref in in_refs[d]], d == 1, L)
           for b in range(nb) for d in range(2)}
    o_refs = (of_ref, ob_ref)

    p2 = 2 * L
    lane_a = lax.broadcasted_iota(jnp.int32, (L, LANES), 1) < RW_HEAD
    dts = ((lax.broadcasted_iota(jnp.int32, (p2, p2), 0) & (L - 1))
           - (lax.broadcasted_iota(jnp.int32, (p2, p2), 1) & (L - 1)))
    strict = (dts > 0, dts < 0)
    incl = (dts >= 0, dts <= 0)
    eye = (lax.broadcasted_iota(jnp.int32, (p2, p2), 0)
           == lax.broadcasted_iota(jnp.int32, (p2, p2), 1)).astype(F32)

    chains = [(b, d, p) for p in range(n_pairs) for b in range(nb) for d in range(2)]
    lanes_of = lambda p: slice(p * LANES, (p + 1) * LANES)
    stk = lambda name, b, d, p: _head_pair_stack(ops[b, d][name][:, lanes_of(p)], lane_a)
    kt_s = [stk("kt", *ch).astype(BF16) for ch in chains]
    rt_s = [stk("rt", *ch).astype(BF16) for ch in chains]
    y_s = [jnp.concatenate([stk("bt", *ch).astype(BF16), stk("kdt", *ch).astype(BF16)], axis=0)
           for ch in chains]
    v_s = [stk("v", *ch).astype(BF16) for ch in chains]
    h = [h_scr[ch] for ch in chains]
    h_b = [x.astype(BF16) for x in h]
    n = range(len(chains))
    sc = [_dot_nt(jnp.concatenate([kt_s[i], rt_s[i]], axis=0), y_s[i]) for i in n]
    rh = [_dot(rt_s[i], h_b[i]) for i in n]
    msk = lambda m, x: jnp.where(m, x, 0.0)
    pw = [-msk(strict[d], sc[i][:p2, :p2]) for i, (b, d, p) in enumerate(chains)]
    a_k = [msk(strict[d], sc[i][:p2, p2:]).astype(BF16) for i, (b, d, p) in enumerate(chains)]
    pkb = [jnp.concatenate([msk(incl[d], sc[i][p2:, p2:]), -msk(incl[d], sc[i][p2:, :p2])],
                           axis=1).astype(BF16) for i, (b, d, p) in enumerate(chains)]
    u = [_dot(jnp.concatenate([kt_s[i], a_k[i]], axis=1), jnp.concatenate([h_b[i], v_s[i]], axis=0))
         for i in n]

    pwb = [x.astype(BF16) for x in pw]
    tinv = [eye + pw[i] for i in n]
    pw = [_dot(pwb[i], pwb[i]) for i in n]
    for j in range(1, int(math.log2(L))):
        pwb = [x.astype(BF16) for x in pw]
        if j < int(math.log2(L)) - 1:
            w = [_dot(jnp.concatenate([tinv[i].astype(BF16), pwb[i]], axis=0), pwb[i]) for i in n]
            tinv = [tinv[i] + w[i][:p2] for i in n]
            pw = [w[i][p2:] for i in n]
        else:
            tinv = [tinv[i] + _dot(tinv[i].astype(BF16), pwb[i]) for i in n]
    u = [_dot(tinv[i].astype(BF16), u[i].astype(BF16)) for i in n]

    vu = [jnp.concatenate([v_s[i], u[i].astype(BF16)], axis=0) for i in n]
    for i, (b, d, p) in enumerate(chains):
        o_s = rh[i] + _dot(pkb[i], vu[i])
        o_refs[d][b, :, lanes_of(p)] = o_s[:L] + o_s[L:]
    for i, (b, d, p) in enumerate(chains):
        kb = jnp.concatenate([stk("khat", b, d, p), -stk("bhat", b, d, p)], axis=0).astype(BF16)
        tot_col = jnp.broadcast_to(ops[b, d]["tot"][:, lanes_of(p)], (LANES, LANES)).T
        h_scr[b, d, p] = jnp.exp(tot_col) * h[i] + _dot_tn(kb, vu[i])

    @pl.when(ci == nc - 1)
    def _():
        sfin_ref[...] = h_scr[...]


def _wkv_scan(r, v, kkn, logw, kd, bb, s0, chunk=SCAN_CHUNK):
    b, t, c = r.shape
    n_pairs = c // LANES
    assert t % chunk == 0 and 2 * RW_HEAD == LANES
    nc = t // chunk
    fwd = lambda ci: ci
    bwd = lambda ci: nc - 1 - ci
    shared = lambda cmap: pl.BlockSpec((b, chunk, c), lambda ci: (0, cmap(ci), 0))
    perdir = lambda d, cmap: pl.BlockSpec((None, b, chunk, c), lambda ci: (d, 0, cmap(ci), 0))
    state = pl.BlockSpec((b, 2, n_pairs, LANES, LANES), lambda ci: (0, 0, 0, 0, 0))
    kern = functools.partial(_wkv_scan_kernel, chunk=chunk, n_pairs=n_pairs)
    return pl.pallas_call(
        kern,
        grid=(nc,),
        in_specs=[shared(fwd)] * 3 + [perdir(0, fwd)] * 3 + [shared(bwd)] * 3 + [perdir(1, bwd)] * 3 + [state],
        out_specs=[shared(fwd), shared(bwd), state],
        out_shape=[jax.ShapeDtypeStruct((b, t, c), F32), jax.ShapeDtypeStruct((b, t, c), F32),
                   jax.ShapeDtypeStruct((b, 2, n_pairs, LANES, LANES), F32)],
        scratch_shapes=[pltpu.VMEM((b, 2, n_pairs, LANES, LANES), F32)],
        compiler_params=_cparams(("arbitrary",)),
        name="wkv_scan",
    )(r, v, kkn, logw, kd, bb, r, v, kkn, logw, kd, bb, s0)


def _rwkv_readout_kernel(of_ref, ob_ref, r_ref, v_ref, kd_ref, g_ref, x_ref, gate_ref,
                         rk_ref, gng_ref, gnb_ref, seg_ref, w_ref, out_ref):
    seg = seg_ref[...]
    inv_n = 1.0 / RW_HEAD
    o = of_ref[...] + ob_ref[...]
    mean = _seg_sum(o, seg) * inv_n
    dlt = o - mean
    var = _seg_sum(dlt * dlt, seg) * inv_n
    y = dlt * lax.rsqrt(var + GN_EPS) * gng_ref[...] + gnb_ref[...]
    bonus = _seg_sum(r_ref[...] * (kd_ref[0] + kd_ref[1]) * rk_ref[...], seg) * v_ref[...]
    y = (y + bonus) * _silu(g_ref[...])
    out_ref[...] = x_ref[...] + gate_ref[...] * _dot(y.astype(BF16), w_ref[...])


def _rwkv_readout(o_f, o_b, r, v, kd, proj, x2d, gate, r_k, gn_g, gn_b, seg, w_out, rows_per_group, tm):
    r_rows, c = r.shape
    d = x2d.shape[1]
    tpg = rows_per_group // tm
    gcol = (3 * c) // c
    full = lambda shape: pl.BlockSpec(shape, lambda i: (0,) * len(shape))
    row = lambda n: pl.BlockSpec((tm, n), lambda i: (i, 0))
    dirs = pl.BlockSpec((2, tm, c), lambda i: (0, i, 0))
    return pl.pallas_call(
        _rwkv_readout_kernel,
        grid=(r_rows // tm,),
        in_specs=[row(c), row(c), row(c), row(c), dirs,
                  pl.BlockSpec((tm, c), lambda i: (i, gcol)),
                  row(d),
                  pl.BlockSpec((None, 1, d), lambda i: (i // tpg, 0, 0)),
                  full((1, c)), full((1, c)), full((1, c)), full((2 * LANES, LANES)), full((c, d))],
        out_specs=row(d),
        out_shape=jax.ShapeDtypeStruct((r_rows, d), F32),
        compiler_params=_cparams(("arbitrary",)),
        name="rwkv_readout",
    )(o_f, o_b, r, v, kd, proj, x2d, gate, r_k.reshape(1, c), gn_g.reshape(1, c), gn_b.reshape(1, c), seg,
      w_out)


def _head_norm(x, gain, seg):
    ms = _seg_sum(x * x, seg) * (1.0 / DA_HEAD)
    return x * lax.rsqrt(ms + RMS_EPS) * gain


def _rope(x, cos, sin_signed):
    n = x.shape[-1]
    lane = lax.broadcasted_iota(jnp.int32, x.shape, 1)
    half0 = (lane & (2 * ROPE_FREQS - 1)) < ROPE_FREQS
    partner = jnp.where(half0, pltpu.roll(x, n - ROPE_FREQS, 1), pltpu.roll(x, ROPE_FREQS, 1))
    reps = n // cos.shape[-1]
    wide = lambda tab: jnp.concatenate([tab] * reps, axis=1)
    return x * wide(cos) + partner * wide(sin_signed)


def _qkv_prep_kernel(*refs, has_q, use_rope, q_scale):
    it = iter(refs)
    q_ref = next(it) if has_q else None
    k_ref = next(it)
    v_ref = next(it)
    cos_ref = next(it) if use_rope else None
    sin_ref = next(it) if use_rope else None
    qn_ref = next(it) if has_q else None
    kn_ref = next(it)
    seg_ref = next(it)
    qo_ref = next(it) if has_q else None
    ko_ref = next(it)
    vo_ref = next(it)
    seg = seg_ref[...]
    k = _head_norm(k_ref[...], kn_ref[...], seg)
    if use_rope:
        k = _rope(k, cos_ref[...], sin_ref[...])
    ko_ref[...] = k.astype(BF16)
    vo_ref[...] = v_ref[...].T.astype(BF16)
    if has_q:
        q = _head_norm(q_ref[...], qn_ref[...], seg)
        if use_rope:
            q = _rope(q, cos_ref[...], sin_ref[...])
        qo_ref[...] = (q * q_scale).astype(BF16)


def _qkv_prep(proj, col0, has_q, cos, sin_signed, qn, kn, seg, w, seq_len, tm):
    r_rows = proj.shape[0]
    use_rope = cos is not None
    tps = seq_len // tm
    full = lambda shape: pl.BlockSpec(shape, lambda i: (0,) * len(shape))
    col = lambda j: pl.BlockSpec((tm, w), lambda i: (i, j))
    tab = pl.BlockSpec((tm, LANES), lambda i: (i % tps, 0))
    in_specs, args = [], []
    ncol = col0
    if has_q:
        in_specs.append(col(ncol)); args.append(proj); ncol += 1
    in_specs += [col(ncol), col(ncol + 1)]
    args += [proj, proj]
    if use_rope:
        in_specs += [tab, tab]
        args += [cos, sin_signed]
    tile_gain = lambda g: jnp.tile(g, w // g.shape[0]).reshape(1, w)
    if has_q:
        in_specs.append(full((1, w))); args.append(tile_gain(qn))
    in_specs += [full((1, w)), full((2 * LANES, LANES))]
    args += [tile_gain(kn), seg]
    n_rowmajor = 2 if has_q else 1
    kern = functools.partial(_qkv_prep_kernel, has_q=has_q, use_rope=use_rope, q_scale=DA_HEAD ** -0.5)
    return pl.pallas_call(
        kern,
        grid=(r_rows // tm,),
        in_specs=in_specs,
        out_specs=[pl.BlockSpec((tm, w), lambda i: (i, 0))] * n_rowmajor
                  + [pl.BlockSpec((None, w, tm), lambda i: (i // tps, 0, i % tps))],
        out_shape=[jax.ShapeDtypeStruct((r_rows, w), BF16)] * n_rowmajor
                  + [jax.ShapeDtypeStruct((r_rows // seq_len, w, seq_len), BF16)],
        compiler_params=_cparams(("arbitrary",)),
        name="qkv_prep_q" if has_q else "qkv_prep_ctx",
    )(*args)


def _diff_attn_kernel(q_ref, kc_ref, kl_ref, vtc_ref, vtl_ref, lam_ref, sub_ref, o_ref, m_scr, l_scr, acc_scr,
                      *, lam_init, tk, bounded):
    q = q_ref[...]
    lane = lax.broadcasted_iota(jnp.int32, q.shape, 1)
    zero = jnp.zeros_like(q)
    qm = (jnp.where(lane < DA_HEAD, q, zero), jnp.where(lane < DA_HEAD, zero, q))
    m_scr[...] = jnp.full(m_scr.shape, -jnp.inf, F32)
    l_scr[...] = jnp.zeros(l_scr.shape, F32)
    acc_scr[...] = jnp.zeros(acc_scr.shape, F32)

    def accumulate(k, vt):
        for c in range(2):
            s = _dot_nt(k, qm[c])
            if bounded:
                p = jnp.exp(s)
                l_scr[c] += jnp.sum(p, axis=0, keepdims=True)
                acc_scr[c] += _dot(vt, p.astype(BF16))
            else:
                m_prev = m_scr[c]
                m_new = jnp.maximum(m_prev, jnp.max(s, axis=0, keepdims=True))
                alpha = jnp.exp(m_prev - m_new)
                p = jnp.exp(s - m_new)
                l_scr[c] = alpha * l_scr[c] + jnp.sum(p, axis=0, keepdims=True)
                acc_scr[c] = alpha * acc_scr[c] + _dot(vt, p.astype(BF16))
                m_scr[c] = m_new

    def body(j, carry):
        off = pl.multiple_of(j * tk, tk)
        accumulate(kl_ref[pl.ds(off, tk), :], vtl_ref[:, pl.ds(off, tk)])
        return carry

    accumulate(kc_ref[...], vtc_ref[...])
    lax.fori_loop(0, kl_ref.shape[0] // tk, body, 0, unroll=bounded)

    lv = lam_ref[...]
    lam = (jnp.exp(jnp.sum(lv[0:1] * lv[1:2], axis=-1, keepdims=True))
           - jnp.exp(jnp.sum(lv[2:3] * lv[3:4], axis=-1, keepdims=True)) + lam_init)
    o = acc_scr[0] / l_scr[0] - lam * (acc_scr[1] / l_scr[1])
    y = o * lax.rsqrt(jnp.mean(o * o, axis=0, keepdims=True) + SUBLN_EPS)
    o_ref[...] = (y * (sub_ref[...] * (1.0 - lam_init))).T


def _diff_attn(q, k_ctx, k_lat, vt_ctx, vt_lat, lam_vecs, subln, lam_init, tq, tk, bounded):
    b, t, w = q.shape
    lc = k_ctx.shape[1]
    nh = w // DA_VHEAD
    assert t % tq == 0 and t % tk == 0
    kern = functools.partial(_diff_attn_kernel, lam_init=lam_init, tk=tk, bounded=bounded)
    return pl.pallas_call(
        kern,
        grid=(b, nh, t // tq),
        in_specs=[pl.BlockSpec((None, tq, DA_VHEAD), lambda bi, h, i: (bi, i, h)),
                  pl.BlockSpec((None, lc, DA_VHEAD), lambda bi, h, i: (bi, 0, h)),
                  pl.BlockSpec((None, t, DA_VHEAD), lambda bi, h, i: (bi, 0, h)),
                  pl.BlockSpec((None, DA_VHEAD, lc), lambda bi, h, i: (bi, h, 0)),
                  pl.BlockSpec((None, DA_VHEAD, t), lambda bi, h, i: (bi, h, 0)),
                  pl.BlockSpec((4, DA_HEAD), lambda bi, h, i: (0, 0)),
                  pl.BlockSpec((DA_VHEAD, 1), lambda bi, h, i: (0, 0))],
        out_specs=pl.BlockSpec((None, tq, DA_VHEAD), lambda bi, h, i: (bi, i, h)),
        out_shape=jax.ShapeDtypeStruct((b, t, w), F32),
        scratch_shapes=[pltpu.VMEM((2, 1, tq), F32), pltpu.VMEM((2, 1, tq), F32),
                        pltpu.VMEM((2, DA_VHEAD, tq), F32)],
        compiler_params=_cparams(("arbitrary", "arbitrary", "arbitrary")),
        name="diff_attn_bounded" if bounded else "diff_attn",
    )(q, k_ctx, k_lat, vt_ctx, vt_lat, lam_vecs, subln.reshape(DA_VHEAD, 1))


def _gated_out_kernel(y_ref, g_ref, x_ref, gate_ref, w_ref, out_ref):
    y = y_ref[...] * _silu(g_ref[...])
    out_ref[...] = x_ref[...] + gate_ref[...] * _dot(y.astype(BF16), w_ref[...])


def _gated_out(y, proj, gcol, x2d, gate, w_out, rows_per_group, tm):
    r_rows, w = y.shape
    d = x2d.shape[1]
    tpg = rows_per_group // tm
    row = lambda n: pl.BlockSpec((tm, n), lambda i: (i, 0))
    return pl.pallas_call(
        _gated_out_kernel,
        grid=(r_rows // tm,),
        in_specs=[row(w), pl.BlockSpec((tm, w), lambda i: (i, gcol)), row(d),
                  pl.BlockSpec((None, 1, d), lambda i: (i // tpg, 0, 0)),
                  pl.BlockSpec((w, d), lambda i: (0, 0))],
        out_specs=row(d),
        out_shape=jax.ShapeDtypeStruct((r_rows, d), F32),
        compiler_params=_cparams(("arbitrary",)),
        name="gated_out",
    )(y, proj, x2d, gate, w_out)


def _segment_matrix(width):
    i = (lax.broadcasted_iota(jnp.int32, (2 * LANES, LANES), 0) % LANES) // width
    j = lax.broadcasted_iota(jnp.int32, (2 * LANES, LANES), 1) // width
    return (i == j).astype(BF16)


def _rope_tables(rows):
    row_ids = jnp.repeat(jnp.arange(rows), GRID_W).astype(F32)
    col_ids = jnp.tile(jnp.arange(GRID_W), rows).astype(F32)
    inv_freq = ROPE_THETA ** (-jnp.arange(ROPE_FREQS, dtype=F32) / ROPE_FREQS)
    ang_r = row_ids[:, None] * inv_freq
    ang_c = col_ids[:, None] * inv_freq
    cr, sr, cc, sc = jnp.cos(ang_r), jnp.sin(ang_r), jnp.cos(ang_c), jnp.sin(ang_c)
    cos = jnp.concatenate([cr, cr, cc, cc], axis=-1)
    sin_signed = jnp.concatenate([-sr, sr, -sc, sc], axis=-1)
    return jnp.tile(cos, (1, 2)), jnp.tile(sin_signed, (1, 2))


def kernel(x, c, ctx, c_ctx, ada_w, ada_b, norm_g, rw_in, rw_mu, rw_w0, rw_w2, rw_a0, rw_a2, rw_kk, rw_ka, rw_rk, rw_gn_g, rw_gn_b, rw_out, da_in, da_qn, da_kn, da_lam, da_subln, da_out):
    bsz, t, d = x.shape
    lc = ctx.shape[1]
    depth = ada_w.shape[0]
    assert depth == 2 and bsz + 1 <= SUBLANES
    cw = rw_kk.shape[-1]
    x2 = x.reshape(bsz * t, d)
    xc2 = ctx.reshape(bsz * lc, d)

    rows = jnp.zeros((SUBLANES, d), F32).at[:bsz].set(c).at[bsz].set(c_ctx)
    mod = _ada_mod(rows, ada_w, ada_b)
    shift, scale, gate = mod[:, :, :d], mod[:, :, d:2 * d], mod[:, :, 2 * d:]
    mult = norm_g[:, None, :] * (1.0 + scale)
    lat = lambda a, i: a[i, :bsz].reshape(bsz, 1, d)
    con = lambda a, i: a[i, bsz:bsz + 1].reshape(1, 1, d)

    tm_l = 256
    tm_c = min(256, lc)

    w_in0 = rw_in[0].astype(BF16)
    proj_l = _normmod_mm(x2, lat(mult, 0), lat(shift, 0), w_in0, t, tm_l)
    proj_c = _normmod_mm(xc2, con(mult, 0), con(shift, 0), w_in0, bsz * lc, tm_c)
    seg64 = _segment_matrix(RW_HEAD)
    prep = functools.partial(_rwkv_prep, mu=rw_mu[0], w0=rw_w0[0], w2=rw_w2[0], a0=rw_a0[0], a2=rw_a2[0],
                             k_k=rw_kk[0], k_a=rw_ka[0], seg=seg64)
    r_l, v_l, kk_l, lw_l, kd_l, bb_l = prep(proj_l, t, tm=tm_l)
    r_c, v_c, kk_c, lw_c, kd_c, bb_c = prep(proj_c, lc, tm=tm_c)

    s0 = jnp.zeros((bsz, 2, cw // LANES, LANES, LANES), F32)
    b3 = lambda a, n: a.reshape(bsz, n, cw)
    b4 = lambda a, n: a.reshape(2, bsz, n, cw)
    ocf, ocb, s_c = _wkv_scan(b3(r_c, lc), b3(v_c, lc), b3(kk_c, lc), b4(lw_c, lc), b4(kd_c, lc),
                              b4(bb_c, lc), s0)
    olf, olb, _ = _wkv_scan(b3(r_l, t), b3(v_l, t), b3(kk_l, t), b4(lw_l, t), b4(kd_l, t), b4(bb_l, t), s_c)

    w_out0 = rw_out[0].astype(BF16)
    readout = functools.partial(_rwkv_readout, r_k=rw_rk[0].reshape(-1), gn_g=rw_gn_g[0], gn_b=rw_gn_b[0],
                                seg=seg64, w_out=w_out0)
    x1 = readout(olf.reshape(bsz * t, cw), olb.reshape(bsz * t, cw), r_l, v_l, kd_l, proj_l, x2,
                 lat(gate, 0), rows_per_group=t, tm=tm_l)
    xc1 = readout(ocf.reshape(bsz * lc, cw), ocb.reshape(bsz * lc, cw), r_c, v_c, kd_c, proj_c, xc2,
                  con(gate, 0), rows_per_group=bsz * lc, tm=tm_c)

    qw = DA_HEADS * 2 * DA_HEAD
    w_in1 = da_in[0].astype(BF16)
    proj2_l = _normmod_mm(x1, lat(mult, 1), lat(shift, 1), w_in1, t, tm_l)
    proj2_c = _normmod_mm(xc1, con(mult, 1), con(shift, 1), w_in1[:, qw:3 * qw], bsz * lc, tm_c)
    seg_da = _segment_matrix(DA_HEAD)
    cos, sin_signed = _rope_tables(t // GRID_W)
    q_b, k_b, vt_b = _qkv_prep(proj2_l, 0, True, cos, sin_signed, da_qn[0], da_kn[0], seg_da, qw, t, tm_l)
    kc_b, vtc_b = _qkv_prep(proj2_c, 0, False, None, None, None, da_kn[0], seg_da, qw, lc, tm_c)
    lam_init = 0.8 - 0.6 * math.exp(-0.3 * 1)
    attn = functools.partial(_diff_attn, lam_vecs=da_lam[0], subln=da_subln[0], lam_init=lam_init,
                             tq=min(1024, t), tk=min(1024, t))
    score_bound = DA_HEAD ** 0.5 * jnp.max(jnp.abs(da_qn[0])) * jnp.max(jnp.abs(da_kn[0]))
    y_att = lax.cond(score_bound < MAX_UNSHIFTED_SCORE,
                     functools.partial(attn, bounded=True), functools.partial(attn, bounded=False),
                     q_b.reshape(bsz, t, qw), kc_b.reshape(bsz, lc, qw), k_b.reshape(bsz, t, qw), vtc_b, vt_b)
    out = _gated_out(y_att.reshape(bsz * t, qw), proj2_l, 3, x1, lat(gate, 1), da_out[0].astype(BF16),
                     rows_per_group=t, tm=tm_l)
    return out.reshape(bsz, t, d)
```

```python
import functools
import math

import jax
import jax.numpy as jnp
from jax import lax
from jax.experimental import pallas as pl
from jax.experimental.pallas import tpu as pltpu

F32 = jnp.float32
BF16 = jnp.bfloat16

GRID_W = 64
RMS_EPS = 1e-6
RW_HEAD = 64
W_LORA = 64
A_LORA = 64
GN_EPS = 64e-5
DA_HEADS = 8
DA_HEAD = 64
DA_VHEAD = 2 * DA_HEAD
SUBLN_EPS = 1e-5
ROPE_THETA = 10000.0
ROPE_FREQS = DA_HEAD // 4

LANES = 128
SUBLANES = 8
SCAN_CHUNK = 64
VMEM_LIMIT = 56 * 1024 * 1024
MAX_UNSHIFTED_SCORE = 40.0


def _cparams(sem):
    return pltpu.CompilerParams(dimension_semantics=sem, vmem_limit_bytes=VMEM_LIMIT)


def _dot(a, b):
    return jnp.dot(a, b, preferred_element_type=F32)


def _dot_nt(a, b):
    return lax.dot_general(a, b, (((1,), (1,)), ((), ())), preferred_element_type=F32)


def _dot_tn(a, b):
    return lax.dot_general(a, b, (((0,), (0,)), ((), ())), preferred_element_type=F32)


def _split2(x):
    hi = x.astype(BF16)
    lo = (x - hi.astype(F32)).astype(BF16)
    return hi, lo


def _seg_sum(x, seg2):
    out = []
    for j in range(x.shape[-1] // LANES):
        hi, lo = _split2(x[:, j * LANES:(j + 1) * LANES])
        out.append(_dot(jnp.concatenate([hi, lo], axis=1), seg2))
    return jnp.concatenate(out, axis=1)


def _sigmoid(x):
    return 1.0 / (1.0 + jnp.exp(-x))


def _silu(x):
    return x * _sigmoid(x)


def _ada_kernel(c_ref, w_ref, b_ref, o_ref):
    s = _silu(c_ref[...])
    o_ref[...] = jnp.dot(s, w_ref[...], preferred_element_type=F32,
                         precision=lax.Precision.HIGHEST) + b_ref[...]


def _ada_mod(rows, ada_w, ada_b):
    depth, d, n = ada_w.shape
    tn = 768
    return pl.pallas_call(
        _ada_kernel,
        grid=(depth, n // tn),
        in_specs=[pl.BlockSpec((SUBLANES, d), lambda i, j: (0, 0)),
                  pl.BlockSpec((None, d, tn), lambda i, j: (i, 0, j)),
                  pl.BlockSpec((None, 1, tn), lambda i, j: (i, 0, j))],
        out_specs=pl.BlockSpec((None, SUBLANES, tn), lambda i, j: (i, 0, j)),
        out_shape=jax.ShapeDtypeStruct((depth, SUBLANES, n), F32),
        compiler_params=_cparams(("arbitrary", "arbitrary")),
        name="ada_mod",
    )(rows, ada_w, ada_b.reshape(depth, 1, n))


def _normmod(x, mult, shift):
    y = x * lax.rsqrt(jnp.mean(x * x, axis=-1, keepdims=True) + RMS_EPS)
    return (y * mult + shift).astype(BF16)


def _token_shift(u, prev8, next8, mu, first, last):
    tm = u.shape[0]
    row = lax.broadcasted_iota(jnp.int32, u.shape, 0)
    prow = jnp.where(first, 0.0, prev8[SUBLANES - 1:SUBLANES, :])
    nrow = jnp.where(last, 0.0, next8[0:1, :])
    prev = jnp.where(row == 0, prow, pltpu.roll(u, 1, 0))
    nxt = jnp.where(row == tm - 1, nrow, pltpu.roll(u, tm - 1, 0))
    return u + mu * (0.5 * (prev + nxt) - u)


def _rwkv_in_kernel(x_ref, xp_ref, xn_ref, mult_ref, shift_ref, w_ref,
                    mua_ref, mub_ref, w2_ref, w0_ref, a2_ref, a0_ref, kk_ref, ka_ref, seg_ref,
                    r_ref, v_ref, kkn_ref, g_ref, lw_ref, kd_ref, bb_ref, *, tiles_per_seq, c):
    i = pl.program_id(0)
    first = (i % tiles_per_seq) == 0
    last = (i % tiles_per_seq) == tiles_per_seq - 1
    tm = x_ref.shape[0]
    x_ext = jnp.concatenate([xp_ref[...], x_ref[...], xn_ref[...]], axis=0)
    h = _normmod(x_ext, mult_ref[...], shift_ref[...])
    lo, hi = SUBLANES, SUBLANES + tm

    proj = _dot(h, w_ref[...])

    def proj_cols(c0, c1):
        return proj[:, c0:c1]

    def shifted(c0, c1, mu):
        p = proj_cols(c0, c1)
        return _token_shift(p[lo:hi], p[:lo], p[hi:], mu, first, last)

    mb = shifted(4 * c, w_ref.shape[1], mub_ref[...])
    nl = 2 * W_LORA
    w_raw = w0_ref[...] + _dot(jnp.tanh(mb[:, :nl]).astype(BF16), w2_ref[...])
    a = _sigmoid(a0_ref[...] + _dot(mb[:, nl:].astype(BF16), a2_ref[...]))
    logw = -math.exp(-0.5) * _sigmoid(w_raw)
    for d in range(2):
        lw_ref[d] = logw[:, d * c:(d + 1) * c]

    k = shifted(c, 2 * c, mua_ref[:, c:2 * c])
    kk = k * kk_ref[...]
    kkn = kk * lax.rsqrt(_seg_sum(kk * kk, seg_ref[...]) + 1e-12)
    kkn_ref[...] = kkn
    for d in range(2):
        a_d = a[:, d * c:(d + 1) * c]
        kd_ref[d] = k * (1.0 + (a_d - 1.0) * ka_ref[...])
        bb_ref[d] = kkn * a_d

    r_ref[...] = shifted(0, c, mua_ref[:, :c])
    v_ref[...] = shifted(2 * c, 3 * c, mua_ref[:, 2 * c:3 * c])
    g_ref[...] = proj_cols(3 * c, 4 * c)[lo:hi]


def _rwkv_in(x2d, mult, shift, w_in, seq_len, rows_per_group, mu, w0, w2, a0, a2, k_k, k_a, seg, tm):
    r_rows, dm = x2d.shape
    ncols = w_in.shape[1]
    c = k_k.shape[-1]
    nb = ncols - 4 * c
    assert nb == 2 * (W_LORA + A_LORA) and seq_len % tm == 0 and rows_per_group % tm == 0
    tps = seq_len // tm
    tpg = rows_per_group // tm
    t8 = tm // SUBLANES
    nblk8 = r_rows // SUBLANES
    mua = mu[:3 * c].reshape(1, 3 * c)
    mub = mu[3 * c:].reshape(1, nb)
    zw = jnp.zeros((W_LORA, c), F32)
    w2cat = jnp.concatenate([jnp.concatenate([w2[0], zw], 1), jnp.concatenate([zw, w2[1]], 1)], 0).astype(BF16)
    a2cat = jnp.concatenate([jnp.concatenate([a2[0], zw], 1), jnp.concatenate([zw, a2[1]], 1)], 0).astype(BF16)
    w0cat = w0.reshape(1, 2 * c)
    a0cat = a0.reshape(1, 2 * c)
    full = lambda shape: pl.BlockSpec(shape, lambda i: (0,) * len(shape))
    prev_idx = lambda i: jnp.maximum(i * t8 - 1, 0)
    next_idx = lambda i: jnp.minimum((i + 1) * t8, nblk8 - 1)
    row_out = pl.BlockSpec((tm, c), lambda i: (i, 0))
    dir_out = pl.BlockSpec((2, tm, c), lambda i: (0, i, 0))
    group = pl.BlockSpec((None, 1, dm), lambda i: (i // tpg, 0, 0))
    kern = functools.partial(_rwkv_in_kernel, tiles_per_seq=tps, c=c)
    return pl.pallas_call(
        kern,
        grid=(r_rows // tm,),
        in_specs=[pl.BlockSpec((tm, dm), lambda i: (i, 0)),
                  pl.BlockSpec((SUBLANES, dm), lambda i: (prev_idx(i), 0)),
                  pl.BlockSpec((SUBLANES, dm), lambda i: (next_idx(i), 0)),
                  group, group, full((dm, ncols)),
                  full((1, 3 * c)), full((1, nb)),
                  full((2 * W_LORA, 2 * c)), full((1, 2 * c)),
                  full((2 * A_LORA, 2 * c)), full((1, 2 * c)),
                  full((1, c)), full((1, c)), full((2 * LANES, LANES))],
        out_specs=[row_out, row_out, row_out, row_out, dir_out, dir_out, dir_out],
        out_shape=[jax.ShapeDtypeStruct((r_rows, c), F32)] * 4
                  + [jax.ShapeDtypeStruct((2, r_rows, c), F32)] * 3,
        compiler_params=_cparams(("arbitrary",)),
        name="rwkv_in",
    )(x2d, x2d, x2d, mult, shift, w_in, mua, mub, w2cat, w0cat, a2cat, a0cat,
      k_k.reshape(1, c), k_a.reshape(1, c), seg)


def _head_pair_stack(xp, lane_a):
    return jnp.concatenate([jnp.where(lane_a, xp, 0.0), jnp.where(lane_a, 0.0, xp)], axis=0)


def _wkv_chunk_operands(r, v, kk, lw, kd, bb, reverse, L):
    ti = lax.broadcasted_iota(jnp.int32, (L, L), 0)
    si = lax.broadcasted_iota(jnp.int32, (L, L), 1)
    tri = ((ti <= si) if reverse else (ti >= si)).astype(BF16)
    hi = lw.astype(BF16)
    r1 = lw - hi.astype(F32)
    mid = r1.astype(BF16)
    lo = (r1 - mid.astype(F32)).astype(BF16)
    cum = _dot(tri, hi) + _dot(tri, mid) + _dot(tri, lo)
    tot = jnp.sum(lw, axis=0, keepdims=True)
    ginv = jnp.exp(-cum)
    gend = jnp.exp(tot - cum)
    return dict(kt=kk * jnp.exp(cum - lw), rt=r * jnp.exp(cum), bt=bb * ginv,
                kdt=kd * ginv, khat=kd * gend, bhat=bb * gend, v=v, tot=tot)


def _wkv_scan_kernel(rf_ref, vf_ref, kkf_ref, lwf_ref, kdf_ref, bbf_ref,
                     rb_ref, vb_ref, kkb_ref, lwb_ref, kdb_ref, bbb_ref, s0_ref,
                     of_ref, ob_ref, sfin_ref, h_scr, *, chunk, n_pairs):
    L = chunk
    ci = pl.program_id(0)
    nc = pl.num_programs(0)
    nb = rf_ref.shape[0]

    @pl.when(ci == 0)
    def _():
        h_scr[...] = s0_ref[...]

    in_refs = ((rf_ref, vf_ref, kkf_ref, lwf_ref, kdf_ref, bbf_ref),
               (rb_ref, vb_ref, kkb_ref, lwb_ref, kdb_ref, bbb_ref))
    ops = {(b, d): _wkv_chunk_operands(*[ref[b] for ref in in_refs[d]], d == 1, L)
           for b in range(nb) for d in range(2)}
    o_refs = (of_ref, ob_ref)

    p2 = 2 * L
    lane_a = lax.broadcasted_iota(jnp.int32, (L, LANES), 1) < RW_HEAD
    dts = ((lax.broadcasted_iota(jnp.int32, (p2, p2), 0) & (L - 1))
           - (lax.broadcasted_iota(jnp.int32, (p2, p2), 1) & (L - 1)))
    strict = (dts > 0, dts < 0)
    incl = (dts >= 0, dts <= 0)
    eye = (lax.broadcasted_iota(jnp.int32, (p2, p2), 0)
           == lax.broadcasted_iota(jnp.int32, (p2, p2), 1)).astype(F32)

    chains = [(b, d, p) for p in range(n_pairs) for b in range(nb) for d in range(2)]
    lanes_of = lambda p: slice(p * LANES, (p + 1) * LANES)
    stk = lambda name, b, d, p: _head_pair_stack(ops[b, d][name][:, lanes_of(p)], lane_a)
    kt_s = [stk("kt", *ch).astype(BF16) for ch in chains]
    rt_s = [stk("rt", *ch).astype(BF16) for ch in chains]
    y_s = [jnp.concatenate([stk("bt", *ch).astype(BF16), stk("kdt", *ch).astype(BF16)], axis=0)
           for ch in chains]
    v_s = [stk("v", *ch).astype(BF16) for ch in chains]
    h = [h_scr[ch] for ch in chains]
    h_b = [x.astype(BF16) for x in h]
    n = range(len(chains))
    sc = [_dot_nt(jnp.concatenate([kt_s[i], rt_s[i]], axis=0), y_s[i]) for i in n]
    rh = [_dot(rt_s[i], h_b[i]) for i in n]
    msk = lambda m, x: jnp.where(m, x, 0.0)
    pw = [-msk(strict[d], sc[i][:p2, :p2]) for i, (b, d, p) in enumerate(chains)]
    a_k = [msk(strict[d], sc[i][:p2, p2:]).astype(BF16) for i, (b, d, p) in enumerate(chains)]
    pkb = [jnp.concatenate([msk(incl[d], sc[i][p2:, p2:]), -msk(incl[d], sc[i][p2:, :p2])],
                           axis=1).astype(BF16) for i, (b, d, p) in enumerate(chains)]
    u = [_dot(jnp.concatenate([kt_s[i], a_k[i]], axis=1), jnp.concatenate([h_b[i], v_s[i]], axis=0))
         for i in n]

    pwb = [x.astype(BF16) for x in pw]
    tinv = [eye + pw[i] for i in n]
    pw = [_dot(pwb[i], pwb[i]) for i in n]
    for j in range(1, int(math.log2(L))):
        pwb = [x.astype(BF16) for x in pw]
        if j < int(math.log2(L)) - 1:
            w = [_dot(jnp.concatenate([tinv[i].astype(BF16), pwb[i]], axis=0), pwb[i]) for i in n]
            tinv = [tinv[i] + w[i][:p2] for i in n]
            pw = [w[i][p2:] for i in n]
        else:
            tinv = [tinv[i] + _dot(tinv[i].astype(BF16), pwb[i]) for i in n]
    u = [_dot(tinv[i].astype(BF16), u[i].astype(BF16)) for i in n]

    vu = [jnp.concatenate([v_s[i], u[i].astype(BF16)], axis=0) for i in n]
    for i, (b, d, p) in enumerate(chains):
        o_s = rh[i] + _dot(pkb[i], vu[i])
        o_refs[d][b, :, lanes_of(p)] = o_s[:L] + o_s[L:]
    for i, (b, d, p) in enumerate(chains):
        kb = jnp.concatenate([stk("khat", b, d, p), -stk("bhat", b, d, p)], axis=0).astype(BF16)
        tot_col = jnp.broadcast_to(ops[b, d]["tot"][:, lanes_of(p)], (LANES, LANES)).T
        h_scr[b, d, p] = jnp.exp(tot_col) * h[i] + _dot_tn(kb, vu[i])

    @pl.when(ci == nc - 1)
    def _():
        sfin_ref[...] = h_scr[...]


def _wkv_scan(r, v, kkn, logw, kd, bb, s0, chunk=SCAN_CHUNK):
    b, t, c = r.shape
    n_pairs = c // LANES
    assert t % chunk == 0 and 2 * RW_HEAD == LANES
    nc = t // chunk
    fwd = lambda ci: ci
    bwd = lambda ci: nc - 1 - ci
    shared = lambda cmap: pl.BlockSpec((b, chunk, c), lambda ci: (0, cmap(ci), 0))
    perdir = lambda d, cmap: pl.BlockSpec((None, b, chunk, c), lambda ci: (d, 0, cmap(ci), 0))
    state = pl.BlockSpec((b, 2, n_pairs, LANES, LANES), lambda ci: (0, 0, 0, 0, 0))
    kern = functools.partial(_wkv_scan_kernel, chunk=chunk, n_pairs=n_pairs)
    return pl.pallas_call(
        kern,
        grid=(nc,),
        in_specs=[shared(fwd)] * 3 + [perdir(0, fwd)] * 3 + [shared(bwd)] * 3 + [perdir(1, bwd)] * 3 + [state],
        out_specs=[shared(fwd), shared(bwd), state],
        out_shape=[jax.ShapeDtypeStruct((b, t, c), F32), jax.ShapeDtypeStruct((b, t, c), F32),
                   jax.ShapeDtypeStruct((b, 2, n_pairs, LANES, LANES), F32)],
        scratch_shapes=[pltpu.VMEM((b, 2, n_pairs, LANES, LANES), F32)],
        compiler_params=_cparams(("arbitrary",)),
        name="wkv_scan",
    )(r, v, kkn, logw, kd, bb, r, v, kkn, logw, kd, bb, s0)


def _rwkv_readout_kernel(of_ref, ob_ref, r_ref, v_ref, kd_ref, g_ref, x_ref, gate_ref,
                         rk_ref, gng_ref, gnb_ref, seg_ref, w_ref, out_ref):
    seg = seg_ref[...]
    inv_n = 1.0 / RW_HEAD
    o = of_ref[...] + ob_ref[...]
    mean = _seg_sum(o, seg) * inv_n
    dlt = o - mean
    var = _seg_sum(dlt * dlt, seg) * inv_n
    y = dlt * lax.rsqrt(var + GN_EPS) * gng_ref[...] + gnb_ref[...]
    bonus = _seg_sum(r_ref[...] * (kd_ref[0] + kd_ref[1]) * rk_ref[...], seg) * v_ref[...]
    y = (y + bonus) * _silu(g_ref[...])
    out_ref[...] = x_ref[...] + gate_ref[...] * _dot(y.astype(BF16), w_ref[...])


def _rwkv_readout(o_f, o_b, r, v, kd, g, x2d, gate, r_k, gn_g, gn_b, seg, w_out, rows_per_group, tm):
    r_rows, c = r.shape
    d = x2d.shape[1]
    tpg = rows_per_group // tm
    full = lambda shape: pl.BlockSpec(shape, lambda i: (0,) * len(shape))
    row = lambda n: pl.BlockSpec((tm, n), lambda i: (i, 0))
    dirs = pl.BlockSpec((2, tm, c), lambda i: (0, i, 0))
    return pl.pallas_call(
        _rwkv_readout_kernel,
        grid=(r_rows // tm,),
        in_specs=[row(c), row(c), row(c), row(c), dirs,
                  row(c),
                  row(d),
                  pl.BlockSpec((None, 1, d), lambda i: (i // tpg, 0, 0)),
                  full((1, c)), full((1, c)), full((1, c)), full((2 * LANES, LANES)), full((c, d))],
        out_specs=row(d),
        out_shape=jax.ShapeDtypeStruct((r_rows, d), F32),
        compiler_params=_cparams(("arbitrary",)),
        name="rwkv_readout",
    )(o_f, o_b, r, v, kd, g, x2d, gate, r_k.reshape(1, c), gn_g.reshape(1, c), gn_b.reshape(1, c), seg,
      w_out)


def _head_norm(x, gain, seg):
    ms = _seg_sum(x * x, seg) * (1.0 / DA_HEAD)
    return x * lax.rsqrt(ms + RMS_EPS) * gain


def _rope(x, cos, sin_signed):
    n = x.shape[-1]
    lane = lax.broadcasted_iota(jnp.int32, x.shape, 1)
    half0 = (lane & (2 * ROPE_FREQS - 1)) < ROPE_FREQS
    partner = jnp.where(half0, pltpu.roll(x, n - ROPE_FREQS, 1), pltpu.roll(x, ROPE_FREQS, 1))
    reps = n // cos.shape[-1]
    wide = lambda tab: jnp.concatenate([tab] * reps, axis=1)
    return x * wide(cos) + partner * wide(sin_signed)


def _attn_in_kernel(*refs, latent, q_scale):
    it = iter(refs)
    x_ref, mult_ref, shift_ref, w_ref = next(it), next(it), next(it), next(it)
    cos_ref = next(it) if latent else None
    sin_ref = next(it) if latent else None
    qn_ref = next(it) if latent else None
    kn_ref, seg_ref = next(it), next(it)
    qo_ref = next(it) if latent else None
    ko_ref, vo_ref = next(it), next(it)
    go_ref = next(it) if latent else None
    h = _normmod(x_ref[...], mult_ref[...], shift_ref[...])
    w = ko_ref.shape[-1]
    proj = _dot(h, w_ref[...])
    col = lambda j: proj[:, j * w:(j + 1) * w]
    seg = seg_ref[...]
    kcol = 1 if latent else 0
    k = _head_norm(col(kcol), kn_ref[...], seg)
    if latent:
        k = _rope(k, cos_ref[...], sin_ref[...])
    ko_ref[...] = k.astype(BF16)
    if latent:
        q = _rope(_head_norm(col(0), qn_ref[...], seg), cos_ref[...], sin_ref[...])
        qo_ref[...] = (q * q_scale).astype(BF16)
    vo_ref[...] = col(kcol + 1).T.astype(BF16)
    if latent:
        go_ref[...] = col(3)


def _attn_in(x2d, mult, shift, w_in, latent, cos, sin_signed, qn, kn, seg, w, seq_len, rows_per_group, tm):
    r_rows, dm = x2d.shape
    tps = seq_len // tm
    tpg = rows_per_group // tm
    full = lambda shape: pl.BlockSpec(shape, lambda i: (0,) * len(shape))
    group = pl.BlockSpec((None, 1, dm), lambda i: (i // tpg, 0, 0))
    tab = pl.BlockSpec((tm, LANES), lambda i: (i % tps, 0))
    tile_gain = lambda g: jnp.tile(g, w // g.shape[0]).reshape(1, w)
    in_specs = [pl.BlockSpec((tm, dm), lambda i: (i, 0)), group, group, full(w_in.shape)]
    args = [x2d, mult, shift, w_in]
    if latent:
        in_specs += [tab, tab, full((1, w))]
        args += [cos, sin_signed, tile_gain(qn)]
    in_specs += [full((1, w)), full((2 * LANES, LANES))]
    args += [tile_gain(kn), seg]
    rows_bf = (pl.BlockSpec((tm, w), lambda i: (i, 0)), jax.ShapeDtypeStruct((r_rows, w), BF16))
    rows_f32 = (pl.BlockSpec((tm, w), lambda i: (i, 0)), jax.ShapeDtypeStruct((r_rows, w), F32))
    v_t = (pl.BlockSpec((None, w, tm), lambda i: (i // tps, 0, i % tps)),
           jax.ShapeDtypeStruct((r_rows // seq_len, w, seq_len), BF16))
    outs = [rows_bf, rows_bf, v_t, rows_f32] if latent else [rows_bf, v_t]
    kern = functools.partial(_attn_in_kernel, latent=latent, q_scale=DA_HEAD ** -0.5)
    return pl.pallas_call(
        kern,
        grid=(r_rows // tm,),
        in_specs=in_specs,
        out_specs=[o[0] for o in outs],
        out_shape=[o[1] for o in outs],
        compiler_params=_cparams(("arbitrary",)),
        name="attn_in_lat" if latent else "attn_in_ctx",
    )(*args)


def _diff_attn_kernel(q_ref, kc_ref, kl_ref, vtc_ref, vtl_ref, lam_ref, sub_ref, o_ref, m_scr, l_scr, acc_scr,
                      *, lam_init, tk, bounded):
    q = q_ref[...]
    lane = lax.broadcasted_iota(jnp.int32, q.shape, 1)
    zero = jnp.zeros_like(q)
    qm = (jnp.where(lane < DA_HEAD, q, zero), jnp.where(lane < DA_HEAD, zero, q))
    m_scr[...] = jnp.full(m_scr.shape, -jnp.inf, F32)
    l_scr[...] = jnp.zeros(l_scr.shape, F32)
    acc_scr[...] = jnp.zeros(acc_scr.shape, F32)

    def accumulate(k, vt):
        for c in range(2):
            s = _dot_nt(k, qm[c])
            if bounded:
                p = jnp.exp(s)
                l_scr[c] += jnp.sum(p, axis=0, keepdims=True)
                acc_scr[c] += _dot(vt, p.astype(BF16))
            else:
                m_prev = m_scr[c]
                m_new = jnp.maximum(m_prev, jnp.max(s, axis=0, keepdims=True))
                alpha = jnp.exp(m_prev - m_new)
                p = jnp.exp(s - m_new)
                l_scr[c] = alpha * l_scr[c] + jnp.sum(p, axis=0, keepdims=True)
                acc_scr[c] = alpha * acc_scr[c] + _dot(vt, p.astype(BF16))
                m_scr[c] = m_new

    def body(j, carry):
        off = pl.multiple_of(j * tk, tk)
        accumulate(kl_ref[pl.ds(off, tk), :], vtl_ref[:, pl.ds(off, tk)])
        return carry

    accumulate(kc_ref[...], vtc_ref[...])
    lax.fori_loop(0, kl_ref.shape[0] // tk, body, 0, unroll=bounded)

    lv = lam_ref[...]
    lam = (jnp.exp(jnp.sum(lv[0:1] * lv[1:2], axis=-1, keepdims=True))
           - jnp.exp(jnp.sum(lv[2:3] * lv[3:4], axis=-1, keepdims=True)) + lam_init)
    o = acc_scr[0] / l_scr[0] - lam * (acc_scr[1] / l_scr[1])
    y = o * lax.rsqrt(jnp.mean(o * o, axis=0, keepdims=True) + SUBLN_EPS)
    o_ref[...] = (y * (sub_ref[...] * (1.0 - lam_init))).T


def _diff_attn(q, k_ctx, k_lat, vt_ctx, vt_lat, lam_vecs, subln, lam_init, tq, tk, bounded):
    b, t, w = q.shape
    lc = k_ctx.shape[1]
    nh = w // DA_VHEAD
    assert t % tq == 0 and t % tk == 0
    kern = functools.partial(_diff_attn_kernel, lam_init=lam_init, tk=tk, bounded=bounded)
    return pl.pallas_call(
        kern,
        grid=(b, nh, t // tq),
        in_specs=[pl.BlockSpec((None, tq, DA_VHEAD), lambda bi, h, i: (bi, i, h)),
                  pl.BlockSpec((None, lc, DA_VHEAD), lambda bi, h, i: (bi, 0, h)),
                  pl.BlockSpec((None, t, DA_VHEAD), lambda bi, h, i: (bi, 0, h)),
                  pl.BlockSpec((None, DA_VHEAD, lc), lambda bi, h, i: (bi, h, 0)),
                  pl.BlockSpec((None, DA_VHEAD, t), lambda bi, h, i: (bi, h, 0)),
                  pl.BlockSpec((4, DA_HEAD), lambda bi, h, i: (0, 0)),
                  pl.BlockSpec((DA_VHEAD, 1), lambda bi, h, i: (0, 0))],
        out_specs=pl.BlockSpec((None, tq, DA_VHEAD), lambda bi, h, i: (bi, i, h)),
        out_shape=jax.ShapeDtypeStruct((b, t, w), F32),
        scratch_shapes=[pltpu.VMEM((2, 1, tq), F32), pltpu.VMEM((2, 1, tq), F32),
                        pltpu.VMEM((2, DA_VHEAD, tq), F32)],
        compiler_params=_cparams(("arbitrary", "arbitrary", "arbitrary")),
        name="diff_attn_bounded" if bounded else "diff_attn",
    )(q, k_ctx, k_lat, vt_ctx, vt_lat, lam_vecs, subln.reshape(DA_VHEAD, 1))


def _gated_out_kernel(y_ref, g_ref, x_ref, gate_ref, w_ref, out_ref):
    y = y_ref[...] * _silu(g_ref[...])
    out_ref[...] = x_ref[...] + gate_ref[...] * _dot(y.astype(BF16), w_ref[...])


def _gated_out(y, g, x2d, gate, w_out, rows_per_group, tm):
    r_rows, w = y.shape
    d = x2d.shape[1]
    tpg = rows_per_group // tm
    row = lambda n: pl.BlockSpec((tm, n), lambda i: (i, 0))
    return pl.pallas_call(
        _gated_out_kernel,
        grid=(r_rows // tm,),
        in_specs=[row(w), row(w), row(d),
                  pl.BlockSpec((None, 1, d), lambda i: (i // tpg, 0, 0)),
                  pl.BlockSpec((w, d), lambda i: (0, 0))],
        out_specs=row(d),
        out_shape=jax.ShapeDtypeStruct((r_rows, d), F32),
        compiler_params=_cparams(("arbitrary",)),
        name="gated_out",
    )(y, g, x2d, gate, w_out)


def _segment_matrix(width):
    i = (lax.broadcasted_iota(jnp.int32, (2 * LANES, LANES), 0) % LANES) // width
    j = lax.broadcasted_iota(jnp.int32, (2 * LANES, LANES), 1) // width
    return (i == j).astype(BF16)


def _rope_tables(rows):
    row_ids = jnp.repeat(jnp.arange(rows), GRID_W).astype(F32)
    col_ids = jnp.tile(jnp.arange(GRID_W), rows).astype(F32)
    inv_freq = ROPE_THETA ** (-jnp.arange(ROPE_FREQS, dtype=F32) / ROPE_FREQS)
    ang_r = row_ids[:, None] * inv_freq
    ang_c = col_ids[:, None] * inv_freq
    cr, sr, cc, sc = jnp.cos(ang_r), jnp.sin(ang_r), jnp.cos(ang_c), jnp.sin(ang_c)
    cos = jnp.concatenate([cr, cr, cc, cc], axis=-1)
    sin_signed = jnp.concatenate([-sr, sr, -sc, sc], axis=-1)
    return jnp.tile(cos, (1, 2)), jnp.tile(sin_signed, (1, 2))


def kernel(x, c, ctx, c_ctx, ada_w, ada_b, norm_g, rw_in, rw_mu, rw_w0, rw_w2, rw_a0, rw_a2, rw_kk, rw_ka, rw_rk, rw_gn_g, rw_gn_b, rw_out, da_in, da_qn, da_kn, da_lam, da_subln, da_out):
    bsz, t, d = x.shape
    lc = ctx.shape[1]
    depth = ada_w.shape[0]
    assert depth == 2 and bsz + 1 <= SUBLANES
    cw = rw_kk.shape[-1]
    x2 = x.reshape(bsz * t, d)
    xc2 = ctx.reshape(bsz * lc, d)

    rows = jnp.zeros((SUBLANES, d), F32).at[:bsz].set(c).at[bsz].set(c_ctx)
    mod = _ada_mod(rows, ada_w, ada_b)
    shift, scale, gate = mod[:, :, :d], mod[:, :, d:2 * d], mod[:, :, 2 * d:]
    mult = norm_g[:, None, :] * (1.0 + scale)
    lat = lambda a, i: a[i, :bsz].reshape(bsz, 1, d)
    con = lambda a, i: a[i, bsz:bsz + 1].reshape(1, 1, d)

    tm_l = 256
    tm_c = min(256, lc)

    w_in0 = rw_in[0].astype(BF16)
    seg64 = _segment_matrix(RW_HEAD)
    rwkv_in = functools.partial(_rwkv_in, w_in=w_in0, mu=rw_mu[0], w0=rw_w0[0], w2=rw_w2[0], a0=rw_a0[0],
                                a2=rw_a2[0], k_k=rw_kk[0], k_a=rw_ka[0], seg=seg64)
    r_l, v_l, kk_l, g_l, lw_l, kd_l, bb_l = rwkv_in(x2, lat(mult, 0), lat(shift, 0), seq_len=t,
                                                    rows_per_group=t, tm=tm_l)
    r_c, v_c, kk_c, g_c, lw_c, kd_c, bb_c = rwkv_in(xc2, con(mult, 0), con(shift, 0), seq_len=lc,
                                                    rows_per_group=bsz * lc, tm=tm_c)

    s0 = jnp.zeros((bsz, 2, cw // LANES, LANES, LANES), F32)
    b3 = lambda a, n: a.reshape(bsz, n, cw)
    b4 = lambda a, n: a.reshape(2, bsz, n, cw)
    ocf, ocb, s_c = _wkv_scan(b3(r_c, lc), b3(v_c, lc), b3(kk_c, lc), b4(lw_c, lc), b4(kd_c, lc),
                              b4(bb_c, lc), s0)
    olf, olb, _ = _wkv_scan(b3(r_l, t), b3(v_l, t), b3(kk_l, t), b4(lw_l, t), b4(kd_l, t), b4(bb_l, t), s_c)

    w_out0 = rw_out[0].astype(BF16)
    readout = functools.partial(_rwkv_readout, r_k=rw_rk[0].reshape(-1), gn_g=rw_gn_g[0], gn_b=rw_gn_b[0],
                                seg=seg64, w_out=w_out0)
    x1 = readout(olf.reshape(bsz * t, cw), olb.reshape(bsz * t, cw), r_l, v_l, kd_l, g_l, x2,
                 lat(gate, 0), rows_per_group=t, tm=tm_l)
    xc1 = readout(ocf.reshape(bsz * lc, cw), ocb.reshape(bsz * lc, cw), r_c, v_c, kd_c, g_c, xc2,
                  con(gate, 0), rows_per_group=bsz * lc, tm=tm_c)

    qw = DA_HEADS * 2 * DA_HEAD
    w_in1 = da_in[0].astype(BF16)
    seg_da = _segment_matrix(DA_HEAD)
    cos, sin_signed = _rope_tables(t // GRID_W)
    q_b, k_b, vt_b, g2_l = _attn_in(x1, lat(mult, 1), lat(shift, 1), w_in1, True, cos, sin_signed, da_qn[0],
                                    da_kn[0], seg_da, qw, seq_len=t, rows_per_group=t, tm=tm_l)
    kc_b, vtc_b = _attn_in(xc1, con(mult, 1), con(shift, 1), w_in1[:, qw:3 * qw], False, None, None, None,
                           da_kn[0], seg_da, qw, seq_len=lc, rows_per_group=bsz * lc, tm=tm_c)
    lam_init = 0.8 - 0.6 * math.exp(-0.3 * 1)
    attn = functools.partial(_diff_attn, lam_vecs=da_lam[0], subln=da_subln[0], lam_init=lam_init,
                             tq=min(1024, t), tk=min(1024, t))
    score_bound = DA_HEAD ** 0.5 * jnp.max(jnp.abs(da_qn[0])) * jnp.max(jnp.abs(da_kn[0]))
    y_att = lax.cond(score_bound < MAX_UNSHIFTED_SCORE,
                     functools.partial(attn, bounded=True), functools.partial(attn, bounded=False),
                     q_b.reshape(bsz, t, qw), kc_b.reshape(bsz, lc, qw), k_b.reshape(bsz, t, qw), vtc_b, vt_b)
    out = _gated_out(y_att.reshape(bsz * t, qw), g2_l, x1, lat(gate, 1), da_out[0].astype(BF16),
                     rows_per_group=t, tm=tm_l)
    return out.reshape(bsz, t, d)
```

```python
import functools
import math

import jax
import jax.numpy as jnp
from jax import lax
from jax.experimental import pallas as pl
from jax.experimental.pallas import tpu as pltpu

F32 = jnp.float32
BF16 = jnp.bfloat16

GRID_W = 64
RMS_EPS = 1e-6
RW_HEAD = 64
W_LORA = 64
A_LORA = 64
GN_EPS = 64e-5
DA_HEADS = 8
DA_HEAD = 64
DA_VHEAD = 2 * DA_HEAD
SUBLN_EPS = 1e-5
ROPE_THETA = 10000.0
ROPE_FREQS = DA_HEAD // 4

LANES = 128
SUBLANES = 8
SCAN_CHUNK = 64
VMEM_LIMIT = 56 * 1024 * 1024
MAX_UNSHIFTED_SCORE = 40.0


def _cparams(sem):
    return pltpu.CompilerParams(dimension_semantics=sem, vmem_limit_bytes=VMEM_LIMIT)


def _dot(a, b):
    return jnp.dot(a, b, preferred_element_type=F32)


def _dot_nt(a, b):
    return lax.dot_general(a, b, (((1,), (1,)), ((), ())), preferred_element_type=F32)


def _dot_tn(a, b):
    return lax.dot_general(a, b, (((0,), (0,)), ((), ())), preferred_element_type=F32)


def _split2(x):
    hi = x.astype(BF16)
    lo = (x - hi.astype(F32)).astype(BF16)
    return hi, lo


def _seg_sum(x, seg2):
    out = []
    for j in range(x.shape[-1] // LANES):
        hi, lo = _split2(x[:, j * LANES:(j + 1) * LANES])
        out.append(_dot(jnp.concatenate([hi, lo], axis=1), seg2))
    return jnp.concatenate(out, axis=1)


def _sigmoid(x):
    return 1.0 / (1.0 + jnp.exp(-x))


def _silu(x):
    return x * _sigmoid(x)


def _ada_kernel(c_ref, w_ref, b_ref, o_ref):
    s = _silu(c_ref[...])
    o_ref[...] = jnp.dot(s, w_ref[...], preferred_element_type=F32,
                         precision=lax.Precision.HIGHEST) + b_ref[...]


def _ada_mod(rows, ada_w, ada_b):
    depth, d, n = ada_w.shape
    tn = 768
    return pl.pallas_call(
        _ada_kernel,
        grid=(depth, n // tn),
        in_specs=[pl.BlockSpec((SUBLANES, d), lambda i, j: (0, 0)),
                  pl.BlockSpec((None, d, tn), lambda i, j: (i, 0, j)),
                  pl.BlockSpec((None, 1, tn), lambda i, j: (i, 0, j))],
        out_specs=pl.BlockSpec((None, SUBLANES, tn), lambda i, j: (i, 0, j)),
        out_shape=jax.ShapeDtypeStruct((depth, SUBLANES, n), F32),
        compiler_params=_cparams(("arbitrary", "arbitrary")),
        name="ada_mod",
    )(rows, ada_w, ada_b.reshape(depth, 1, n))


def _normmod(x, mult, shift):
    y = x * lax.rsqrt(jnp.mean(x * x, axis=-1, keepdims=True) + RMS_EPS)
    return (y * mult + shift).astype(BF16)


def _token_shift(u, prev8, next8, mu, first, last):
    tm = u.shape[0]
    row8 = lax.broadcasted_iota(jnp.int32, (SUBLANES, u.shape[1]), 0)
    prow = jnp.where(first, 0.0, prev8[SUBLANES - 1:SUBLANES, :])
    nrow = jnp.where(last, 0.0, next8[0:1, :])
    nbr = pltpu.roll(u, 1, 0) + pltpu.roll(u, tm - 1, 0)
    top = nbr[:SUBLANES] + jnp.where(row8 == 0, prow - u[tm - 1:tm], 0.0)
    bot = nbr[tm - SUBLANES:] + jnp.where(row8 == SUBLANES - 1, nrow - u[0:1], 0.0)
    nbr = jnp.concatenate([top, nbr[SUBLANES:tm - SUBLANES], bot], axis=0)
    return (1.0 - mu) * u + (0.5 * mu) * nbr


def _rwkv_in_kernel(x_ref, xp_ref, xn_ref, mult_ref, shift_ref, w_ref,
                    mua_ref, mub_ref, w2_ref, w0_ref, a2_ref, a0_ref, kk_ref, ka_ref, rk_ref, seg_ref,
                    r_ref, v_ref, kkn_ref, g_ref, bonus_ref, lw_ref, kd_ref, bb_ref, *, tiles_per_seq, c):
    i = pl.program_id(0)
    first = (i % tiles_per_seq) == 0
    last = (i % tiles_per_seq) == tiles_per_seq - 1
    tm = x_ref.shape[0]
    lo, hi = SUBLANES, SUBLANES + tm
    x_ext = jnp.concatenate([xp_ref[...], x_ref[...], xn_ref[...]], axis=0)
    h = _normmod(x_ext, mult_ref[...], shift_ref[...])
    proj = _dot(h, w_ref[...])

    def shifted(c0, c1, mu):
        p = proj[:, c0:c1]
        return _token_shift(p[lo:hi], p[:lo], p[hi:], mu, first, last)

    mb = shifted(4 * c, w_ref.shape[1], mub_ref[...])
    nl = 2 * W_LORA
    w_raw = w0_ref[...] + _dot(jnp.tanh(mb[:, :nl]).astype(BF16), w2_ref[...])
    a = _sigmoid(a0_ref[...] + _dot(mb[:, nl:].astype(BF16), a2_ref[...]))
    logw = -math.exp(-0.5) / (1.0 + jnp.exp(-w_raw))
    for d in range(2):
        lw_ref[d] = logw[:, d * c:(d + 1) * c]

    k = shifted(c, 2 * c, mua_ref[:, c:2 * c])
    kk = k * kk_ref[...]
    kkn = kk * lax.rsqrt(_seg_sum(kk * kk, seg_ref[...]) + 1e-12)
    kkn_ref[...] = kkn
    kka = k * ka_ref[...]
    k_rest = k - kka
    kd = [k_rest + kka * a[:, d * c:(d + 1) * c] for d in range(2)]
    for d in range(2):
        kd_ref[d] = kd[d]
        bb_ref[d] = kkn * a[:, d * c:(d + 1) * c]

    r = shifted(0, c, mua_ref[:, :c])
    v = shifted(2 * c, 3 * c, mua_ref[:, 2 * c:3 * c])
    r_ref[...] = r
    v_ref[...] = v
    bonus_ref[...] = _seg_sum(r * (kd[0] + kd[1]) * rk_ref[...], seg_ref[...]) * v
    g_ref[...] = proj[lo:hi, 3 * c:4 * c]


def _rwkv_in(x2d, mult, shift, w_in, seq_len, rows_per_group, mu, w0, w2, a0, a2, k_k, k_a, r_k, seg, tm):
    r_rows, dm = x2d.shape
    ncols = w_in.shape[1]
    c = k_k.shape[-1]
    nb = ncols - 4 * c
    assert nb == 2 * (W_LORA + A_LORA) and seq_len % tm == 0 and rows_per_group % tm == 0
    tps = seq_len // tm
    tpg = rows_per_group // tm
    t8 = tm // SUBLANES
    nblk8 = r_rows // SUBLANES
    mua = mu[:3 * c].reshape(1, 3 * c)
    mub = mu[3 * c:].reshape(1, nb)
    zw = jnp.zeros((W_LORA, c), F32)
    w2cat = jnp.concatenate([jnp.concatenate([w2[0], zw], 1), jnp.concatenate([zw, w2[1]], 1)], 0).astype(BF16)
    a2cat = jnp.concatenate([jnp.concatenate([a2[0], zw], 1), jnp.concatenate([zw, a2[1]], 1)], 0).astype(BF16)
    w0cat = w0.reshape(1, 2 * c)
    a0cat = a0.reshape(1, 2 * c)
    full = lambda shape: pl.BlockSpec(shape, lambda i: (0,) * len(shape))
    prev_idx = lambda i: jnp.maximum(i * t8 - 1, 0)
    next_idx = lambda i: jnp.minimum((i + 1) * t8, nblk8 - 1)
    row_out = pl.BlockSpec((tm, c), lambda i: (i, 0))
    dir_out = pl.BlockSpec((2, tm, c), lambda i: (0, i, 0))
    group = pl.BlockSpec((None, 1, dm), lambda i: (i // tpg, 0, 0))
    kern = functools.partial(_rwkv_in_kernel, tiles_per_seq=tps, c=c)
    return pl.pallas_call(
        kern,
        grid=(r_rows // tm,),
        in_specs=[pl.BlockSpec((tm, dm), lambda i: (i, 0)),
                  pl.BlockSpec((SUBLANES, dm), lambda i: (prev_idx(i), 0)),
                  pl.BlockSpec((SUBLANES, dm), lambda i: (next_idx(i), 0)),
                  group, group, full((dm, ncols)),
                  full((1, 3 * c)), full((1, nb)),
                  full((2 * W_LORA, 2 * c)), full((1, 2 * c)),
                  full((2 * A_LORA, 2 * c)), full((1, 2 * c)),
                  full((1, c)), full((1, c)), full((1, c)), full((2 * LANES, LANES))],
        out_specs=[row_out] * 5 + [dir_out] * 3,
        out_shape=[jax.ShapeDtypeStruct((r_rows, c), F32)] * 5
                  + [jax.ShapeDtypeStruct((2, r_rows, c), F32)] * 3,
        compiler_params=_cparams(("arbitrary",)),
        name="rwkv_in",
    )(x2d, x2d, x2d, mult, shift, w_in, mua, mub, w2cat, w0cat, a2cat, a0cat,
      k_k.reshape(1, c), k_a.reshape(1, c), r_k.reshape(1, c), seg)


def _head_pair_stack(xp, lane_a):
    return jnp.concatenate([jnp.where(lane_a, xp, 0.0), jnp.where(lane_a, 0.0, xp)], axis=0)


def _wkv_chunk_operands(r, v, kk, lw, kd, bb, reverse, L):
    ti = lax.broadcasted_iota(jnp.int32, (L, L), 0)
    si = lax.broadcasted_iota(jnp.int32, (L, L), 1)
    tri = ((ti <= si) if reverse else (ti >= si)).astype(BF16)
    hi = lw.astype(BF16)
    r1 = lw - hi.astype(F32)
    mid = r1.astype(BF16)
    lo = (r1 - mid.astype(F32)).astype(BF16)
    cum = _dot(tri, hi) + _dot(tri, mid) + _dot(tri, lo)
    tot = jnp.sum(lw, axis=0, keepdims=True)
    ginv = jnp.exp(-cum)
    gend = jnp.exp(tot - cum)
    return dict(kt=kk * jnp.exp(cum - lw), rt=r * jnp.exp(cum), bt=bb * ginv,
                kdt=kd * ginv, khat=kd * gend, bhat=bb * gend, v=v, tot=tot)


def _wkv_scan_kernel(rf_ref, vf_ref, kkf_ref, lwf_ref, kdf_ref, bbf_ref,
                     rb_ref, vb_ref, kkb_ref, lwb_ref, kdb_ref, bbb_ref, s0_ref,
                     of_ref, ob_ref, sfin_ref, h_scr, *, chunk, n_pairs):
    L = chunk
    ci = pl.program_id(0)
    nc = pl.num_programs(0)
    nb = rf_ref.shape[0]

    @pl.when(ci == 0)
    def _():
        h_scr[...] = s0_ref[...]

    in_refs = ((rf_ref, vf_ref, kkf_ref, lwf_ref, kdf_ref, bbf_ref),
               (rb_ref, vb_ref, kkb_ref, lwb_ref, kdb_ref, bbb_ref))
    ops = {(b, d): _wkv_chunk_operands(*[ref[b] for ref in in_refs[d]], d == 1, L)
           for b in range(nb) for d in range(2)}
    o_refs = (of_ref, ob_ref)

    p2 = 2 * L
    lane_a = lax.broadcasted_iota(jnp.int32, (L, LANES), 1) < RW_HEAD
    dts = ((lax.broadcasted_iota(jnp.int32, (p2, p2), 0) & (L - 1))
           - (lax.broadcasted_iota(jnp.int32, (p2, p2), 1) & (L - 1)))
    strict = (dts > 0, dts < 0)
    incl = (dts >= 0, dts <= 0)
    eye = (lax.broadcasted_iota(jnp.int32, (p2, p2), 0)
           == lax.broadcasted_iota(jnp.int32, (p2, p2), 1)).astype(F32)

    chains = [(b, d, p) for p in range(n_pairs) for b in range(nb) for d in range(2)]
    lanes_of = lambda p: slice(p * LANES, (p + 1) * LANES)
    stk = lambda name, b, d, p: _head_pair_stack(ops[b, d][name][:, lanes_of(p)], lane_a)
    kt_s = [stk("kt", *ch).astype(BF16) for ch in chains]
    rt_s = [stk("rt", *ch).astype(BF16) for ch in chains]
    y_s = [jnp.concatenate([stk("bt", *ch).astype(BF16), stk("kdt", *ch).astype(BF16)], axis=0)
           for ch in chains]
    v_s = [stk("v", *ch).astype(BF16) for ch in chains]
    h = [h_scr[ch] for ch in chains]
    h_b = [x.astype(BF16) for x in h]
    n = range(len(chains))
    sc = [_dot_nt(jnp.concatenate([kt_s[i], rt_s[i]], axis=0), y_s[i]) for i in n]
    rh = [_dot(rt_s[i], h_b[i]) for i in n]
    msk = lambda m, x: jnp.where(m, x, 0.0)
    pw = [-msk(strict[d], sc[i][:p2, :p2]) for i, (b, d, p) in enumerate(chains)]
    a_k = [msk(strict[d], sc[i][:p2, p2:]).astype(BF16) for i, (b, d, p) in enumerate(chains)]
    pkb = [jnp.concatenate([msk(incl[d], sc[i][p2:, p2:]), -msk(incl[d], sc[i][p2:, :p2])],
                           axis=1).astype(BF16) for i, (b, d, p) in enumerate(chains)]
    u = [_dot(jnp.concatenate([kt_s[i], a_k[i]], axis=1), jnp.concatenate([h_b[i], v_s[i]], axis=0))
         for i in n]

    pwb = [x.astype(BF16) for x in pw]
    tinv = [eye + pw[i] for i in n]
    pw = [_dot(pwb[i], pwb[i]) for i in n]
    for j in range(1, int(math.log2(L))):
        pwb = [x.astype(BF16) for x in pw]
        if j < int(math.log2(L)) - 1:
            w = [_dot(jnp.concatenate([tinv[i].astype(BF16), pwb[i]], axis=0), pwb[i]) for i in n]
            tinv = [tinv[i] + w[i][:p2] for i in n]
            pw = [w[i][p2:] for i in n]
        else:
            tinv = [tinv[i] + _dot(tinv[i].astype(BF16), pwb[i]) for i in n]
    u = [_dot(tinv[i].astype(BF16), u[i].astype(BF16)) for i in n]

    vu = [jnp.concatenate([v_s[i], u[i].astype(BF16)], axis=0) for i in n]
    for i, (b, d, p) in enumerate(chains):
        o_s = rh[i] + _dot(pkb[i], vu[i])
        o_refs[d][b, :, lanes_of(p)] = o_s[:L] + o_s[L:]
    for i, (b, d, p) in enumerate(chains):
        kb = jnp.concatenate([stk("khat", b, d, p), -stk("bhat", b, d, p)], axis=0).astype(BF16)
        tot_col = jnp.broadcast_to(ops[b, d]["tot"][:, lanes_of(p)], (LANES, LANES)).T
        h_scr[b, d, p] = jnp.exp(tot_col) * h[i] + _dot_tn(kb, vu[i])

    @pl.when(ci == nc - 1)
    def _():
        sfin_ref[...] = h_scr[...]


def _wkv_scan(r, v, kkn, logw, kd, bb, s0, chunk=SCAN_CHUNK):
    b, t, c = r.shape
    n_pairs = c // LANES
    assert t % chunk == 0 and 2 * RW_HEAD == LANES
    nc = t // chunk
    fwd = lambda ci: ci
    bwd = lambda ci: nc - 1 - ci
    shared = lambda cmap: pl.BlockSpec((b, chunk, c), lambda ci: (0, cmap(ci), 0))
    perdir = lambda d, cmap: pl.BlockSpec((None, b, chunk, c), lambda ci: (d, 0, cmap(ci), 0))
    state = pl.BlockSpec((b, 2, n_pairs, LANES, LANES), lambda ci: (0, 0, 0, 0, 0))
    kern = functools.partial(_wkv_scan_kernel, chunk=chunk, n_pairs=n_pairs)
    return pl.pallas_call(
        kern,
        grid=(nc,),
        in_specs=[shared(fwd)] * 3 + [perdir(0, fwd)] * 3 + [shared(bwd)] * 3 + [perdir(1, bwd)] * 3 + [state],
        out_specs=[shared(fwd), shared(bwd), state],
        out_shape=[jax.ShapeDtypeStruct((b, t, c), F32), jax.ShapeDtypeStruct((b, t, c), F32),
                   jax.ShapeDtypeStruct((b, 2, n_pairs, LANES, LANES), F32)],
        scratch_shapes=[pltpu.VMEM((b, 2, n_pairs, LANES, LANES), F32)],
        compiler_params=_cparams(("arbitrary",)),
        name="wkv_scan",
    )(r, v, kkn, logw, kd, bb, r, v, kkn, logw, kd, bb, s0)


def _rwkv_readout_kernel(of_ref, ob_ref, bonus_ref, g_ref, x_ref, gate_ref,
                         gng_ref, gnb_ref, seg_ref, w_ref, out_ref):
    seg = seg_ref[...]
    inv_n = 1.0 / RW_HEAD
    o = of_ref[...] + ob_ref[...]
    mean = _seg_sum(o, seg) * inv_n
    dlt = o - mean
    var = _seg_sum(dlt * dlt, seg) * inv_n
    y = dlt * lax.rsqrt(var + GN_EPS) * gng_ref[...] + gnb_ref[...]
    y = (y + bonus_ref[...]) * _silu(g_ref[...])
    out_ref[...] = x_ref[...] + gate_ref[...] * _dot(y.astype(BF16), w_ref[...])


def _rwkv_readout(o_f, o_b, bonus, g, x2d, gate, gn_g, gn_b, seg, w_out, rows_per_group, tm):
    r_rows, c = g.shape
    d = x2d.shape[1]
    tpg = rows_per_group // tm
    full = lambda shape: pl.BlockSpec(shape, lambda i: (0,) * len(shape))
    row = lambda n: pl.BlockSpec((tm, n), lambda i: (i, 0))
    return pl.pallas_call(
        _rwkv_readout_kernel,
        grid=(r_rows // tm,),
        in_specs=[row(c), row(c), row(c), row(c), row(d),
                  pl.BlockSpec((None, 1, d), lambda i: (i // tpg, 0, 0)),
                  full((1, c)), full((1, c)), full((2 * LANES, LANES)), full((c, d))],
        out_specs=row(d),
        out_shape=jax.ShapeDtypeStruct((r_rows, d), F32),
        compiler_params=_cparams(("arbitrary",)),
        name="rwkv_readout",
    )(o_f, o_b, bonus, g, x2d, gate, gn_g.reshape(1, c), gn_b.reshape(1, c), seg, w_out)


def _head_norm(x, gain, seg):
    ms = _seg_sum(x * x, seg) * (1.0 / DA_HEAD)
    return x * lax.rsqrt(ms + RMS_EPS) * gain


def _rope(x, cos, sin_signed):
    n = x.shape[-1]
    lane = lax.broadcasted_iota(jnp.int32, x.shape, 1)
    half0 = (lane & (2 * ROPE_FREQS - 1)) < ROPE_FREQS
    partner = jnp.where(half0, pltpu.roll(x, n - ROPE_FREQS, 1), pltpu.roll(x, ROPE_FREQS, 1))
    reps = n // cos.shape[-1]
    wide = lambda tab: jnp.concatenate([tab] * reps, axis=1)
    return x * wide(cos) + partner * wide(sin_signed)


def _attn_in_kernel(*refs, latent, q_scale):
    it = iter(refs)
    x_ref, mult_ref, shift_ref, w_ref = next(it), next(it), next(it), next(it)
    cos_ref = next(it) if latent else None
    sin_ref = next(it) if latent else None
    qn_ref = next(it) if latent else None
    kn_ref, seg_ref = next(it), next(it)
    qo_ref = next(it) if latent else None
    ko_ref, vo_ref = next(it), next(it)
    go_ref = next(it) if latent else None
    h = _normmod(x_ref[...], mult_ref[...], shift_ref[...])
    w = ko_ref.shape[-1]
    proj = _dot(h, w_ref[...])
    col = lambda j: proj[:, j * w:(j + 1) * w]
    seg = seg_ref[...]
    kcol = 1 if latent else 0
    k = _head_norm(col(kcol), kn_ref[...], seg)
    if latent:
        k = _rope(k, cos_ref[...], sin_ref[...])
    ko_ref[...] = k.astype(BF16)
    if latent:
        q = _rope(_head_norm(col(0), qn_ref[...], seg), cos_ref[...], sin_ref[...])
        qo_ref[...] = (q * q_scale).astype(BF16)
    vo_ref[...] = col(kcol + 1).T.astype(BF16)
    if latent:
        go_ref[...] = _silu(col(3)).astype(BF16)


def _attn_in(x2d, mult, shift, w_in, latent, cos, sin_signed, qn, kn, seg, w, seq_len, rows_per_group, tm):
    r_rows, dm = x2d.shape
    tps = seq_len // tm
    tpg = rows_per_group // tm
    full = lambda shape: pl.BlockSpec(shape, lambda i: (0,) * len(shape))
    group = pl.BlockSpec((None, 1, dm), lambda i: (i // tpg, 0, 0))
    tab = pl.BlockSpec((tm, LANES), lambda i: (i % tps, 0))
    tile_gain = lambda g: jnp.tile(g, w // g.shape[0]).reshape(1, w)
    in_specs = [pl.BlockSpec((tm, dm), lambda i: (i, 0)), group, group, full(w_in.shape)]
    args = [x2d, mult, shift, w_in]
    if latent:
        in_specs += [tab, tab, full((1, w))]
        args += [cos, sin_signed, tile_gain(qn)]
    in_specs += [full((1, w)), full((2 * LANES, LANES))]
    args += [tile_gain(kn), seg]
    rows_bf = (pl.BlockSpec((tm, w), lambda i: (i, 0)), jax.ShapeDtypeStruct((r_rows, w), BF16))
    v_t = (pl.BlockSpec((None, w, tm), lambda i: (i // tps, 0, i % tps)),
           jax.ShapeDtypeStruct((r_rows // seq_len, w, seq_len), BF16))
    outs = [rows_bf, rows_bf, v_t, rows_bf] if latent else [rows_bf, v_t]
    kern = functools.partial(_attn_in_kernel, latent=latent, q_scale=DA_HEAD ** -0.5)
    return pl.pallas_call(
        kern,
        grid=(r_rows // tm,),
        in_specs=in_specs,
        out_specs=[o[0] for o in outs],
        out_shape=[o[1] for o in outs],
        compiler_params=_cparams(("arbitrary",)),
        name="attn_in_lat" if latent else "attn_in_ctx",
    )(*args)


def _diff_attn_kernel(q_ref, kc_ref, kl_ref, vtc_ref, vtl_ref, lam_ref, sub_ref, o_ref, m_scr, l_scr, acc_scr,
                      *, lam_init, tk, bounded):
    q = q_ref[...]
    lane = lax.broadcasted_iota(jnp.int32, q.shape, 1)
    zero = jnp.zeros_like(q)
    qm = (jnp.where(lane < DA_HEAD, q, zero), jnp.where(lane < DA_HEAD, zero, q))
    m_scr[...] = jnp.full(m_scr.shape, -jnp.inf, F32)
    l_scr[...] = jnp.zeros(l_scr.shape, F32)
    acc_scr[...] = jnp.zeros(acc_scr.shape, F32)

    def accumulate(k, vt):
        for c in range(2):
            s = _dot_nt(k, qm[c])
            if bounded:
                p = jnp.exp(s)
                l_scr[c] += jnp.sum(p, axis=0, keepdims=True)
                acc_scr[c] += _dot(vt, p.astype(BF16))
            else:
                m_prev = m_scr[c]
                m_new = jnp.maximum(m_prev, jnp.max(s, axis=0, keepdims=True))
                alpha = jnp.exp(m_prev - m_new)
                p = jnp.exp(s - m_new)
                l_scr[c] = alpha * l_scr[c] + jnp.sum(p, axis=0, keepdims=True)
                acc_scr[c] = alpha * acc_scr[c] + _dot(vt, p.astype(BF16))
                m_scr[c] = m_new

    def body(j, carry):
        off = pl.multiple_of(j * tk, tk)
        accumulate(kl_ref[pl.ds(off, tk), :], vtl_ref[:, pl.ds(off, tk)])
        return carry

    accumulate(kc_ref[...], vtc_ref[...])
    lax.fori_loop(0, kl_ref.shape[0] // tk, body, 0, unroll=bounded)

    lv = lam_ref[...]
    lam = (jnp.exp(jnp.sum(lv[0:1] * lv[1:2], axis=-1, keepdims=True))
           - jnp.exp(jnp.sum(lv[2:3] * lv[3:4], axis=-1, keepdims=True)) + lam_init)
    o = acc_scr[0] / l_scr[0] - lam * (acc_scr[1] / l_scr[1])
    y = o * lax.rsqrt(jnp.mean(o * o, axis=0, keepdims=True) + SUBLN_EPS)
    o_ref[...] = (y * (sub_ref[...] * (1.0 - lam_init))).T.astype(BF16)


def _diff_attn(q, k_ctx, k_lat, vt_ctx, vt_lat, lam_vecs, subln, lam_init, tq, tk, bounded):
    b, t, w = q.shape
    lc = k_ctx.shape[1]
    nh = w // DA_VHEAD
    assert t % tq == 0 and t % tk == 0
    kern = functools.partial(_diff_attn_kernel, lam_init=lam_init, tk=tk, bounded=bounded)
    return pl.pallas_call(
        kern,
        grid=(b, nh, t // tq),
        in_specs=[pl.BlockSpec((None, tq, DA_VHEAD), lambda bi, h, i: (bi, i, h)),
                  pl.BlockSpec((None, lc, DA_VHEAD), lambda bi, h, i: (bi, 0, h)),
                  pl.BlockSpec((None, t, DA_VHEAD), lambda bi, h, i: (bi, 0, h)),
                  pl.BlockSpec((None, DA_VHEAD, lc), lambda bi, h, i: (bi, h, 0)),
                  pl.BlockSpec((None, DA_VHEAD, t), lambda bi, h, i: (bi, h, 0)),
                  pl.BlockSpec((4, DA_HEAD), lambda bi, h, i: (0, 0)),
                  pl.BlockSpec((DA_VHEAD, 1), lambda bi, h, i: (0, 0))],
        out_specs=pl.BlockSpec((None, tq, DA_VHEAD), lambda bi, h, i: (bi, i, h)),
        out_shape=jax.ShapeDtypeStruct((b, t, w), BF16),
        scratch_shapes=[pltpu.VMEM((2, 1, tq), F32), pltpu.VMEM((2, 1, tq), F32),
                        pltpu.VMEM((2, DA_VHEAD, tq), F32)],
        compiler_params=_cparams(("arbitrary", "arbitrary", "arbitrary")),
        name="diff_attn_bounded" if bounded else "diff_attn",
    )(q, k_ctx, k_lat, vt_ctx, vt_lat, lam_vecs, subln.reshape(DA_VHEAD, 1))


def _gated_out_kernel(y_ref, g_ref, x_ref, gate_ref, w_ref, out_ref):
    y = y_ref[...].astype(F32) * g_ref[...].astype(F32)
    out_ref[...] = x_ref[...] + gate_ref[...] * _dot(y.astype(BF16), w_ref[...])


def _gated_out(y, g, x2d, gate, w_out, rows_per_group, tm):
    r_rows, w = y.shape
    d = x2d.shape[1]
    tpg = rows_per_group // tm
    row = lambda n: pl.BlockSpec((tm, n), lambda i: (i, 0))
    return pl.pallas_call(
        _gated_out_kernel,
        grid=(r_rows // tm,),
        in_specs=[row(w), row(w), row(d),
                  pl.BlockSpec((None, 1, d), lambda i: (i // tpg, 0, 0)),
                  pl.BlockSpec((w, d), lambda i: (0, 0))],
        out_specs=row(d),
        out_shape=jax.ShapeDtypeStruct((r_rows, d), F32),
        compiler_params=_cparams(("arbitrary",)),
        name="gated_out",
    )(y, g, x2d, gate, w_out)


def _segment_matrix(width):
    i = (lax.broadcasted_iota(jnp.int32, (2 * LANES, LANES), 0) % LANES) // width
    j = lax.broadcasted_iota(jnp.int32, (2 * LANES, LANES), 1) // width
    return (i == j).astype(BF16)


def _rope_tables(rows):
    row_ids = jnp.repeat(jnp.arange(rows), GRID_W).astype(F32)
    col_ids = jnp.tile(jnp.arange(GRID_W), rows).astype(F32)
    inv_freq = ROPE_THETA ** (-jnp.arange(ROPE_FREQS, dtype=F32) / ROPE_FREQS)
    ang_r = row_ids[:, None] * inv_freq
    ang_c = col_ids[:, None] * inv_freq
    cr, sr, cc, sc = jnp.cos(ang_r), jnp.sin(ang_r), jnp.cos(ang_c), jnp.sin(ang_c)
    cos = jnp.concatenate([cr, cr, cc, cc], axis=-1)
    sin_signed = jnp.concatenate([-sr, sr, -sc, sc], axis=-1)
    return jnp.tile(cos, (1, 2)), jnp.tile(sin_signed, (1, 2))


def kernel(x, c, ctx, c_ctx, ada_w, ada_b, norm_g, rw_in, rw_mu, rw_w0, rw_w2, rw_a0, rw_a2, rw_kk, rw_ka, rw_rk, rw_gn_g, rw_gn_b, rw_out, da_in, da_qn, da_kn, da_lam, da_subln, da_out):
    bsz, t, d = x.shape
    lc = ctx.shape[1]
    depth = ada_w.shape[0]
    assert depth == 2 and bsz + 1 <= SUBLANES
    cw = rw_kk.shape[-1]
    x2 = x.reshape(bsz * t, d)
    xc2 = ctx.reshape(bsz * lc, d)

    rows = jnp.zeros((SUBLANES, d), F32).at[:bsz].set(c).at[bsz].set(c_ctx)
    mod = _ada_mod(rows, ada_w, ada_b)
    shift, scale, gate = mod[:, :, :d], mod[:, :, d:2 * d], mod[:, :, 2 * d:]
    mult = norm_g[:, None, :] * (1.0 + scale)
    lat = lambda a, i: a[i, :bsz].reshape(bsz, 1, d)
    con = lambda a, i: a[i, bsz:bsz + 1].reshape(1, 1, d)

    tm_l = 256
    tm_c = min(256, lc)

    w_in0 = rw_in[0].astype(BF16)
    seg64 = _segment_matrix(RW_HEAD)
    rwkv_in = functools.partial(_rwkv_in, w_in=w_in0, mu=rw_mu[0], w0=rw_w0[0], w2=rw_w2[0], a0=rw_a0[0],
                                a2=rw_a2[0], k_k=rw_kk[0], k_a=rw_ka[0], r_k=rw_rk[0].reshape(-1), seg=seg64)
    r_l, v_l, kk_l, g_l, bonus_l, lw_l, kd_l, bb_l = rwkv_in(x2, lat(mult, 0), lat(shift, 0), seq_len=t,
                                                             rows_per_group=t, tm=tm_l)
    r_c, v_c, kk_c, g_c, bonus_c, lw_c, kd_c, bb_c = rwkv_in(xc2, con(mult, 0), con(shift, 0), seq_len=lc,
                                                             rows_per_group=bsz * lc, tm=tm_c)

    s0 = jnp.zeros((bsz, 2, cw // LANES, LANES, LANES), F32)
    b3 = lambda a, n: a.reshape(bsz, n, cw)
    b4 = lambda a, n: a.reshape(2, bsz, n, cw)
    ocf, ocb, s_c = _wkv_scan(b3(r_c, lc), b3(v_c, lc), b3(kk_c, lc), b4(lw_c, lc), b4(kd_c, lc),
                              b4(bb_c, lc), s0)
    olf, olb, _ = _wkv_scan(b3(r_l, t), b3(v_l, t), b3(kk_l, t), b4(lw_l, t), b4(kd_l, t), b4(bb_l, t), s_c)

    w_out0 = rw_out[0].astype(BF16)
    readout = functools.partial(_rwkv_readout, gn_g=rw_gn_g[0], gn_b=rw_gn_b[0], seg=seg64, w_out=w_out0)
    x1 = readout(olf.reshape(bsz * t, cw), olb.reshape(bsz * t, cw), bonus_l, g_l, x2,
                 lat(gate, 0), rows_per_group=t, tm=tm_l)
    xc1 = readout(ocf.reshape(bsz * lc, cw), ocb.reshape(bsz * lc, cw), bonus_c, g_c, xc2,
                  con(gate, 0), rows_per_group=bsz * lc, tm=tm_c)

    qw = DA_HEADS * 2 * DA_HEAD
    w_in1 = da_in[0].astype(BF16)
    seg_da = _segment_matrix(DA_HEAD)
    cos, sin_signed = _rope_tables(t // GRID_W)
    q_b, k_b, vt_b, g2_l = _attn_in(x1, lat(mult, 1), lat(shift, 1), w_in1, True, cos, sin_signed, da_qn[0],
                                    da_kn[0], seg_da, qw, seq_len=t, rows_per_group=t, tm=tm_l)
    kc_b, vtc_b = _attn_in(xc1, con(mult, 1), con(shift, 1), w_in1[:, qw:3 * qw], False, None, None, None,
                           da_kn[0], seg_da, qw, seq_len=lc, rows_per_group=bsz * lc, tm=tm_c)
    lam_init = 0.8 - 0.6 * math.exp(-0.3 * 1)
    attn = functools.partial(_diff_attn, lam_vecs=da_lam[0], subln=da_subln[0], lam_init=lam_init,
                             tq=min(1024, t), tk=min(1024, t))
    score_bound = DA_HEAD ** 0.5 * jnp.max(jnp.abs(da_qn[0])) * jnp.max(jnp.abs(da_kn[0]))
    y_att = lax.cond(score_bound < MAX_UNSHIFTED_SCORE,
                     functools.partial(attn, bounded=True), functools.partial(attn, bounded=False),
                     q_b.reshape(bsz, t, qw), kc_b.reshape(bsz, lc, qw), k_b.reshape(bsz, t, qw), vtc_b, vt_b)
    out = _gated_out(y_att.reshape(bsz * t, qw), g2_l, x1, lat(gate, 1), da_out[0].astype(BF16),
                     rows_per_group=t, tm=tm_l)
    return out.reshape(bsz, t, d)
```

```python
import functools
import math

import jax
import jax.numpy as jnp
from jax import lax
from jax.experimental import pallas as pl
from jax.experimental.pallas import tpu as pltpu

F32 = jnp.float32
BF16 = jnp.bfloat16

GRID_W = 64
RMS_EPS = 1e-6
RW_HEAD = 64
W_LORA = 64
A_LORA = 64
GN_EPS = 64e-5
DA_HEADS = 8
DA_HEAD = 64
DA_VHEAD = 2 * DA_HEAD
SUBLN_EPS = 1e-5
ROPE_THETA = 10000.0
ROPE_FREQS = DA_HEAD // 4

LANES = 128
SUBLANES = 8
SCAN_CHUNK = 64
VMEM_LIMIT = 56 * 1024 * 1024
MAX_UNSHIFTED_SCORE = 40.0
PROJ_ROW_TILE = 256
STREAM_ROW_TILE = 512
ATTN_Q_TILE = 1024
ATTN_KEY_CHUNK = 1024


def _cparams(sem):
    return pltpu.CompilerParams(dimension_semantics=sem, vmem_limit_bytes=VMEM_LIMIT)


def _dot(a, b):
    return jnp.dot(a, b, preferred_element_type=F32)


def _dot_nt(a, b):
    return lax.dot_general(a, b, (((1,), (1,)), ((), ())), preferred_element_type=F32)


def _dot_tn(a, b):
    return lax.dot_general(a, b, (((0,), (0,)), ((), ())), preferred_element_type=F32)


def _split2(x):
    hi = x.astype(BF16)
    lo = (x - hi.astype(F32)).astype(BF16)
    return hi, lo


def _seg_sum(x, seg2):
    out = []
    for j in range(x.shape[-1] // LANES):
        hi, lo = _split2(x[:, j * LANES:(j + 1) * LANES])
        out.append(_dot(jnp.concatenate([hi, lo], axis=1), seg2))
    return jnp.concatenate(out, axis=1)


def _sigmoid(x):
    return 1.0 / (1.0 + jnp.exp(-x))


def _silu(x):
    return x * _sigmoid(x)


def _ada_kernel(c_ref, w_ref, b_ref, o_ref):
    s = _silu(c_ref[...])
    o_ref[...] = jnp.dot(s, w_ref[...], preferred_element_type=F32,
                         precision=lax.Precision.HIGHEST) + b_ref[...]


def _ada_mod(rows, ada_w, ada_b):
    depth, d, n = ada_w.shape
    tn = 768
    return pl.pallas_call(
        _ada_kernel,
        grid=(depth, n // tn),
        in_specs=[pl.BlockSpec((SUBLANES, d), lambda i, j: (0, 0)),
                  pl.BlockSpec((None, d, tn), lambda i, j: (i, 0, j)),
                  pl.BlockSpec((None, 1, tn), lambda i, j: (i, 0, j))],
        out_specs=pl.BlockSpec((None, SUBLANES, tn), lambda i, j: (i, 0, j)),
        out_shape=jax.ShapeDtypeStruct((depth, SUBLANES, n), F32),
        compiler_params=_cparams(("arbitrary", "arbitrary")),
        name="ada_mod",
    )(rows, ada_w, ada_b.reshape(depth, 1, n))


def _normmod(x, mult, shift):
    y = x * lax.rsqrt(jnp.mean(x * x, axis=-1, keepdims=True) + RMS_EPS)
    return (y * mult + shift).astype(BF16)


def _token_shift(u, prev8, next8, mu, first, last):
    tm = u.shape[0]
    row8 = lax.broadcasted_iota(jnp.int32, (SUBLANES, u.shape[1]), 0)
    prow = jnp.where(first, 0.0, prev8[SUBLANES - 1:SUBLANES, :])
    nrow = jnp.where(last, 0.0, next8[0:1, :])
    nbr = pltpu.roll(u, 1, 0) + pltpu.roll(u, tm - 1, 0)
    top = nbr[:SUBLANES] + jnp.where(row8 == 0, prow - u[tm - 1:tm], 0.0)
    bot = nbr[tm - SUBLANES:] + jnp.where(row8 == SUBLANES - 1, nrow - u[0:1], 0.0)
    nbr = jnp.concatenate([top, nbr[SUBLANES:tm - SUBLANES], bot], axis=0)
    return (1.0 - mu) * u + (0.5 * mu) * nbr


def _rwkv_in_kernel(x_ref, xp_ref, xn_ref, mult_ref, shift_ref, w_ref,
                    mua_ref, mub_ref, w2_ref, w0_ref, a2_ref, a0_ref, kk_ref, ka_ref, rk_ref, seg_ref,
                    r_ref, v_ref, kkn_ref, g_ref, bonus_ref, lw_ref, kd_ref, bb_ref, *, tiles_per_seq, c):
    i = pl.program_id(0)
    first = (i % tiles_per_seq) == 0
    last = (i % tiles_per_seq) == tiles_per_seq - 1
    tm = x_ref.shape[0]
    lo, hi = SUBLANES, SUBLANES + tm
    x_ext = jnp.concatenate([xp_ref[...], x_ref[...], xn_ref[...]], axis=0)
    h = _normmod(x_ext, mult_ref[...], shift_ref[...])

    def project(c0, c1):
        return _dot(h, w_ref[:, c0:c1])

    def shifted(p, mu):
        return _token_shift(p[lo:hi], p[:lo], p[hi:], mu, first, last)

    p_lora = project(4 * c, w_ref.shape[1])
    p_k = project(c, 2 * c)
    mb = shifted(p_lora, mub_ref[...])
    nl = 2 * W_LORA
    w_raw = w0_ref[...] + _dot(jnp.tanh(mb[:, :nl]).astype(BF16), w2_ref[...])
    a = _sigmoid(a0_ref[...] + _dot(mb[:, nl:].astype(BF16), a2_ref[...]))
    logw = -math.exp(-0.5) / (1.0 + jnp.exp(-w_raw))
    for d in range(2):
        lw_ref[d] = logw[:, d * c:(d + 1) * c]

    p_r = project(0, c)
    k = shifted(p_k, mua_ref[:, c:2 * c])
    kk = k * kk_ref[...]
    kkn = kk * lax.rsqrt(_seg_sum(kk * kk, seg_ref[...]) + 1e-12)
    kkn_ref[...] = kkn
    kka = k * ka_ref[...]
    k_rest = k - kka
    kd = [k_rest + kka * a[:, d * c:(d + 1) * c] for d in range(2)]
    for d in range(2):
        kd_ref[d] = kd[d]
        bb_ref[d] = kkn * a[:, d * c:(d + 1) * c]

    p_v = project(2 * c, 3 * c)
    r = shifted(p_r, mua_ref[:, :c])
    r_ref[...] = r
    p_g = project(3 * c, 4 * c)
    v = shifted(p_v, mua_ref[:, 2 * c:3 * c])
    v_ref[...] = v
    bonus_ref[...] = _seg_sum(r * (kd[0] + kd[1]) * rk_ref[...], seg_ref[...]) * v
    g_ref[...] = p_g[lo:hi]


def _rwkv_in(x2d, mult, shift, w_in, seq_len, rows_per_group, mu, w0, w2, a0, a2, k_k, k_a, r_k, seg, tm):
    r_rows, dm = x2d.shape
    ncols = w_in.shape[1]
    c = k_k.shape[-1]
    nb = ncols - 4 * c
    assert nb == 2 * (W_LORA + A_LORA) and seq_len % tm == 0 and rows_per_group % tm == 0
    tps = seq_len // tm
    tpg = rows_per_group // tm
    t8 = tm // SUBLANES
    nblk8 = r_rows // SUBLANES
    mua = mu[:3 * c].reshape(1, 3 * c)
    mub = mu[3 * c:].reshape(1, nb)
    zw = jnp.zeros((W_LORA, c), F32)
    w2cat = jnp.concatenate([jnp.concatenate([w2[0], zw], 1), jnp.concatenate([zw, w2[1]], 1)], 0).astype(BF16)
    a2cat = jnp.concatenate([jnp.concatenate([a2[0], zw], 1), jnp.concatenate([zw, a2[1]], 1)], 0).astype(BF16)
    w0cat = w0.reshape(1, 2 * c)
    a0cat = a0.reshape(1, 2 * c)
    full = lambda shape: pl.BlockSpec(shape, lambda i: (0,) * len(shape))
    prev_idx = lambda i: jnp.maximum(i * t8 - 1, 0)
    next_idx = lambda i: jnp.minimum((i + 1) * t8, nblk8 - 1)
    row_out = pl.BlockSpec((tm, c), lambda i: (i, 0))
    dir_out = pl.BlockSpec((2, tm, c), lambda i: (0, i, 0))
    group = pl.BlockSpec((None, 1, dm), lambda i: (i // tpg, 0, 0))
    kern = functools.partial(_rwkv_in_kernel, tiles_per_seq=tps, c=c)
    return pl.pallas_call(
        kern,
        grid=(r_rows // tm,),
        in_specs=[pl.BlockSpec((tm, dm), lambda i: (i, 0)),
                  pl.BlockSpec((SUBLANES, dm), lambda i: (prev_idx(i), 0)),
                  pl.BlockSpec((SUBLANES, dm), lambda i: (next_idx(i), 0)),
                  group, group, full((dm, ncols)),
                  full((1, 3 * c)), full((1, nb)),
                  full((2 * W_LORA, 2 * c)), full((1, 2 * c)),
                  full((2 * A_LORA, 2 * c)), full((1, 2 * c)),
                  full((1, c)), full((1, c)), full((1, c)), full((2 * LANES, LANES))],
        out_specs=[row_out] * 5 + [dir_out] * 3,
        out_shape=[jax.ShapeDtypeStruct((r_rows, c), F32)] * 5
                  + [jax.ShapeDtypeStruct((2, r_rows, c), F32)] * 3,
        compiler_params=_cparams(("arbitrary",)),
        name="rwkv_in",
    )(x2d, x2d, x2d, mult, shift, w_in, mua, mub, w2cat, w0cat, a2cat, a0cat,
      k_k.reshape(1, c), k_a.reshape(1, c), r_k.reshape(1, c), seg)


def _head_pair_stack(xp, lane_a):
    return jnp.concatenate([jnp.where(lane_a, xp, 0.0), jnp.where(lane_a, 0.0, xp)], axis=0)


def _wkv_chunk_operands(r, v, kk, lw, kd, bb, reverse, L):
    ti = lax.broadcasted_iota(jnp.int32, (L, L), 0)
    si = lax.broadcasted_iota(jnp.int32, (L, L), 1)
    tri = ((ti <= si) if reverse else (ti >= si)).astype(BF16)
    hi = lw.astype(BF16)
    r1 = lw - hi.astype(F32)
    mid = r1.astype(BF16)
    lo = (r1 - mid.astype(F32)).astype(BF16)
    cum = _dot(tri, hi) + _dot(tri, mid) + _dot(tri, lo)
    tot = jnp.sum(lw, axis=0, keepdims=True)
    ginv = jnp.exp(-cum)
    gend = jnp.exp(tot - cum)
    return dict(kt=kk * jnp.exp(cum - lw), rt=r * jnp.exp(cum), bt=bb * ginv,
                kdt=kd * ginv, khat=kd * gend, bhat=bb * gend, v=v, tot=tot)


def _wkv_scan_kernel(rf_ref, vf_ref, kkf_ref, lwf_ref, kdf_ref, bbf_ref,
                     rb_ref, vb_ref, kkb_ref, lwb_ref, kdb_ref, bbb_ref, s0_ref,
                     of_ref, ob_ref, sfin_ref, h_scr, *, chunk, n_pairs):
    L = chunk
    ci = pl.program_id(0)
    nc = pl.num_programs(0)
    nb = rf_ref.shape[0]

    @pl.when(ci == 0)
    def _():
        h_scr[...] = s0_ref[...]

    in_refs = ((rf_ref, vf_ref, kkf_ref, lwf_ref, kdf_ref, bbf_ref),
               (rb_ref, vb_ref, kkb_ref, lwb_ref, kdb_ref, bbb_ref))
    ops = {(b, d): _wkv_chunk_operands(*[ref[b] for ref in in_refs[d]], d == 1, L)
           for b in range(nb) for d in range(2)}
    o_refs = (of_ref, ob_ref)

    p2 = 2 * L
    lane_a = lax.broadcasted_iota(jnp.int32, (L, LANES), 1) < RW_HEAD
    dts = ((lax.broadcasted_iota(jnp.int32, (p2, p2), 0) & (L - 1))
           - (lax.broadcasted_iota(jnp.int32, (p2, p2), 1) & (L - 1)))
    strict = (dts > 0, dts < 0)
    incl = (dts >= 0, dts <= 0)
    eye = (lax.broadcasted_iota(jnp.int32, (p2, p2), 0)
           == lax.broadcasted_iota(jnp.int32, (p2, p2), 1)).astype(F32)

    chains = [(b, d, p) for p in range(n_pairs) for b in range(nb) for d in range(2)]
    lanes_of = lambda p: slice(p * LANES, (p + 1) * LANES)
    stk = lambda name, b, d, p: _head_pair_stack(ops[b, d][name][:, lanes_of(p)], lane_a)
    kt_s = [stk("kt", *ch).astype(BF16) for ch in chains]
    rt_s = [stk("rt", *ch).astype(BF16) for ch in chains]
    y_s = [jnp.concatenate([stk("bt", *ch).astype(BF16), stk("kdt", *ch).astype(BF16)], axis=0)
           for ch in chains]
    v_s = [stk("v", *ch).astype(BF16) for ch in chains]
    h = [h_scr[ch] for ch in chains]
    h_b = [x.astype(BF16) for x in h]
    n = range(len(chains))
    sc = [_dot_nt(jnp.concatenate([kt_s[i], rt_s[i]], axis=0), y_s[i]) for i in n]
    rh = [_dot(rt_s[i], h_b[i]) for i in n]
    msk = lambda m, x: jnp.where(m, x, 0.0)
    pw = [-msk(strict[d], sc[i][:p2, :p2]) for i, (b, d, p) in enumerate(chains)]
    a_k = [msk(strict[d], sc[i][:p2, p2:]).astype(BF16) for i, (b, d, p) in enumerate(chains)]
    pkb = [jnp.concatenate([msk(incl[d], sc[i][p2:, p2:]), -msk(incl[d], sc[i][p2:, :p2])],
                           axis=1).astype(BF16) for i, (b, d, p) in enumerate(chains)]
    u = [_dot(jnp.concatenate([kt_s[i], a_k[i]], axis=1), jnp.concatenate([h_b[i], v_s[i]], axis=0))
         for i in n]

    pwb = [x.astype(BF16) for x in pw]
    tinv = [eye + pw[i] for i in n]
    pw = [_dot(pwb[i], pwb[i]) for i in n]
    for j in range(1, int(math.log2(L))):
        pwb = [x.astype(BF16) for x in pw]
        if j < int(math.log2(L)) - 1:
            w = [_dot(jnp.concatenate([tinv[i].astype(BF16), pwb[i]], axis=0), pwb[i]) for i in n]
            tinv = [tinv[i] + w[i][:p2] for i in n]
            pw = [w[i][p2:] for i in n]
        else:
            tinv = [tinv[i] + _dot(tinv[i].astype(BF16), pwb[i]) for i in n]
    u = [_dot(tinv[i].astype(BF16), u[i].astype(BF16)) for i in n]

    vu = [jnp.concatenate([v_s[i], u[i].astype(BF16)], axis=0) for i in n]
    for i, (b, d, p) in enumerate(chains):
        o_s = rh[i] + _dot(pkb[i], vu[i])
        o_refs[d][b, :, lanes_of(p)] = o_s[:L] + o_s[L:]
    for i, (b, d, p) in enumerate(chains):
        kb = jnp.concatenate([stk("khat", b, d, p), -stk("bhat", b, d, p)], axis=0).astype(BF16)
        tot_col = jnp.broadcast_to(ops[b, d]["tot"][:, lanes_of(p)], (LANES, LANES)).T
        h_scr[b, d, p] = jnp.exp(tot_col) * h[i] + _dot_tn(kb, vu[i])

    @pl.when(ci == nc - 1)
    def _():
        sfin_ref[...] = h_scr[...]


def _wkv_scan(r, v, kkn, logw, kd, bb, s0, chunk=SCAN_CHUNK):
    b, t, c = r.shape
    n_pairs = c // LANES
    assert t % chunk == 0 and 2 * RW_HEAD == LANES
    nc = t // chunk
    fwd = lambda ci: ci
    bwd = lambda ci: nc - 1 - ci
    shared = lambda cmap: pl.BlockSpec((b, chunk, c), lambda ci: (0, cmap(ci), 0))
    perdir = lambda d, cmap: pl.BlockSpec((None, b, chunk, c), lambda ci: (d, 0, cmap(ci), 0))
    state = pl.BlockSpec((b, 2, n_pairs, LANES, LANES), lambda ci: (0, 0, 0, 0, 0))
    kern = functools.partial(_wkv_scan_kernel, chunk=chunk, n_pairs=n_pairs)
    return pl.pallas_call(
        kern,
        grid=(nc,),
        in_specs=[shared(fwd)] * 3 + [perdir(0, fwd)] * 3 + [shared(bwd)] * 3 + [perdir(1, bwd)] * 3 + [state],
        out_specs=[shared(fwd), shared(bwd), state],
        out_shape=[jax.ShapeDtypeStruct((b, t, c), F32), jax.ShapeDtypeStruct((b, t, c), F32),
                   jax.ShapeDtypeStruct((b, 2, n_pairs, LANES, LANES), F32)],
        scratch_shapes=[pltpu.VMEM((b, 2, n_pairs, LANES, LANES), F32)],
        compiler_params=_cparams(("arbitrary",)),
        name="wkv_scan",
    )(r, v, kkn, logw, kd, bb, r, v, kkn, logw, kd, bb, s0)


def _rwkv_readout_kernel(of_ref, ob_ref, bonus_ref, g_ref, x_ref, gate_ref,
                         gng_ref, gnb_ref, seg_ref, w_ref, out_ref):
    seg = seg_ref[...]
    inv_n = 1.0 / RW_HEAD
    o = of_ref[...] + ob_ref[...]
    mean = _seg_sum(o, seg) * inv_n
    dlt = o - mean
    var = _seg_sum(dlt * dlt, seg) * inv_n
    y = dlt * lax.rsqrt(var + GN_EPS) * gng_ref[...] + gnb_ref[...]
    y = (y + bonus_ref[...]) * _silu(g_ref[...])
    out_ref[...] = x_ref[...] + gate_ref[...] * _dot(y.astype(BF16), w_ref[...])


def _rwkv_readout(o_f, o_b, bonus, g, x2d, gate, gn_g, gn_b, seg, w_out, rows_per_group, tm):
    r_rows, c = g.shape
    d = x2d.shape[1]
    tpg = rows_per_group // tm
    full = lambda shape: pl.BlockSpec(shape, lambda i: (0,) * len(shape))
    row = lambda n: pl.BlockSpec((tm, n), lambda i: (i, 0))
    return pl.pallas_call(
        _rwkv_readout_kernel,
        grid=(r_rows // tm,),
        in_specs=[row(c), row(c), row(c), row(c), row(d),
                  pl.BlockSpec((None, 1, d), lambda i: (i // tpg, 0, 0)),
                  full((1, c)), full((1, c)), full((2 * LANES, LANES)), full((c, d))],
        out_specs=row(d),
        out_shape=jax.ShapeDtypeStruct((r_rows, d), F32),
        compiler_params=_cparams(("arbitrary",)),
        name="rwkv_readout",
    )(o_f, o_b, bonus, g, x2d, gate, gn_g.reshape(1, c), gn_b.reshape(1, c), seg, w_out)


def _head_norm(x, gain, seg):
    ms = _seg_sum(x * x, seg) * (1.0 / DA_HEAD)
    return x * lax.rsqrt(ms + RMS_EPS) * gain


def _rope(x, cos, sin_signed):
    n = x.shape[-1]
    lane = lax.broadcasted_iota(jnp.int32, x.shape, 1)
    half0 = (lane & (2 * ROPE_FREQS - 1)) < ROPE_FREQS
    partner = jnp.where(half0, pltpu.roll(x, n - ROPE_FREQS, 1), pltpu.roll(x, ROPE_FREQS, 1))
    reps = n // cos.shape[-1]
    wide = lambda tab: jnp.concatenate([tab] * reps, axis=1)
    return x * wide(cos) + partner * wide(sin_signed)


def _attn_in_kernel(*refs, latent, q_scale):
    it = iter(refs)
    x_ref, mult_ref, shift_ref, w_ref = next(it), next(it), next(it), next(it)
    cos_ref = next(it) if latent else None
    sin_ref = next(it) if latent else None
    qn_ref = next(it) if latent else None
    kn_ref, seg_ref = next(it), next(it)
    qo_ref = next(it) if latent else None
    ko_ref, vo_ref = next(it), next(it)
    go_ref = next(it) if latent else None
    h = _normmod(x_ref[...], mult_ref[...], shift_ref[...])
    w = ko_ref.shape[-1]
    cols = lambda j0, j1: _dot(h, w_ref[:, j0 * w:j1 * w])
    seg = seg_ref[...]
    if latent:
        p_k = cols(1, 2)
        p_q = cols(0, 1)
        k = _rope(_head_norm(p_k, kn_ref[...], seg), cos_ref[...], sin_ref[...])
        ko_ref[...] = k.astype(BF16)
        p_vg = cols(2, 4)
        q = _rope(_head_norm(p_q, qn_ref[...], seg), cos_ref[...], sin_ref[...])
        qo_ref[...] = (q * q_scale).astype(BF16)
        vo_ref[...] = p_vg[:, :w].T.astype(BF16)
        go_ref[...] = _silu(p_vg[:, w:]).astype(BF16)
    else:
        p_kv = cols(0, 2)
        ko_ref[...] = _head_norm(p_kv[:, :w], kn_ref[...], seg).astype(BF16)
        vo_ref[...] = p_kv[:, w:].T.astype(BF16)


def _attn_in(x2d, mult, shift, w_in, latent, cos, sin_signed, qn, kn, seg, w, seq_len, rows_per_group, tm):
    r_rows, dm = x2d.shape
    tps = seq_len // tm
    tpg = rows_per_group // tm
    full = lambda shape: pl.BlockSpec(shape, lambda i: (0,) * len(shape))
    group = pl.BlockSpec((None, 1, dm), lambda i: (i // tpg, 0, 0))
    tab = pl.BlockSpec((tm, LANES), lambda i: (i % tps, 0))
    tile_gain = lambda g: jnp.tile(g, w // g.shape[0]).reshape(1, w)
    in_specs = [pl.BlockSpec((tm, dm), lambda i: (i, 0)), group, group, full(w_in.shape)]
    args = [x2d, mult, shift, w_in]
    if latent:
        in_specs += [tab, tab, full((1, w))]
        args += [cos, sin_signed, tile_gain(qn)]
    in_specs += [full((1, w)), full((2 * LANES, LANES))]
    args += [tile_gain(kn), seg]
    rows_bf = (pl.BlockSpec((tm, w), lambda i: (i, 0)), jax.ShapeDtypeStruct((r_rows, w), BF16))
    v_t = (pl.BlockSpec((None, w, tm), lambda i: (i // tps, 0, i % tps)),
           jax.ShapeDtypeStruct((r_rows // seq_len, w, seq_len), BF16))
    outs = [rows_bf, rows_bf, v_t, rows_bf] if latent else [rows_bf, v_t]
    kern = functools.partial(_attn_in_kernel, latent=latent, q_scale=DA_HEAD ** -0.5)
    return pl.pallas_call(
        kern,
        grid=(r_rows // tm,),
        in_specs=in_specs,
        out_specs=[o[0] for o in outs],
        out_shape=[o[1] for o in outs],
        compiler_params=_cparams(("arbitrary",)),
        name="attn_in_lat" if latent else "attn_in_ctx",
    )(*args)


def _diff_attn_kernel(q_ref, kc_ref, kl_ref, vtc_ref, vtl_ref, lam_ref, sub_ref, o_ref, m_scr, l_scr, acc_scr,
                      *, lam_init, tk, bounded):
    q = q_ref[...]
    lane = lax.broadcasted_iota(jnp.int32, q.shape, 1)
    zero = jnp.zeros_like(q)
    qm = (jnp.where(lane < DA_HEAD, q, zero), jnp.where(lane < DA_HEAD, zero, q))
    m_scr[...] = jnp.full(m_scr.shape, -jnp.inf, F32)
    l_scr[...] = jnp.zeros(l_scr.shape, F32)
    acc_scr[...] = jnp.zeros(acc_scr.shape, F32)

    def accumulate(k, vt):
        for c in range(2):
            s = _dot_nt(k, qm[c])
            if bounded:
                p = jnp.exp(s)
                l_scr[c] += jnp.sum(p, axis=0, keepdims=True)
                acc_scr[c] += _dot(vt, p.astype(BF16))
            else:
                m_prev = m_scr[c]
                m_new = jnp.maximum(m_prev, jnp.max(s, axis=0, keepdims=True))
                alpha = jnp.exp(m_prev - m_new)
                p = jnp.exp(s - m_new)
                l_scr[c] = alpha * l_scr[c] + jnp.sum(p, axis=0, keepdims=True)
                acc_scr[c] = alpha * acc_scr[c] + _dot(vt, p.astype(BF16))
                m_scr[c] = m_new

    def body(j, carry):
        off = pl.multiple_of(j * tk, tk)
        accumulate(kl_ref[pl.ds(off, tk), :], vtl_ref[:, pl.ds(off, tk)])
        return carry

    accumulate(kc_ref[...], vtc_ref[...])
    lax.fori_loop(0, kl_ref.shape[0] // tk, body, 0, unroll=bounded)

    lv = lam_ref[...]
    lam = (jnp.exp(jnp.sum(lv[0:1] * lv[1:2], axis=-1, keepdims=True))
           - jnp.exp(jnp.sum(lv[2:3] * lv[3:4], axis=-1, keepdims=True)) + lam_init)
    o = acc_scr[0] / l_scr[0] - lam * (acc_scr[1] / l_scr[1])
    y = o * lax.rsqrt(jnp.mean(o * o, axis=0, keepdims=True) + SUBLN_EPS)
    o_ref[...] = (y * (sub_ref[...] * (1.0 - lam_init))).T.astype(BF16)


def _diff_attn(q, k_ctx, k_lat, vt_ctx, vt_lat, lam_vecs, subln, lam_init, tq, tk, bounded):
    b, t, w = q.shape
    lc = k_ctx.shape[1]
    nh = w // DA_VHEAD
    assert t % tq == 0 and t % tk == 0
    kern = functools.partial(_diff_attn_kernel, lam_init=lam_init, tk=tk, bounded=bounded)
    return pl.pallas_call(
        kern,
        grid=(b, nh, t // tq),
        in_specs=[pl.BlockSpec((None, tq, DA_VHEAD), lambda bi, h, i: (bi, i, h)),
                  pl.BlockSpec((None, lc, DA_VHEAD), lambda bi, h, i: (bi, 0, h)),
                  pl.BlockSpec((None, t, DA_VHEAD), lambda bi, h, i: (bi, 0, h)),
                  pl.BlockSpec((None, DA_VHEAD, lc), lambda bi, h, i: (bi, h, 0)),
                  pl.BlockSpec((None, DA_VHEAD, t), lambda bi, h, i: (bi, h, 0)),
                  pl.BlockSpec((4, DA_HEAD), lambda bi, h, i: (0, 0)),
                  pl.BlockSpec((DA_VHEAD, 1), lambda bi, h, i: (0, 0))],
        out_specs=pl.BlockSpec((None, tq, DA_VHEAD), lambda bi, h, i: (bi, i, h)),
        out_shape=jax.ShapeDtypeStruct((b, t, w), BF16),
        scratch_shapes=[pltpu.VMEM((2, 1, tq), F32), pltpu.VMEM((2, 1, tq), F32),
                        pltpu.VMEM((2, DA_VHEAD, tq), F32)],
        compiler_params=_cparams(("arbitrary", "arbitrary", "arbitrary")),
        name="diff_attn_bounded" if bounded else "diff_attn",
    )(q, k_ctx, k_lat, vt_ctx, vt_lat, lam_vecs, subln.reshape(DA_VHEAD, 1))


def _gated_out_kernel(y_ref, g_ref, x_ref, gate_ref, w_ref, out_ref):
    y = y_ref[...].astype(F32) * g_ref[...].astype(F32)
    out_ref[...] = x_ref[...] + gate_ref[...] * _dot(y.astype(BF16), w_ref[...])


def _gated_out(y, g, x2d, gate, w_out, rows_per_group, tm):
    r_rows, w = y.shape
    d = x2d.shape[1]
    tpg = rows_per_group // tm
    row = lambda n: pl.BlockSpec((tm, n), lambda i: (i, 0))
    return pl.pallas_call(
        _gated_out_kernel,
        grid=(r_rows // tm,),
        in_specs=[row(w), row(w), row(d),
                  pl.BlockSpec((None, 1, d), lambda i: (i // tpg, 0, 0)),
                  pl.BlockSpec((w, d), lambda i: (0, 0))],
        out_specs=row(d),
        out_shape=jax.ShapeDtypeStruct((r_rows, d), F32),
        compiler_params=_cparams(("arbitrary",)),
        name="gated_out",
    )(y, g, x2d, gate, w_out)


def _segment_matrix(width):
    i = (lax.broadcasted_iota(jnp.int32, (2 * LANES, LANES), 0) % LANES) // width
    j = lax.broadcasted_iota(jnp.int32, (2 * LANES, LANES), 1) // width
    return (i == j).astype(BF16)


def _rope_tables(rows):
    row_ids = jnp.repeat(jnp.arange(rows), GRID_W).astype(F32)
    col_ids = jnp.tile(jnp.arange(GRID_W), rows).astype(F32)
    inv_freq = ROPE_THETA ** (-jnp.arange(ROPE_FREQS, dtype=F32) / ROPE_FREQS)
    ang_r = row_ids[:, None] * inv_freq
    ang_c = col_ids[:, None] * inv_freq
    cr, sr, cc, sc = jnp.cos(ang_r), jnp.sin(ang_r), jnp.cos(ang_c), jnp.sin(ang_c)
    cos = jnp.concatenate([cr, cr, cc, cc], axis=-1)
    sin_signed = jnp.concatenate([-sr, sr, -sc, sc], axis=-1)
    return jnp.tile(cos, (1, 2)), jnp.tile(sin_signed, (1, 2))


def kernel(x, c, ctx, c_ctx, ada_w, ada_b, norm_g, rw_in, rw_mu, rw_w0, rw_w2, rw_a0, rw_a2, rw_kk, rw_ka, rw_rk, rw_gn_g, rw_gn_b, rw_out, da_in, da_qn, da_kn, da_lam, da_subln, da_out):
    bsz, t, d = x.shape
    lc = ctx.shape[1]
    depth = ada_w.shape[0]
    assert depth == 2 and bsz + 1 <= SUBLANES
    cw = rw_kk.shape[-1]
    x2 = x.reshape(bsz * t, d)
    xc2 = ctx.reshape(bsz * lc, d)

    rows = jnp.zeros((SUBLANES, d), F32).at[:bsz].set(c).at[bsz].set(c_ctx)
    mod = _ada_mod(rows, ada_w, ada_b)
    shift, scale, gate = mod[:, :, :d], mod[:, :, d:2 * d], mod[:, :, 2 * d:]
    mult = norm_g[:, None, :] * (1.0 + scale)
    lat = lambda a, i: a[i, :bsz].reshape(bsz, 1, d)
    con = lambda a, i: a[i, bsz:bsz + 1].reshape(1, 1, d)

    tm_l = min(PROJ_ROW_TILE, t)
    tm_c = min(PROJ_ROW_TILE, lc)
    tm_stream = min(STREAM_ROW_TILE, t)

    w_in0 = rw_in[0].astype(BF16)
    seg64 = _segment_matrix(RW_HEAD)
    rwkv_in = functools.partial(_rwkv_in, w_in=w_in0, mu=rw_mu[0], w0=rw_w0[0], w2=rw_w2[0], a0=rw_a0[0],
                                a2=rw_a2[0], k_k=rw_kk[0], k_a=rw_ka[0], r_k=rw_rk[0].reshape(-1), seg=seg64)
    r_l, v_l, kk_l, g_l, bonus_l, lw_l, kd_l, bb_l = rwkv_in(x2, lat(mult, 0), lat(shift, 0), seq_len=t,
                                                             rows_per_group=t, tm=tm_l)
    r_c, v_c, kk_c, g_c, bonus_c, lw_c, kd_c, bb_c = rwkv_in(xc2, con(mult, 0), con(shift, 0), seq_len=lc,
                                                             rows_per_group=bsz * lc, tm=tm_c)

    s0 = jnp.zeros((bsz, 2, cw // LANES, LANES, LANES), F32)
    b3 = lambda a, n: a.reshape(bsz, n, cw)
    b4 = lambda a, n: a.reshape(2, bsz, n, cw)
    ocf, ocb, s_c = _wkv_scan(b3(r_c, lc), b3(v_c, lc), b3(kk_c, lc), b4(lw_c, lc), b4(kd_c, lc),
                              b4(bb_c, lc), s0)
    olf, olb, _ = _wkv_scan(b3(r_l, t), b3(v_l, t), b3(kk_l, t), b4(lw_l, t), b4(kd_l, t), b4(bb_l, t), s_c)

    w_out0 = rw_out[0].astype(BF16)
    readout = functools.partial(_rwkv_readout, gn_g=rw_gn_g[0], gn_b=rw_gn_b[0], seg=seg64, w_out=w_out0)
    x1 = readout(olf.reshape(bsz * t, cw), olb.reshape(bsz * t, cw), bonus_l, g_l, x2,
                 lat(gate, 0), rows_per_group=t, tm=tm_stream)
    xc1 = readout(ocf.reshape(bsz * lc, cw), ocb.reshape(bsz * lc, cw), bonus_c, g_c, xc2,
                  con(gate, 0), rows_per_group=bsz * lc, tm=tm_c)

    qw = DA_HEADS * 2 * DA_HEAD
    w_in1 = da_in[0].astype(BF16)
    seg_da = _segment_matrix(DA_HEAD)
    cos, sin_signed = _rope_tables(t // GRID_W)
    q_b, k_b, vt_b, g2_l = _attn_in(x1, lat(mult, 1), lat(shift, 1), w_in1, True, cos, sin_signed, da_qn[0],
                                    da_kn[0], seg_da, qw, seq_len=t, rows_per_group=t, tm=tm_l)
    kc_b, vtc_b = _attn_in(xc1, con(mult, 1), con(shift, 1), w_in1[:, qw:3 * qw], False, None, None, None,
                           da_kn[0], seg_da, qw, seq_len=lc, rows_per_group=bsz * lc, tm=tm_c)
    lam_init = 0.8 - 0.6 * math.exp(-0.3 * 1)
    attn = functools.partial(_diff_attn, lam_vecs=da_lam[0], subln=da_subln[0], lam_init=lam_init,
                             tq=min(ATTN_Q_TILE, t), tk=min(ATTN_KEY_CHUNK, t))
    score_bound = DA_HEAD ** 0.5 * jnp.max(jnp.abs(da_qn[0])) * jnp.max(jnp.abs(da_kn[0]))
    y_att = lax.cond(score_bound < MAX_UNSHIFTED_SCORE,
                     functools.partial(attn, bounded=True), functools.partial(attn, bounded=False),
                     q_b.reshape(bsz, t, qw), kc_b.reshape(bsz, lc, qw), k_b.reshape(bsz, t, qw), vtc_b, vt_b)
    out = _gated_out(y_att.reshape(bsz * t, qw), g2_l, x1, lat(gate, 1), da_out[0].astype(BF16),
                     rows_per_group=t, tm=tm_stream)
    return out.reshape(bsz, t, d)
```

```python
import functools
import math

import jax
import jax.numpy as jnp
from jax import lax
from jax.experimental import pallas as pl
from jax.experimental.pallas import tpu as pltpu

F32 = jnp.float32
BF16 = jnp.bfloat16

GRID_W = 64
RMS_EPS = 1e-6
RW_HEAD = 64
W_LORA = 64
A_LORA = 64
GN_EPS = 64e-5
DA_HEADS = 8
DA_HEAD = 64
DA_VHEAD = 2 * DA_HEAD
SUBLN_EPS = 1e-5
ROPE_THETA = 10000.0
ROPE_FREQS = DA_HEAD // 4

LANES = 128
SUBLANES = 8
SCAN_CHUNK = 64
VMEM_LIMIT = 56 * 1024 * 1024
MAX_UNSHIFTED_SCORE = 40.0
PROJ_ROW_TILE = 256
STREAM_ROW_TILE = 512
ATTN_Q_TILE = 1024
ATTN_KEY_CHUNK = 1024


def _cparams(sem):
    return pltpu.CompilerParams(dimension_semantics=sem, vmem_limit_bytes=VMEM_LIMIT)


def _dot(a, b):
    return jnp.dot(a, b, preferred_element_type=F32)


def _dot_nt(a, b):
    return lax.dot_general(a, b, (((1,), (1,)), ((), ())), preferred_element_type=F32)


def _dot_tn(a, b):
    return lax.dot_general(a, b, (((0,), (0,)), ((), ())), preferred_element_type=F32)


def _split2(x):
    hi = x.astype(BF16)
    lo = (x - hi.astype(F32)).astype(BF16)
    return hi, lo


def _seg_sum(x, seg2):
    out = []
    for j in range(x.shape[-1] // LANES):
        hi, lo = _split2(x[:, j * LANES:(j + 1) * LANES])
        out.append(_dot(jnp.concatenate([hi, lo], axis=1), seg2))
    return jnp.concatenate(out, axis=1)


def _sigmoid(x):
    return 1.0 / (1.0 + jnp.exp(-x))


def _silu(x):
    return x * _sigmoid(x)


def _ada_kernel(c_ref, w_ref, b_ref, o_ref):
    s = _silu(c_ref[...])
    o_ref[...] = jnp.dot(s, w_ref[...], preferred_element_type=F32,
                         precision=lax.Precision.HIGHEST) + b_ref[...]


def _ada_mod(rows, ada_w, ada_b):
    depth, d, n = ada_w.shape
    tn = 768
    return pl.pallas_call(
        _ada_kernel,
        grid=(depth, n // tn),
        in_specs=[pl.BlockSpec((SUBLANES, d), lambda i, j: (0, 0)),
                  pl.BlockSpec((None, d, tn), lambda i, j: (i, 0, j)),
                  pl.BlockSpec((None, 1, tn), lambda i, j: (i, 0, j))],
        out_specs=pl.BlockSpec((None, SUBLANES, tn), lambda i, j: (i, 0, j)),
        out_shape=jax.ShapeDtypeStruct((depth, SUBLANES, n), F32),
        compiler_params=_cparams(("arbitrary", "arbitrary")),
        name="ada_mod",
    )(rows, ada_w, ada_b.reshape(depth, 1, n))


def _normmod(x, mult, shift):
    y = x * lax.rsqrt(jnp.mean(x * x, axis=-1, keepdims=True) + RMS_EPS)
    return (y * mult + shift).astype(BF16)


def _token_shift(u, prev8, next8, mu, first, last):
    tm = u.shape[0]
    row8 = lax.broadcasted_iota(jnp.int32, (SUBLANES, u.shape[1]), 0)
    prow = jnp.where(first, 0.0, prev8[SUBLANES - 1:SUBLANES, :])
    nrow = jnp.where(last, 0.0, next8[0:1, :])
    nbr = pltpu.roll(u, 1, 0) + pltpu.roll(u, tm - 1, 0)
    top = nbr[:SUBLANES] + jnp.where(row8 == 0, prow - u[tm - 1:tm], 0.0)
    bot = nbr[tm - SUBLANES:] + jnp.where(row8 == SUBLANES - 1, nrow - u[0:1], 0.0)
    nbr = jnp.concatenate([top, nbr[SUBLANES:tm - SUBLANES], bot], axis=0)
    return (1.0 - mu) * u + (0.5 * mu) * nbr


def _rwkv_in_kernel(x_ref, xp_ref, xn_ref, mult_ref, shift_ref, w_ref,
                    mua_ref, mub_ref, w2_ref, w0_ref, a2_ref, a0_ref, kk_ref, ka_ref, rk_ref, seg_ref,
                    r_ref, v_ref, kkn_ref, g_ref, bonus_ref, lw_ref, kd_ref, bb_ref, *, tiles_per_seq, c):
    i = pl.program_id(0)
    first = (i % tiles_per_seq) == 0
    last = (i % tiles_per_seq) == tiles_per_seq - 1
    tm = x_ref.shape[0]
    lo, hi = SUBLANES, SUBLANES + tm
    x_ext = jnp.concatenate([xp_ref[...], x_ref[...], xn_ref[...]], axis=0)
    h = _normmod(x_ext, mult_ref[...], shift_ref[...])

    def project(c0, c1):
        return _dot(h, w_ref[:, c0:c1])

    def shifted(p, mu):
        return _token_shift(p[lo:hi], p[:lo], p[hi:], mu, first, last)

    p_lora = project(4 * c, w_ref.shape[1])
    p_k = project(c, 2 * c)
    mb = shifted(p_lora, mub_ref[...])
    nl = 2 * W_LORA
    w_raw = w0_ref[...] + _dot(jnp.tanh(mb[:, :nl]).astype(BF16), w2_ref[...])
    a = _sigmoid(a0_ref[...] + _dot(mb[:, nl:].astype(BF16), a2_ref[...]))
    logw = -math.exp(-0.5) / (1.0 + jnp.exp(-w_raw))
    for d in range(2):
        lw_ref[d] = logw[:, d * c:(d + 1) * c]

    p_r = project(0, c)
    k = shifted(p_k, mua_ref[:, c:2 * c])
    kk = k * kk_ref[...]
    kkn = kk * lax.rsqrt(_seg_sum(kk * kk, seg_ref[...]) + 1e-12)
    kkn_ref[...] = kkn
    kka = k * ka_ref[...]
    k_rest = k - kka
    kd = [k_rest + kka * a[:, d * c:(d + 1) * c] for d in range(2)]
    for d in range(2):
        kd_ref[d] = kd[d]
        bb_ref[d] = kkn * a[:, d * c:(d + 1) * c]

    p_v = project(2 * c, 3 * c)
    r = shifted(p_r, mua_ref[:, :c])
    r_ref[...] = r
    p_g = project(3 * c, 4 * c)
    v = shifted(p_v, mua_ref[:, 2 * c:3 * c])
    v_ref[...] = v
    bonus_ref[...] = _seg_sum(r * (kd[0] + kd[1]) * rk_ref[...], seg_ref[...]) * v
    g_ref[...] = p_g[lo:hi]


def _rwkv_in(x2d, mult, shift, w_in, seq_len, rows_per_group, mu, w0, w2, a0, a2, k_k, k_a, r_k, seg, tm):
    r_rows, dm = x2d.shape
    ncols = w_in.shape[1]
    c = k_k.shape[-1]
    nb = ncols - 4 * c
    assert nb == 2 * (W_LORA + A_LORA) and seq_len % tm == 0 and rows_per_group % tm == 0
    tps = seq_len // tm
    tpg = rows_per_group // tm
    t8 = tm // SUBLANES
    nblk8 = r_rows // SUBLANES
    mua = mu[:3 * c].reshape(1, 3 * c)
    mub = mu[3 * c:].reshape(1, nb)
    zw = jnp.zeros((W_LORA, c), F32)
    w2cat = jnp.concatenate([jnp.concatenate([w2[0], zw], 1), jnp.concatenate([zw, w2[1]], 1)], 0).astype(BF16)
    a2cat = jnp.concatenate([jnp.concatenate([a2[0], zw], 1), jnp.concatenate([zw, a2[1]], 1)], 0).astype(BF16)
    w0cat = w0.reshape(1, 2 * c)
    a0cat = a0.reshape(1, 2 * c)
    full = lambda shape: pl.BlockSpec(shape, lambda i: (0,) * len(shape))
    prev_idx = lambda i: jnp.maximum(i * t8 - 1, 0)
    next_idx = lambda i: jnp.minimum((i + 1) * t8, nblk8 - 1)
    row_out = pl.BlockSpec((tm, c), lambda i: (i, 0))
    dir_out = pl.BlockSpec((2, tm, c), lambda i: (0, i, 0))
    group = pl.BlockSpec((None, 1, dm), lambda i: (i // tpg, 0, 0))
    kern = functools.partial(_rwkv_in_kernel, tiles_per_seq=tps, c=c)
    return pl.pallas_call(
        kern,
        grid=(r_rows // tm,),
        in_specs=[pl.BlockSpec((tm, dm), lambda i: (i, 0)),
                  pl.BlockSpec((SUBLANES, dm), lambda i: (prev_idx(i), 0)),
                  pl.BlockSpec((SUBLANES, dm), lambda i: (next_idx(i), 0)),
                  group, group, full((dm, ncols)),
                  full((1, 3 * c)), full((1, nb)),
                  full((2 * W_LORA, 2 * c)), full((1, 2 * c)),
                  full((2 * A_LORA, 2 * c)), full((1, 2 * c)),
                  full((1, c)), full((1, c)), full((1, c)), full((2 * LANES, LANES))],
        out_specs=[row_out] * 5 + [dir_out] * 3,
        out_shape=[jax.ShapeDtypeStruct((r_rows, c), F32)] * 5
                  + [jax.ShapeDtypeStruct((2, r_rows, c), F32)] * 3,
        compiler_params=_cparams(("arbitrary",)),
        name="rwkv_in",
    )(x2d, x2d, x2d, mult, shift, w_in, mua, mub, w2cat, w0cat, a2cat, a0cat,
      k_k.reshape(1, c), k_a.reshape(1, c), r_k.reshape(1, c), seg)


def _head_pair_stack(xp, lane_a):
    return jnp.concatenate([jnp.where(lane_a, xp, 0.0), jnp.where(lane_a, 0.0, xp)], axis=0)


def _wkv_chunk_operands(r, v, kk, lw, kd, bb, reverse, L):
    ti = lax.broadcasted_iota(jnp.int32, (L, L), 0)
    si = lax.broadcasted_iota(jnp.int32, (L, L), 1)
    tri = ((ti <= si) if reverse else (ti >= si)).astype(BF16)
    hi = lw.astype(BF16)
    r1 = lw - hi.astype(F32)
    mid = r1.astype(BF16)
    lo = (r1 - mid.astype(F32)).astype(BF16)
    cum = _dot(tri, hi) + _dot(tri, mid) + _dot(tri, lo)
    tot = jnp.sum(lw, axis=0, keepdims=True)
    ginv = jnp.exp(-cum)
    gend = jnp.exp(tot) * ginv
    return dict(kt=kk * jnp.exp(cum - lw), rt=r * jnp.exp(cum), bt=bb * ginv,
                kdt=kd * ginv, khat=kd * gend, bhat=bb * gend, v=v, tot=tot)


def _wkv_scan_kernel(rf_ref, vf_ref, kkf_ref, lwf_ref, kdf_ref, bbf_ref,
                     rb_ref, vb_ref, kkb_ref, lwb_ref, kdb_ref, bbb_ref, s0_ref,
                     of_ref, ob_ref, sfin_ref, h_scr, *, chunk, n_pairs):
    L = chunk
    ci = pl.program_id(0)
    nc = pl.num_programs(0)
    nb = rf_ref.shape[0]

    @pl.when(ci == 0)
    def _():
        h_scr[...] = s0_ref[...]

    in_refs = ((rf_ref, vf_ref, kkf_ref, lwf_ref, kdf_ref, bbf_ref),
               (rb_ref, vb_ref, kkb_ref, lwb_ref, kdb_ref, bbb_ref))
    ops = {(b, d): _wkv_chunk_operands(*[ref[b] for ref in in_refs[d]], d == 1, L)
           for b in range(nb) for d in range(2)}
    o_refs = (of_ref, ob_ref)

    p2 = 2 * L
    lane_a = lax.broadcasted_iota(jnp.int32, (L, LANES), 1) < RW_HEAD
    dts = ((lax.broadcasted_iota(jnp.int32, (p2, p2), 0) & (L - 1))
           - (lax.broadcasted_iota(jnp.int32, (p2, p2), 1) & (L - 1)))
    strict = (dts > 0, dts < 0)
    incl = (dts >= 0, dts <= 0)
    eye = (lax.broadcasted_iota(jnp.int32, (p2, p2), 0)
           == lax.broadcasted_iota(jnp.int32, (p2, p2), 1)).astype(F32)

    chains = [(b, d, p) for p in range(n_pairs) for b in range(nb) for d in range(2)]
    lanes_of = lambda p: slice(p * LANES, (p + 1) * LANES)
    stk = lambda name, b, d, p: _head_pair_stack(ops[b, d][name][:, lanes_of(p)], lane_a)
    kt_s = [stk("kt", *ch).astype(BF16) for ch in chains]
    rt_s = [stk("rt", *ch).astype(BF16) for ch in chains]
    y_s = [jnp.concatenate([stk("bt", *ch).astype(BF16), stk("kdt", *ch).astype(BF16)], axis=0)
           for ch in chains]
    v_s = [stk("v", *ch).astype(BF16) for ch in chains]
    h = [h_scr[ch] for ch in chains]
    h_b = [x.astype(BF16) for x in h]
    n = range(len(chains))
    sc = [_dot_nt(jnp.concatenate([kt_s[i], rt_s[i]], axis=0), y_s[i]) for i in n]
    rh = [_dot(rt_s[i], h_b[i]) for i in n]
    msk = lambda m, x: jnp.where(m, x, 0.0)
    pw = [-msk(strict[d], sc[i][:p2, :p2]) for i, (b, d, p) in enumerate(chains)]
    a_k = [msk(strict[d], sc[i][:p2, p2:]).astype(BF16) for i, (b, d, p) in enumerate(chains)]
    pkb = [jnp.concatenate([msk(incl[d], sc[i][p2:, p2:]), -msk(incl[d], sc[i][p2:, :p2])],
                           axis=1).astype(BF16) for i, (b, d, p) in enumerate(chains)]
    u = [_dot(jnp.concatenate([kt_s[i], a_k[i]], axis=1), jnp.concatenate([h_b[i], v_s[i]], axis=0))
         for i in n]

    pwb = [x.astype(BF16) for x in pw]
    tinv = [eye + pw[i] for i in n]
    pw = [_dot(pwb[i], pwb[i]) for i in n]
    for j in range(1, int(math.log2(L))):
        pwb = [x.astype(BF16) for x in pw]
        if j < int(math.log2(L)) - 1:
            w = [_dot(jnp.concatenate([tinv[i].astype(BF16), pwb[i]], axis=0), pwb[i]) for i in n]
            tinv = [tinv[i] + w[i][:p2] for i in n]
            pw = [w[i][p2:] for i in n]
        else:
            tinv = [tinv[i] + _dot(tinv[i].astype(BF16), pwb[i]) for i in n]
    u = [_dot(tinv[i].astype(BF16), u[i].astype(BF16)) for i in n]

    vu = [jnp.concatenate([v_s[i], u[i].astype(BF16)], axis=0) for i in n]
    for i, (b, d, p) in enumerate(chains):
        o_s = rh[i] + _dot(pkb[i], vu[i])
        o_refs[d][b, :, lanes_of(p)] = o_s[:L] + o_s[L:]
    for i, (b, d, p) in enumerate(chains):
        kb = jnp.concatenate([stk("khat", b, d, p), -stk("bhat", b, d, p)], axis=0).astype(BF16)
        tot_col = jnp.broadcast_to(ops[b, d]["tot"][:, lanes_of(p)], (LANES, LANES)).T
        h_scr[b, d, p] = jnp.exp(tot_col) * h[i] + _dot_tn(kb, vu[i])

    @pl.when(ci == nc - 1)
    def _():
        sfin_ref[...] = h_scr[...]


def _wkv_scan(r, v, kkn, logw, kd, bb, s0, chunk=SCAN_CHUNK):
    b, t, c = r.shape
    n_pairs = c // LANES
    assert t % chunk == 0 and 2 * RW_HEAD == LANES
    nc = t // chunk
    fwd = lambda ci: ci
    bwd = lambda ci: nc - 1 - ci
    shared = lambda cmap: pl.BlockSpec((b, chunk, c), lambda ci: (0, cmap(ci), 0))
    perdir = lambda d, cmap: pl.BlockSpec((None, b, chunk, c), lambda ci: (d, 0, cmap(ci), 0))
    state = pl.BlockSpec((b, 2, n_pairs, LANES, LANES), lambda ci: (0, 0, 0, 0, 0))
    kern = functools.partial(_wkv_scan_kernel, chunk=chunk, n_pairs=n_pairs)
    return pl.pallas_call(
        kern,
        grid=(nc,),
        in_specs=[shared(fwd)] * 3 + [perdir(0, fwd)] * 3 + [shared(bwd)] * 3 + [perdir(1, bwd)] * 3 + [state],
        out_specs=[shared(fwd), shared(bwd), state],
        out_shape=[jax.ShapeDtypeStruct((b, t, c), F32), jax.ShapeDtypeStruct((b, t, c), F32),
                   jax.ShapeDtypeStruct((b, 2, n_pairs, LANES, LANES), F32)],
        scratch_shapes=[pltpu.VMEM((b, 2, n_pairs, LANES, LANES), F32)],
        compiler_params=_cparams(("arbitrary",)),
        name="wkv_scan",
    )(r, v, kkn, logw, kd, bb, r, v, kkn, logw, kd, bb, s0)


def _rwkv_readout_kernel(of_ref, ob_ref, bonus_ref, g_ref, x_ref, gate_ref,
                         gng_ref, gnb_ref, seg_ref, w_ref, out_ref):
    seg = seg_ref[...]
    inv_n = 1.0 / RW_HEAD
    o = of_ref[...] + ob_ref[...]
    mean = _seg_sum(o, seg) * inv_n
    dlt = o - mean
    var = _seg_sum(dlt * dlt, seg) * inv_n
    y = dlt * lax.rsqrt(var + GN_EPS) * gng_ref[...] + gnb_ref[...]
    y = (y + bonus_ref[...]) * _silu(g_ref[...])
    out_ref[...] = x_ref[...] + gate_ref[...] * _dot(y.astype(BF16), w_ref[...])


def _rwkv_readout(o_f, o_b, bonus, g, x2d, gate, gn_g, gn_b, seg, w_out, rows_per_group, tm):
    r_rows, c = g.shape
    d = x2d.shape[1]
    tpg = rows_per_group // tm
    full = lambda shape: pl.BlockSpec(shape, lambda i: (0,) * len(shape))
    row = lambda n: pl.BlockSpec((tm, n), lambda i: (i, 0))
    return pl.pallas_call(
        _rwkv_readout_kernel,
        grid=(r_rows // tm,),
        in_specs=[row(c), row(c), row(c), row(c), row(d),
                  pl.BlockSpec((None, 1, d), lambda i: (i // tpg, 0, 0)),
                  full((1, c)), full((1, c)), full((2 * LANES, LANES)), full((c, d))],
        out_specs=row(d),
        out_shape=jax.ShapeDtypeStruct((r_rows, d), F32),
        compiler_params=_cparams(("arbitrary",)),
        name="rwkv_readout",
    )(o_f, o_b, bonus, g, x2d, gate, gn_g.reshape(1, c), gn_b.reshape(1, c), seg, w_out)


def _head_norm(x, gain, seg):
    ms = _seg_sum(x * x, seg) * (1.0 / DA_HEAD)
    return x * lax.rsqrt(ms + RMS_EPS) * gain


def _rope(x, cos, sin_signed):
    n = x.shape[-1]
    lane = lax.broadcasted_iota(jnp.int32, x.shape, 1)
    half0 = (lane & (2 * ROPE_FREQS - 1)) < ROPE_FREQS
    partner = jnp.where(half0, pltpu.roll(x, n - ROPE_FREQS, 1), pltpu.roll(x, ROPE_FREQS, 1))
    reps = n // cos.shape[-1]
    wide = lambda tab: jnp.concatenate([tab] * reps, axis=1)
    return x * wide(cos) + partner * wide(sin_signed)


def _attn_in_kernel(*refs, latent, q_scale):
    it = iter(refs)
    x_ref, mult_ref, shift_ref, w_ref = next(it), next(it), next(it), next(it)
    cos_ref = next(it) if latent else None
    sin_ref = next(it) if latent else None
    qn_ref = next(it) if latent else None
    kn_ref, seg_ref = next(it), next(it)
    qo_ref = next(it) if latent else None
    ko_ref, vo_ref = next(it), next(it)
    go_ref = next(it) if latent else None
    h = _normmod(x_ref[...], mult_ref[...], shift_ref[...])
    w = ko_ref.shape[-1]
    cols = lambda j0, j1: _dot(h, w_ref[:, j0 * w:j1 * w])
    seg = seg_ref[...]
    if latent:
        p_k = cols(1, 2)
        p_q = cols(0, 1)
        k = _rope(_head_norm(p_k, kn_ref[...], seg), cos_ref[...], sin_ref[...])
        ko_ref[...] = k.astype(BF16)
        p_vg = cols(2, 4)
        q = _rope(_head_norm(p_q, qn_ref[...], seg), cos_ref[...], sin_ref[...])
        qo_ref[...] = (q * q_scale).astype(BF16)
        vo_ref[...] = p_vg[:, :w].T.astype(BF16)
        go_ref[...] = _silu(p_vg[:, w:]).astype(BF16)
    else:
        p_kv = cols(0, 2)
        ko_ref[...] = _head_norm(p_kv[:, :w], kn_ref[...], seg).astype(BF16)
        vo_ref[...] = p_kv[:, w:].T.astype(BF16)


def _attn_in(x2d, mult, shift, w_in, latent, cos, sin_signed, qn, kn, seg, w, seq_len, rows_per_group, tm):
    r_rows, dm = x2d.shape
    tps = seq_len // tm
    tpg = rows_per_group // tm
    full = lambda shape: pl.BlockSpec(shape, lambda i: (0,) * len(shape))
    group = pl.BlockSpec((None, 1, dm), lambda i: (i // tpg, 0, 0))
    tab = pl.BlockSpec((tm, LANES), lambda i: (i % tps, 0))
    tile_gain = lambda g: jnp.tile(g, w // g.shape[0]).reshape(1, w)
    in_specs = [pl.BlockSpec((tm, dm), lambda i: (i, 0)), group, group, full(w_in.shape)]
    args = [x2d, mult, shift, w_in]
    if latent:
        in_specs += [tab, tab, full((1, w))]
        args += [cos, sin_signed, tile_gain(qn)]
    in_specs += [full((1, w)), full((2 * LANES, LANES))]
    args += [tile_gain(kn), seg]
    rows_bf = (pl.BlockSpec((tm, w), lambda i: (i, 0)), jax.ShapeDtypeStruct((r_rows, w), BF16))
    v_t = (pl.BlockSpec((None, w, tm), lambda i: (i // tps, 0, i % tps)),
           jax.ShapeDtypeStruct((r_rows // seq_len, w, seq_len), BF16))
    outs = [rows_bf, rows_bf, v_t, rows_bf] if latent else [rows_bf, v_t]
    kern = functools.partial(_attn_in_kernel, latent=latent, q_scale=DA_HEAD ** -0.5)
    return pl.pallas_call(
        kern,
        grid=(r_rows // tm,),
        in_specs=in_specs,
        out_specs=[o[0] for o in outs],
        out_shape=[o[1] for o in outs],
        compiler_params=_cparams(("arbitrary",)),
        name="attn_in_lat" if latent else "attn_in_ctx",
    )(*args)


def _diff_attn_kernel(q_ref, kc_ref, kl_ref, vtc_ref, vtl_ref, lam_ref, sub_ref, o_ref, m_scr, l_scr, acc_scr,
                      *, lam_init, tk, bounded):
    q = q_ref[...]
    lane = lax.broadcasted_iota(jnp.int32, q.shape, 1)
    zero = jnp.zeros_like(q)
    qm = (jnp.where(lane < DA_HEAD, q, zero), jnp.where(lane < DA_HEAD, zero, q))
    if not bounded:
        m_scr[...] = jnp.full(m_scr.shape, -jnp.inf, F32)
        l_scr[...] = jnp.zeros(l_scr.shape, F32)
        acc_scr[...] = jnp.zeros(acc_scr.shape, F32)

    def accumulate(k, vt):
        for c in range(2):
            s = _dot_nt(k, qm[c])
            m_prev = m_scr[c]
            m_new = jnp.maximum(m_prev, jnp.max(s, axis=0, keepdims=True))
            alpha = jnp.exp(m_prev - m_new)
            p = jnp.exp(s - m_new)
            l_scr[c] = alpha * l_scr[c] + jnp.sum(p, axis=0, keepdims=True)
            acc_scr[c] = alpha * acc_scr[c] + _dot(vt, p.astype(BF16))
            m_scr[c] = m_new

    def body(j, carry):
        off = pl.multiple_of(j * tk, tk)
        accumulate(kl_ref[pl.ds(off, tk), :], vtl_ref[:, pl.ds(off, tk)])
        return carry

    if bounded:
        n_lat = kl_ref.shape[0] // tk
        keys = [kc_ref] + [kl_ref.at[pl.ds(j * tk, tk), :] for j in range(n_lat)]
        vals = [vtc_ref] + [vtl_ref.at[:, pl.ds(j * tk, tk)] for j in range(n_lat)]
        probs = lambda c, j: jnp.exp(_dot_nt(keys[j][...], qm[c]))

        def add_values(c, j, p):
            l_new = jnp.sum(p, axis=0, keepdims=True)
            acc_new = _dot(vals[j][...], p.astype(BF16))
            l_scr[c] = l_new if j == 0 else l_scr[c] + l_new
            acc_scr[c] = acc_new if j == 0 else acc_scr[c] + acc_new

        p = [probs(0, 0), probs(1, 0)]
        for j in range(len(keys)):
            more = j + 1 < len(keys)
            add_values(0, j, p[0])
            if more:
                p[0] = probs(0, j + 1)
            add_values(1, j, p[1])
            if more:
                p[1] = probs(1, j + 1)
    else:
        accumulate(kc_ref[...], vtc_ref[...])
        lax.fori_loop(0, kl_ref.shape[0] // tk, body, 0)

    lv = lam_ref[...]
    lam = (jnp.exp(jnp.sum(lv[0:1] * lv[1:2], axis=-1, keepdims=True))
           - jnp.exp(jnp.sum(lv[2:3] * lv[3:4], axis=-1, keepdims=True)) + lam_init)
    o = acc_scr[0] * (1.0 / l_scr[0]) - acc_scr[1] * (lam / l_scr[1])
    y = o * lax.rsqrt(jnp.mean(o * o, axis=0, keepdims=True) + SUBLN_EPS)
    o_ref[...] = (y * (sub_ref[...] * (1.0 - lam_init))).T.astype(BF16)


def _diff_attn(q, k_ctx, k_lat, vt_ctx, vt_lat, lam_vecs, subln, lam_init, tq, tk, bounded):
    b, t, w = q.shape
    lc = k_ctx.shape[1]
    nh = w // DA_VHEAD
    assert t % tq == 0 and t % tk == 0
    kern = functools.partial(_diff_attn_kernel, lam_init=lam_init, tk=tk, bounded=bounded)
    return pl.pallas_call(
        kern,
        grid=(b, nh, t // tq),
        in_specs=[pl.BlockSpec((None, tq, DA_VHEAD), lambda bi, h, i: (bi, i, h)),
                  pl.BlockSpec((None, lc, DA_VHEAD), lambda bi, h, i: (bi, 0, h)),
                  pl.BlockSpec((None, t, DA_VHEAD), lambda bi, h, i: (bi, 0, h)),
                  pl.BlockSpec((None, DA_VHEAD, lc), lambda bi, h, i: (bi, h, 0)),
                  pl.BlockSpec((None, DA_VHEAD, t), lambda bi, h, i: (bi, h, 0)),
                  pl.BlockSpec((4, DA_HEAD), lambda bi, h, i: (0, 0)),
                  pl.BlockSpec((DA_VHEAD, 1), lambda bi, h, i: (0, 0))],
        out_specs=pl.BlockSpec((None, tq, DA_VHEAD), lambda bi, h, i: (bi, i, h)),
        out_shape=jax.ShapeDtypeStruct((b, t, w), BF16),
        scratch_shapes=[pltpu.VMEM((2, 1, tq), F32), pltpu.VMEM((2, 1, tq), F32),
                        pltpu.VMEM((2, DA_VHEAD, tq), F32)],
        compiler_params=_cparams(("arbitrary", "arbitrary", "arbitrary")),
        name="diff_attn_bounded" if bounded else "diff_attn",
    )(q, k_ctx, k_lat, vt_ctx, vt_lat, lam_vecs, subln.reshape(DA_VHEAD, 1))


def _gated_out_kernel(y_ref, g_ref, x_ref, gate_ref, w_ref, out_ref):
    y = y_ref[...].astype(F32) * g_ref[...].astype(F32)
    out_ref[...] = x_ref[...] + gate_ref[...] * _dot(y.astype(BF16), w_ref[...])


def _gated_out(y, g, x2d, gate, w_out, rows_per_group, tm):
    r_rows, w = y.shape
    d = x2d.shape[1]
    tpg = rows_per_group // tm
    row = lambda n: pl.BlockSpec((tm, n), lambda i: (i, 0))
    return pl.pallas_call(
        _gated_out_kernel,
        grid=(r_rows // tm,),
        in_specs=[row(w), row(w), row(d),
                  pl.BlockSpec((None, 1, d), lambda i: (i // tpg, 0, 0)),
                  pl.BlockSpec((w, d), lambda i: (0, 0))],
        out_specs=row(d),
        out_shape=jax.ShapeDtypeStruct((r_rows, d), F32),
        compiler_params=_cparams(("arbitrary",)),
        name="gated_out",
    )(y, g, x2d, gate, w_out)


def _segment_matrix(width):
    i = (lax.broadcasted_iota(jnp.int32, (2 * LANES, LANES), 0) % LANES) // width
    j = lax.broadcasted_iota(jnp.int32, (2 * LANES, LANES), 1) // width
    return (i == j).astype(BF16)


def _rope_tables(rows):
    row_ids = jnp.repeat(jnp.arange(rows), GRID_W).astype(F32)
    col_ids = jnp.tile(jnp.arange(GRID_W), rows).astype(F32)
    inv_freq = ROPE_THETA ** (-jnp.arange(ROPE_FREQS, dtype=F32) / ROPE_FREQS)
    ang_r = row_ids[:, None] * inv_freq
    ang_c = col_ids[:, None] * inv_freq
    cr, sr, cc, sc = jnp.cos(ang_r), jnp.sin(ang_r), jnp.cos(ang_c), jnp.sin(ang_c)
    cos = jnp.concatenate([cr, cr, cc, cc], axis=-1)
    sin_signed = jnp.concatenate([-sr, sr, -sc, sc], axis=-1)
    return jnp.tile(cos, (1, 2)), jnp.tile(sin_signed, (1, 2))


def kernel(x, c, ctx, c_ctx, ada_w, ada_b, norm_g, rw_in, rw_mu, rw_w0, rw_w2, rw_a0, rw_a2, rw_kk, rw_ka, rw_rk, rw_gn_g, rw_gn_b, rw_out, da_in, da_qn, da_kn, da_lam, da_subln, da_out):
    bsz, t, d = x.shape
    lc = ctx.shape[1]
    depth = ada_w.shape[0]
    assert depth == 2 and bsz + 1 <= SUBLANES
    cw = rw_kk.shape[-1]
    x2 = x.reshape(bsz * t, d)
    xc2 = ctx.reshape(bsz * lc, d)

    rows = jnp.zeros((SUBLANES, d), F32).at[:bsz].set(c).at[bsz].set(c_ctx)
    mod = _ada_mod(rows, ada_w, ada_b)
    shift, scale, gate = mod[:, :, :d], mod[:, :, d:2 * d], mod[:, :, 2 * d:]
    mult = norm_g[:, None, :] * (1.0 + scale)
    lat = lambda a, i: a[i, :bsz].reshape(bsz, 1, d)
    con = lambda a, i: a[i, bsz:bsz + 1].reshape(1, 1, d)

    tm_l = min(PROJ_ROW_TILE, t)
    tm_c = min(PROJ_ROW_TILE, lc)
    tm_stream = min(STREAM_ROW_TILE, t)

    w_in0 = rw_in[0].astype(BF16)
    seg64 = _segment_matrix(RW_HEAD)
    rwkv_in = functools.partial(_rwkv_in, w_in=w_in0, mu=rw_mu[0], w0=rw_w0[0], w2=rw_w2[0], a0=rw_a0[0],
                                a2=rw_a2[0], k_k=rw_kk[0], k_a=rw_ka[0], r_k=rw_rk[0].reshape(-1), seg=seg64)
    r_l, v_l, kk_l, g_l, bonus_l, lw_l, kd_l, bb_l = rwkv_in(x2, lat(mult, 0), lat(shift, 0), seq_len=t,
                                                             rows_per_group=t, tm=tm_l)
    r_c, v_c, kk_c, g_c, bonus_c, lw_c, kd_c, bb_c = rwkv_in(xc2, con(mult, 0), con(shift, 0), seq_len=lc,
                                                             rows_per_group=bsz * lc, tm=tm_c)

    s0 = jnp.zeros((bsz, 2, cw // LANES, LANES, LANES), F32)
    b3 = lambda a, n: a.reshape(bsz, n, cw)
    b4 = lambda a, n: a.reshape(2, bsz, n, cw)
    ocf, ocb, s_c = _wkv_scan(b3(r_c, lc), b3(v_c, lc), b3(kk_c, lc), b4(lw_c, lc), b4(kd_c, lc),
                              b4(bb_c, lc), s0)
    olf, olb, _ = _wkv_scan(b3(r_l, t), b3(v_l, t), b3(kk_l, t), b4(lw_l, t), b4(kd_l, t), b4(bb_l, t), s_c)

    w_out0 = rw_out[0].astype(BF16)
    readout = functools.partial(_rwkv_readout, gn_g=rw_gn_g[0], gn_b=rw_gn_b[0], seg=seg64, w_out=w_out0)
    x1 = readout(olf.reshape(bsz * t, cw), olb.reshape(bsz * t, cw), bonus_l, g_l, x2,
                 lat(gate, 0), rows_per_group=t, tm=tm_stream)
    xc1 = readout(ocf.reshape(bsz * lc, cw), ocb.reshape(bsz * lc, cw), bonus_c, g_c, xc2,
                  con(gate, 0), rows_per_group=bsz * lc, tm=tm_c)

    qw = DA_HEADS * 2 * DA_HEAD
    w_in1 = da_in[0].astype(BF16)
    seg_da = _segment_matrix(DA_HEAD)
    cos, sin_signed = _rope_tables(t // GRID_W)
    q_b, k_b, vt_b, g2_l = _attn_in(x1, lat(mult, 1), lat(shift, 1), w_in1, True, cos, sin_signed, da_qn[0],
                                    da_kn[0], seg_da, qw, seq_len=t, rows_per_group=t, tm=tm_l)
    kc_b, vtc_b = _attn_in(xc1, con(mult, 1), con(shift, 1), w_in1[:, qw:3 * qw], False, None, None, None,
                           da_kn[0], seg_da, qw, seq_len=lc, rows_per_group=bsz * lc, tm=tm_c)
    lam_init = 0.8 - 0.6 * math.exp(-0.3 * 1)
    attn = functools.partial(_diff_attn, lam_vecs=da_lam[0], subln=da_subln[0], lam_init=lam_init,
                             tq=min(ATTN_Q_TILE, t), tk=min(ATTN_KEY_CHUNK, t))
    score_bound = DA_HEAD ** 0.5 * jnp.max(jnp.abs(da_qn[0])) * jnp.max(jnp.abs(da_kn[0]))
    y_att = lax.cond(score_bound < MAX_UNSHIFTED_SCORE,
                     functools.partial(attn, bounded=True), functools.partial(attn, bounded=False),
                     q_b.reshape(bsz, t, qw), kc_b.reshape(bsz, lc, qw), k_b.reshape(bsz, t, qw), vtc_b, vt_b)
    out = _gated_out(y_att.reshape(bsz * t, qw), g2_l, x1, lat(gate, 1), da_out[0].astype(BF16),
                     rows_per_group=t, tm=tm_stream)
    return out.reshape(bsz, t, d)
```

```python
import functools
import math

import jax
import jax.numpy as jnp
from jax import lax
from jax.experimental import pallas as pl
from jax.experimental.pallas import tpu as pltpu

F32 = jnp.float32
BF16 = jnp.bfloat16

GRID_W = 64
RMS_EPS = 1e-6
RW_HEAD = 64
W_LORA = 64
A_LORA = 64
GN_EPS = 64e-5
DA_HEADS = 8
DA_HEAD = 64
DA_VHEAD = 2 * DA_HEAD
SUBLN_EPS = 1e-5
ROPE_THETA = 10000.0
ROPE_FREQS = DA_HEAD // 4

LANES = 128
SUBLANES = 8
SCAN_CHUNK = 64
VMEM_LIMIT = 56 * 1024 * 1024
MAX_UNSHIFTED_SCORE = 40.0
PROJ_ROW_TILE = 256
STREAM_ROW_TILE = 512
ATTN_Q_TILE = 1024
ATTN_KEY_CHUNK = 1024


def _cparams(sem):
    return pltpu.CompilerParams(dimension_semantics=sem, vmem_limit_bytes=VMEM_LIMIT)


def _dot(a, b):
    return jnp.dot(a, b, preferred_element_type=F32)


def _dot_nt(a, b):
    return lax.dot_general(a, b, (((1,), (1,)), ((), ())), preferred_element_type=F32)


def _dot_tn(a, b):
    return lax.dot_general(a, b, (((0,), (0,)), ((), ())), preferred_element_type=F32)


def _split2(x):
    hi = x.astype(BF16)
    lo = (x - hi.astype(F32)).astype(BF16)
    return hi, lo


def _seg_sum(x, seg2):
    out = []
    for j in range(x.shape[-1] // LANES):
        hi, lo = _split2(x[:, j * LANES:(j + 1) * LANES])
        out.append(_dot(jnp.concatenate([hi, lo], axis=1), seg2))
    return jnp.concatenate(out, axis=1)


def _sigmoid(x):
    return 1.0 / (1.0 + jnp.exp(-x))


def _silu(x):
    return x * _sigmoid(x)


def _ada_kernel(c_ref, w_ref, b_ref, o_ref):
    s = _silu(c_ref[...])
    o_ref[...] = jnp.dot(s, w_ref[...], preferred_element_type=F32,
                         precision=lax.Precision.HIGHEST) + b_ref[...]


def _ada_mod(rows, ada_w, ada_b):
    depth, d, n = ada_w.shape
    tn = 768
    return pl.pallas_call(
        _ada_kernel,
        grid=(depth, n // tn),
        in_specs=[pl.BlockSpec((SUBLANES, d), lambda i, j: (0, 0)),
                  pl.BlockSpec((None, d, tn), lambda i, j: (i, 0, j)),
                  pl.BlockSpec((None, 1, tn), lambda i, j: (i, 0, j))],
        out_specs=pl.BlockSpec((None, SUBLANES, tn), lambda i, j: (i, 0, j)),
        out_shape=jax.ShapeDtypeStruct((depth, SUBLANES, n), F32),
        compiler_params=_cparams(("arbitrary", "arbitrary")),
        name="ada_mod",
    )(rows, ada_w, ada_b.reshape(depth, 1, n))


def _normmod(x, mult, shift):
    y = x * lax.rsqrt(jnp.mean(x * x, axis=-1, keepdims=True) + RMS_EPS)
    return (y * mult + shift).astype(BF16)


def _token_shift(u, prev8, next8, mu, first, last):
    tm = u.shape[0]
    row8 = lax.broadcasted_iota(jnp.int32, (SUBLANES, u.shape[1]), 0)
    prow = jnp.where(first, 0.0, prev8[SUBLANES - 1:SUBLANES, :])
    nrow = jnp.where(last, 0.0, next8[0:1, :])
    nbr = pltpu.roll(u, 1, 0) + pltpu.roll(u, tm - 1, 0)
    top = nbr[:SUBLANES] + jnp.where(row8 == 0, prow - u[tm - 1:tm], 0.0)
    bot = nbr[tm - SUBLANES:] + jnp.where(row8 == SUBLANES - 1, nrow - u[0:1], 0.0)
    nbr = jnp.concatenate([top, nbr[SUBLANES:tm - SUBLANES], bot], axis=0)
    return (1.0 - mu) * u + (0.5 * mu) * nbr


def _rwkv_in_kernel(x_ref, xp_ref, xn_ref, mult_ref, shift_ref, w_ref,
                    mua_ref, mub_ref, w2_ref, w0_ref, a2_ref, a0_ref, kk_ref, ka_ref, rk_ref, seg_ref,
                    r_ref, v_ref, kkn_ref, g_ref, bonus_ref, lw_ref, kd_ref, bb_ref, *, tiles_per_seq, c):
    i = pl.program_id(0)
    first = (i % tiles_per_seq) == 0
    last = (i % tiles_per_seq) == tiles_per_seq - 1
    tm = x_ref.shape[0]
    lo, hi = SUBLANES, SUBLANES + tm
    x_ext = jnp.concatenate([xp_ref[...], x_ref[...], xn_ref[...]], axis=0)
    h = _normmod(x_ext, mult_ref[...], shift_ref[...])

    def project(c0, c1):
        return _dot(h, w_ref[:, c0:c1])

    def shifted(p, mu):
        return _token_shift(p[lo:hi], p[:lo], p[hi:], mu, first, last)

    p_lora = project(4 * c, w_ref.shape[1])
    p_k = project(c, 2 * c)
    mb = shifted(p_lora, mub_ref[...])
    nl = 2 * W_LORA
    w_raw = w0_ref[...] + _dot(jnp.tanh(mb[:, :nl]).astype(BF16), w2_ref[...])
    a = _sigmoid(a0_ref[...] + _dot(mb[:, nl:].astype(BF16), a2_ref[...]))
    logw = -math.exp(-0.5) / (1.0 + jnp.exp(-w_raw))
    for d in range(2):
        lw_ref[d] = logw[:, d * c:(d + 1) * c]

    p_r = project(0, c)
    k = shifted(p_k, mua_ref[:, c:2 * c])
    kk = k * kk_ref[...]
    kkn = kk * lax.rsqrt(_seg_sum(kk * kk, seg_ref[...]) + 1e-12)
    kkn_ref[...] = kkn
    kka = k * ka_ref[...]
    k_rest = k - kka
    kd = [k_rest + kka * a[:, d * c:(d + 1) * c] for d in range(2)]
    for d in range(2):
        kd_ref[d] = kd[d]
        bb_ref[d] = kkn * a[:, d * c:(d + 1) * c]

    p_v = project(2 * c, 3 * c)
    r = shifted(p_r, mua_ref[:, :c])
    r_ref[...] = r
    p_g = project(3 * c, 4 * c)
    v = shifted(p_v, mua_ref[:, 2 * c:3 * c])
    v_ref[...] = v
    bonus_ref[...] = _seg_sum(r * (kd[0] + kd[1]) * rk_ref[...], seg_ref[...]) * v
    g_ref[...] = p_g[lo:hi]


def _rwkv_in(x2d, mult, shift, w_in, seq_len, rows_per_group, mu, w0, w2, a0, a2, k_k, k_a, r_k, seg, tm):
    r_rows, dm = x2d.shape
    ncols = w_in.shape[1]
    c = k_k.shape[-1]
    nb = ncols - 4 * c
    assert nb == 2 * (W_LORA + A_LORA) and seq_len % tm == 0 and rows_per_group % tm == 0
    tps = seq_len // tm
    tpg = rows_per_group // tm
    t8 = tm // SUBLANES
    nblk8 = r_rows // SUBLANES
    mua = mu[:3 * c].reshape(1, 3 * c)
    mub = mu[3 * c:].reshape(1, nb)
    zw = jnp.zeros((W_LORA, c), F32)
    w2cat = jnp.concatenate([jnp.concatenate([w2[0], zw], 1), jnp.concatenate([zw, w2[1]], 1)], 0).astype(BF16)
    a2cat = jnp.concatenate([jnp.concatenate([a2[0], zw], 1), jnp.concatenate([zw, a2[1]], 1)], 0).astype(BF16)
    w0cat = w0.reshape(1, 2 * c)
    a0cat = a0.reshape(1, 2 * c)
    full = lambda shape: pl.BlockSpec(shape, lambda i: (0,) * len(shape))
    prev_idx = lambda i: jnp.maximum(i * t8 - 1, 0)
    next_idx = lambda i: jnp.minimum((i + 1) * t8, nblk8 - 1)
    row_out = pl.BlockSpec((tm, c), lambda i: (i, 0))
    dir_out = pl.BlockSpec((2, tm, c), lambda i: (0, i, 0))
    group = pl.BlockSpec((None, 1, dm), lambda i: (i // tpg, 0, 0))
    kern = functools.partial(_rwkv_in_kernel, tiles_per_seq=tps, c=c)
    return pl.pallas_call(
        kern,
        grid=(r_rows // tm,),
        in_specs=[pl.BlockSpec((tm, dm), lambda i: (i, 0)),
                  pl.BlockSpec((SUBLANES, dm), lambda i: (prev_idx(i), 0)),
                  pl.BlockSpec((SUBLANES, dm), lambda i: (next_idx(i), 0)),
                  group, group, full((dm, ncols)),
                  full((1, 3 * c)), full((1, nb)),
                  full((2 * W_LORA, 2 * c)), full((1, 2 * c)),
                  full((2 * A_LORA, 2 * c)), full((1, 2 * c)),
                  full((1, c)), full((1, c)), full((1, c)), full((2 * LANES, LANES))],
        out_specs=[row_out] * 5 + [dir_out] * 3,
        out_shape=[jax.ShapeDtypeStruct((r_rows, c), F32)] * 5
                  + [jax.ShapeDtypeStruct((2, r_rows, c), F32)] * 3,
        compiler_params=_cparams(("arbitrary",)),
        name="rwkv_in",
    )(x2d, x2d, x2d, mult, shift, w_in, mua, mub, w2cat, w0cat, a2cat, a0cat,
      k_k.reshape(1, c), k_a.reshape(1, c), r_k.reshape(1, c), seg)


def _head_pair_stack(xp, lane_a):
    return jnp.concatenate([jnp.where(lane_a, xp, 0.0), jnp.where(lane_a, 0.0, xp)], axis=0)


def _wkv_chunk_operands(r, v, kk, lw, kd, bb, reverse, L):
    ti = lax.broadcasted_iota(jnp.int32, (L, L), 0)
    si = lax.broadcasted_iota(jnp.int32, (L, L), 1)
    tri = ((ti <= si) if reverse else (ti >= si)).astype(BF16)
    hi = lw.astype(BF16)
    r1 = lw - hi.astype(F32)
    mid = r1.astype(BF16)
    lo = (r1 - mid.astype(F32)).astype(BF16)
    cum = _dot(tri, hi) + _dot(tri, mid) + _dot(tri, lo)
    tot = jnp.sum(lw, axis=0, keepdims=True)
    ginv = jnp.exp(-cum)
    gend = jnp.exp(tot) * ginv
    return dict(kt=kk * jnp.exp(cum - lw), rt=r * jnp.exp(cum), bt=bb * ginv,
                kdt=kd * ginv, khat=kd * gend, bhat=bb * gend, v=v, tot=tot)


def _wkv_scan_kernel(rf_ref, vf_ref, kkf_ref, lwf_ref, kdf_ref, bbf_ref,
                     rb_ref, vb_ref, kkb_ref, lwb_ref, kdb_ref, bbb_ref, s0_ref,
                     of_ref, ob_ref, sfin_ref, h_scr, *, chunk, n_pairs):
    L = chunk
    ci = pl.program_id(0)
    nc = pl.num_programs(0)
    nb = rf_ref.shape[0]

    @pl.when(ci == 0)
    def _():
        h_scr[...] = s0_ref[...]

    in_refs = ((rf_ref, vf_ref, kkf_ref, lwf_ref, kdf_ref, bbf_ref),
               (rb_ref, vb_ref, kkb_ref, lwb_ref, kdb_ref, bbb_ref))
    ops = {(b, d): _wkv_chunk_operands(*[ref[b] for ref in in_refs[d]], d == 1, L)
           for b in range(nb) for d in range(2)}
    o_refs = (of_ref, ob_ref)

    lane_a = lax.broadcasted_iota(jnp.int32, (L, LANES), 1) < RW_HEAD
    dts = (lax.broadcasted_iota(jnp.int32, (L, LANES), 0)
           - (lax.broadcasted_iota(jnp.int32, (L, LANES), 1) & (L - 1)))
    strict = (dts > 0, dts < 0)
    incl = (dts >= 0, dts <= 0)
    eye = (dts == 0).astype(F32)
    same_head = ((lax.broadcasted_iota(jnp.int32, (LANES, LANES), 0) < RW_HEAD)
                 == (lax.broadcasted_iota(jnp.int32, (LANES, LANES), 1) < RW_HEAD))
    bd = lambda x: _head_pair_stack(x, lane_a).astype(BF16)

    chains = [(b, d, p) for p in range(n_pairs) for b in range(nb) for d in range(2)]
    lanes_of = lambda p: slice(p * LANES, (p + 1) * LANES)
    cat = lambda name, b, d, p: ops[b, d][name][:, lanes_of(p)]
    kt_c = [cat("kt", *ch).astype(BF16) for ch in chains]
    rt_c = [cat("rt", *ch).astype(BF16) for ch in chains]
    y_s = [jnp.concatenate([bd(cat("bt", *ch)), bd(cat("kdt", *ch))], axis=0) for ch in chains]
    v_c = [cat("v", *ch) for ch in chains]
    v_s = [bd(x) for x in v_c]
    h = [h_scr[ch] for ch in chains]
    h_b = [x.astype(BF16) for x in h]
    n = range(len(chains))
    sc = [_dot_nt(jnp.concatenate([kt_c[i], rt_c[i]], axis=0), y_s[i]) for i in n]
    msk = lambda m, x: jnp.where(m, x, 0.0)
    pw = [-msk(strict[d], sc[i][:L, :LANES]) for i, (b, d, p) in enumerate(chains)]
    a_k = [msk(strict[d], sc[i][:L, LANES:]).astype(BF16) for i, (b, d, p) in enumerate(chains)]
    pkb = [jnp.concatenate([msk(incl[d], sc[i][L:, LANES:]), -msk(incl[d], sc[i][L:, :LANES])],
                           axis=1).astype(BF16) for i, (b, d, p) in enumerate(chains)]
    u = [_dot(jnp.concatenate([kt_c[i], a_k[i]], axis=1), jnp.concatenate([h_b[i], v_s[i]], axis=0))
         for i in n]

    tinv = [eye + pw[i] for i in n]
    pw = [_dot(pw[i].astype(BF16), bd(pw[i])) for i in n]
    for j in range(1, int(math.log2(L))):
        if j < int(math.log2(L)) - 1:
            w = [_dot(jnp.concatenate([tinv[i], pw[i]], axis=0).astype(BF16), bd(pw[i])) for i in n]
            tinv = [tinv[i] + w[i][:L] for i in n]
            pw = [w[i][L:] for i in n]
        else:
            tinv = [tinv[i] + _dot(tinv[i].astype(BF16), bd(pw[i])) for i in n]
    u = [_dot(tinv[i].astype(BF16), bd(u[i])) for i in n]

    for i, (b, d, p) in enumerate(chains):
        lhs = jnp.concatenate([rt_c[i], pkb[i]], axis=1)
        rhs = jnp.concatenate([h_b[i], v_s[i], bd(u[i])], axis=0)
        o_refs[d][b, :, lanes_of(p)] = _dot(lhs, rhs)
    for i, (b, d, p) in enumerate(chains):
        kb = jnp.concatenate([cat("khat", b, d, p), -cat("bhat", b, d, p)], axis=0).astype(BF16)
        vu = jnp.concatenate([v_c[i], u[i]], axis=0).astype(BF16)
        tot_col = jnp.broadcast_to(ops[b, d]["tot"][:, lanes_of(p)], (LANES, LANES)).T
        h_scr[b, d, p] = jnp.exp(tot_col) * h[i] + jnp.where(same_head, _dot_tn(kb, vu), 0.0)

    @pl.when(ci == nc - 1)
    def _():
        sfin_ref[...] = h_scr[...]


def _wkv_scan(r, v, kkn, logw, kd, bb, s0, chunk=SCAN_CHUNK):
    b, t, c = r.shape
    n_pairs = c // LANES
    assert t % chunk == 0 and 2 * RW_HEAD == LANES
    nc = t // chunk
    fwd = lambda ci: ci
    bwd = lambda ci: nc - 1 - ci
    shared = lambda cmap: pl.BlockSpec((b, chunk, c), lambda ci: (0, cmap(ci), 0))
    perdir = lambda d, cmap: pl.BlockSpec((None, b, chunk, c), lambda ci: (d, 0, cmap(ci), 0))
    state = pl.BlockSpec((b, 2, n_pairs, LANES, LANES), lambda ci: (0, 0, 0, 0, 0))
    kern = functools.partial(_wkv_scan_kernel, chunk=chunk, n_pairs=n_pairs)
    return pl.pallas_call(
        kern,
        grid=(nc,),
        in_specs=[shared(fwd)] * 3 + [perdir(0, fwd)] * 3 + [shared(bwd)] * 3 + [perdir(1, bwd)] * 3 + [state],
        out_specs=[shared(fwd), shared(bwd), state],
        out_shape=[jax.ShapeDtypeStruct((b, t, c), F32), jax.ShapeDtypeStruct((b, t, c), F32),
                   jax.ShapeDtypeStruct((b, 2, n_pairs, LANES, LANES), F32)],
        scratch_shapes=[pltpu.VMEM((b, 2, n_pairs, LANES, LANES), F32)],
        compiler_params=_cparams(("arbitrary",)),
        name="wkv_scan",
    )(r, v, kkn, logw, kd, bb, r, v, kkn, logw, kd, bb, s0)


def _rwkv_readout_kernel(of_ref, ob_ref, bonus_ref, g_ref, x_ref, gate_ref,
                         gng_ref, gnb_ref, seg_ref, w_ref, out_ref):
    seg = seg_ref[...]
    inv_n = 1.0 / RW_HEAD
    o = of_ref[...] + ob_ref[...]
    mean = _seg_sum(o, seg) * inv_n
    dlt = o - mean
    var = _seg_sum(dlt * dlt, seg) * inv_n
    y = dlt * lax.rsqrt(var + GN_EPS) * gng_ref[...] + gnb_ref[...]
    y = (y + bonus_ref[...]) * _silu(g_ref[...])
    out_ref[...] = x_ref[...] + gate_ref[...] * _dot(y.astype(BF16), w_ref[...])


def _rwkv_readout(o_f, o_b, bonus, g, x2d, gate, gn_g, gn_b, seg, w_out, rows_per_group, tm):
    r_rows, c = g.shape
    d = x2d.shape[1]
    tpg = rows_per_group // tm
    full = lambda shape: pl.BlockSpec(shape, lambda i: (0,) * len(shape))
    row = lambda n: pl.BlockSpec((tm, n), lambda i: (i, 0))
    return pl.pallas_call(
        _rwkv_readout_kernel,
        grid=(r_rows // tm,),
        in_specs=[row(c), row(c), row(c), row(c), row(d),
                  pl.BlockSpec((None, 1, d), lambda i: (i // tpg, 0, 0)),
                  full((1, c)), full((1, c)), full((2 * LANES, LANES)), full((c, d))],
        out_specs=row(d),
        out_shape=jax.ShapeDtypeStruct((r_rows, d), F32),
        compiler_params=_cparams(("arbitrary",)),
        name="rwkv_readout",
    )(o_f, o_b, bonus, g, x2d, gate, gn_g.reshape(1, c), gn_b.reshape(1, c), seg, w_out)


def _head_norm(x, gain, seg):
    ms = _seg_sum(x * x, seg) * (1.0 / DA_HEAD)
    return x * lax.rsqrt(ms + RMS_EPS) * gain


def _rope(x, cos, sin_signed):
    n = x.shape[-1]
    lane = lax.broadcasted_iota(jnp.int32, x.shape, 1)
    half0 = (lane & (2 * ROPE_FREQS - 1)) < ROPE_FREQS
    partner = jnp.where(half0, pltpu.roll(x, n - ROPE_FREQS, 1), pltpu.roll(x, ROPE_FREQS, 1))
    reps = n // cos.shape[-1]
    wide = lambda tab: jnp.concatenate([tab] * reps, axis=1)
    return x * wide(cos) + partner * wide(sin_signed)


def _attn_in_kernel(*refs, latent, q_scale):
    it = iter(refs)
    x_ref, mult_ref, shift_ref, w_ref = next(it), next(it), next(it), next(it)
    cos_ref = next(it) if latent else None
    sin_ref = next(it) if latent else None
    qn_ref = next(it) if latent else None
    kn_ref, seg_ref = next(it), next(it)
    qo_ref = next(it) if latent else None
    ko_ref, vo_ref = next(it), next(it)
    go_ref = next(it) if latent else None
    h = _normmod(x_ref[...], mult_ref[...], shift_ref[...])
    w = ko_ref.shape[-1]
    cols = lambda j0, j1: _dot(h, w_ref[:, j0 * w:j1 * w])
    seg = seg_ref[...]
    if latent:
        p_k = cols(1, 2)
        p_q = cols(0, 1)
        k = _rope(_head_norm(p_k, kn_ref[...], seg), cos_ref[...], sin_ref[...])
        ko_ref[...] = k.astype(BF16)
        p_vg = cols(2, 4)
        q = _rope(_head_norm(p_q, qn_ref[...], seg), cos_ref[...], sin_ref[...])
        qo_ref[...] = (q * q_scale).astype(BF16)
        vo_ref[...] = p_vg[:, :w].T.astype(BF16)
        go_ref[...] = _silu(p_vg[:, w:]).astype(BF16)
    else:
        p_kv = cols(0, 2)
        ko_ref[...] = _head_norm(p_kv[:, :w], kn_ref[...], seg).astype(BF16)
        vo_ref[...] = p_kv[:, w:].T.astype(BF16)


def _attn_in(x2d, mult, shift, w_in, latent, cos, sin_signed, qn, kn, seg, w, seq_len, rows_per_group, tm):
    r_rows, dm = x2d.shape
    tps = seq_len // tm
    tpg = rows_per_group // tm
    full = lambda shape: pl.BlockSpec(shape, lambda i: (0,) * len(shape))
    group = pl.BlockSpec((None, 1, dm), lambda i: (i // tpg, 0, 0))
    tab = pl.BlockSpec((tm, LANES), lambda i: (i % tps, 0))
    tile_gain = lambda g: jnp.tile(g, w // g.shape[0]).reshape(1, w)
    in_specs = [pl.BlockSpec((tm, dm), lambda i: (i, 0)), group, group, full(w_in.shape)]
    args = [x2d, mult, shift, w_in]
    if latent:
        in_specs += [tab, tab, full((1, w))]
        args += [cos, sin_signed, tile_gain(qn)]
    in_specs += [full((1, w)), full((2 * LANES, LANES))]
    args += [tile_gain(kn), seg]
    rows_bf = (pl.BlockSpec((tm, w), lambda i: (i, 0)), jax.ShapeDtypeStruct((r_rows, w), BF16))
    v_t = (pl.BlockSpec((None, w, tm), lambda i: (i // tps, 0, i % tps)),
           jax.ShapeDtypeStruct((r_rows // seq_len, w, seq_len), BF16))
    outs = [rows_bf, rows_bf, v_t, rows_bf] if latent else [rows_bf, v_t]
    kern = functools.partial(_attn_in_kernel, latent=latent, q_scale=DA_HEAD ** -0.5)
    return pl.pallas_call(
        kern,
        grid=(r_rows // tm,),
        in_specs=in_specs,
        out_specs=[o[0] for o in outs],
        out_shape=[o[1] for o in outs],
        compiler_params=_cparams(("arbitrary",)),
        name="attn_in_lat" if latent else "attn_in_ctx",
    )(*args)


def _diff_attn_kernel(q_ref, kc_ref, kl_ref, vtc_ref, vtl_ref, lam_ref, sub_ref, o_ref, m_scr, l_scr, acc_scr,
                      *, lam_init, tk, bounded):
    q = q_ref[...]
    lane = lax.broadcasted_iota(jnp.int32, q.shape, 1)
    zero = jnp.zeros_like(q)
    qm = (jnp.where(lane < DA_HEAD, q, zero), jnp.where(lane < DA_HEAD, zero, q))
    if not bounded:
        m_scr[...] = jnp.full(m_scr.shape, -jnp.inf, F32)
        l_scr[...] = jnp.zeros(l_scr.shape, F32)
        acc_scr[...] = jnp.zeros(acc_scr.shape, F32)

    def accumulate(k, vt):
        for c in range(2):
            s = _dot_nt(k, qm[c])
            m_prev = m_scr[c]
            m_new = jnp.maximum(m_prev, jnp.max(s, axis=0, keepdims=True))
            alpha = jnp.exp(m_prev - m_new)
            p = jnp.exp(s - m_new)
            l_scr[c] = alpha * l_scr[c] + jnp.sum(p, axis=0, keepdims=True)
            acc_scr[c] = alpha * acc_scr[c] + _dot(vt, p.astype(BF16))
            m_scr[c] = m_new

    def body(j, carry):
        off = pl.multiple_of(j * tk, tk)
        accumulate(kl_ref[pl.ds(off, tk), :], vtl_ref[:, pl.ds(off, tk)])
        return carry

    if bounded:
        n_lat = kl_ref.shape[0] // tk
        keys = [kc_ref] + [kl_ref.at[pl.ds(j * tk, tk), :] for j in range(n_lat)]
        vals = [vtc_ref] + [vtl_ref.at[:, pl.ds(j * tk, tk)] for j in range(n_lat)]
        probs = lambda c, j: jnp.exp(_dot_nt(keys[j][...], qm[c]))

        def add_values(c, j, p):
            l_new = jnp.sum(p, axis=0, keepdims=True)
            acc_new = _dot(vals[j][...], p.astype(BF16))
            l_scr[c] = l_new if j == 0 else l_scr[c] + l_new
            acc_scr[c] = acc_new if j == 0 else acc_scr[c] + acc_new

        p = [probs(0, 0), probs(1, 0)]
        for j in range(len(keys)):
            more = j + 1 < len(keys)
            add_values(0, j, p[0])
            if more:
                p[0] = probs(0, j + 1)
            add_values(1, j, p[1])
            if more:
                p[1] = probs(1, j + 1)
    else:
        accumulate(kc_ref[...], vtc_ref[...])
        lax.fori_loop(0, kl_ref.shape[0] // tk, body, 0)

    lv = lam_ref[...]
    lam = (jnp.exp(jnp.sum(lv[0:1] * lv[1:2], axis=-1, keepdims=True))
           - jnp.exp(jnp.sum(lv[2:3] * lv[3:4], axis=-1, keepdims=True)) + lam_init)
    o = acc_scr[0] * (1.0 / l_scr[0]) - acc_scr[1] * (lam / l_scr[1])
    y = o * lax.rsqrt(jnp.mean(o * o, axis=0, keepdims=True) + SUBLN_EPS)
    o_ref[...] = (y * (sub_ref[...] * (1.0 - lam_init))).T.astype(BF16)


def _diff_attn(q, k_ctx, k_lat, vt_ctx, vt_lat, lam_vecs, subln, lam_init, tq, tk, bounded):
    b, t, w = q.shape
    lc = k_ctx.shape[1]
    nh = w // DA_VHEAD
    assert t % tq == 0 and t % tk == 0
    kern = functools.partial(_diff_attn_kernel, lam_init=lam_init, tk=tk, bounded=bounded)
    return pl.pallas_call(
        kern,
        grid=(b, nh, t // tq),
        in_specs=[pl.BlockSpec((None, tq, DA_VHEAD), lambda bi, h, i: (bi, i, h)),
                  pl.BlockSpec((None, lc, DA_VHEAD), lambda bi, h, i: (bi, 0, h)),
                  pl.BlockSpec((None, t, DA_VHEAD), lambda bi, h, i: (bi, 0, h)),
                  pl.BlockSpec((None, DA_VHEAD, lc), lambda bi, h, i: (bi, h, 0)),
                  pl.BlockSpec((None, DA_VHEAD, t), lambda bi, h, i: (bi, h, 0)),
                  pl.BlockSpec((4, DA_HEAD), lambda bi, h, i: (0, 0)),
                  pl.BlockSpec((DA_VHEAD, 1), lambda bi, h, i: (0, 0))],
        out_specs=pl.BlockSpec((None, tq, DA_VHEAD), lambda bi, h, i: (bi, i, h)),
        out_shape=jax.ShapeDtypeStruct((b, t, w), BF16),
        scratch_shapes=[pltpu.VMEM((2, 1, tq), F32), pltpu.VMEM((2, 1, tq), F32),
                        pltpu.VMEM((2, DA_VHEAD, tq), F32)],
        compiler_params=_cparams(("arbitrary", "arbitrary", "arbitrary")),
        name="diff_attn_bounded" if bounded else "diff_attn",
    )(q, k_ctx, k_lat, vt_ctx, vt_lat, lam_vecs, subln.reshape(DA_VHEAD, 1))


def _gated_out_kernel(y_ref, g_ref, x_ref, gate_ref, w_ref, out_ref):
    y = y_ref[...].astype(F32) * g_ref[...].astype(F32)
    out_ref[...] = x_ref[...] + gate_ref[...] * _dot(y.astype(BF16), w_ref[...])


def _gated_out(y, g, x2d, gate, w_out, rows_per_group, tm):
    r_rows, w = y.shape
    d = x2d.shape[1]
    tpg = rows_per_group // tm
    row = lambda n: pl.BlockSpec((tm, n), lambda i: (i, 0))
    return pl.pallas_call(
        _gated_out_kernel,
        grid=(r_rows // tm,),
        in_specs=[row(w), row(w), row(d),
                  pl.BlockSpec((None, 1, d), lambda i: (i // tpg, 0, 0)),
                  pl.BlockSpec((w, d), lambda i: (0, 0))],
        out_specs=row(d),
        out_shape=jax.ShapeDtypeStruct((r_rows, d), F32),
        compiler_params=_cparams(("arbitrary",)),
        name="gated_out",
    )(y, g, x2d, gate, w_out)


def _segment_matrix(width):
    i = (lax.broadcasted_iota(jnp.int32, (2 * LANES, LANES), 0) % LANES) // width
    j = lax.broadcasted_iota(jnp.int32, (2 * LANES, LANES), 1) // width
    return (i == j).astype(BF16)


def _rope_tables(rows):
    row_ids = jnp.repeat(jnp.arange(rows), GRID_W).astype(F32)
    col_ids = jnp.tile(jnp.arange(GRID_W), rows).astype(F32)
    inv_freq = ROPE_THETA ** (-jnp.arange(ROPE_FREQS, dtype=F32) / ROPE_FREQS)
    ang_r = row_ids[:, None] * inv_freq
    ang_c = col_ids[:, None] * inv_freq
    cr, sr, cc, sc = jnp.cos(ang_r), jnp.sin(ang_r), jnp.cos(ang_c), jnp.sin(ang_c)
    cos = jnp.concatenate([cr, cr, cc, cc], axis=-1)
    sin_signed = jnp.concatenate([-sr, sr, -sc, sc], axis=-1)
    return jnp.tile(cos, (1, 2)), jnp.tile(sin_signed, (1, 2))


def kernel(x, c, ctx, c_ctx, ada_w, ada_b, norm_g, rw_in, rw_mu, rw_w0, rw_w2, rw_a0, rw_a2, rw_kk, rw_ka, rw_rk, rw_gn_g, rw_gn_b, rw_out, da_in, da_qn, da_kn, da_lam, da_subln, da_out):
    bsz, t, d = x.shape
    lc = ctx.shape[1]
    depth = ada_w.shape[0]
    assert depth == 2 and bsz + 1 <= SUBLANES
    cw = rw_kk.shape[-1]
    x2 = x.reshape(bsz * t, d)
    xc2 = ctx.reshape(bsz * lc, d)

    rows = jnp.zeros((SUBLANES, d), F32).at[:bsz].set(c).at[bsz].set(c_ctx)
    mod = _ada_mod(rows, ada_w, ada_b)
    shift, scale, gate = mod[:, :, :d], mod[:, :, d:2 * d], mod[:, :, 2 * d:]
    mult = norm_g[:, None, :] * (1.0 + scale)
    lat = lambda a, i: a[i, :bsz].reshape(bsz, 1, d)
    con = lambda a, i: a[i, bsz:bsz + 1].reshape(1, 1, d)

    tm_l = min(PROJ_ROW_TILE, t)
    tm_c = min(PROJ_ROW_TILE, lc)
    tm_stream = min(STREAM_ROW_TILE, t)

    w_in0 = rw_in[0].astype(BF16)
    seg64 = _segment_matrix(RW_HEAD)
    rwkv_in = functools.partial(_rwkv_in, w_in=w_in0, mu=rw_mu[0], w0=rw_w0[0], w2=rw_w2[0], a0=rw_a0[0],
                                a2=rw_a2[0], k_k=rw_kk[0], k_a=rw_ka[0], r_k=rw_rk[0].reshape(-1), seg=seg64)
    r_l, v_l, kk_l, g_l, bonus_l, lw_l, kd_l, bb_l = rwkv_in(x2, lat(mult, 0), lat(shift, 0), seq_len=t,
                                                             rows_per_group=t, tm=tm_l)
    r_c, v_c, kk_c, g_c, bonus_c, lw_c, kd_c, bb_c = rwkv_in(xc2, con(mult, 0), con(shift, 0), seq_len=lc,
                                                             rows_per_group=bsz * lc, tm=tm_c)

    s0 = jnp.zeros((bsz, 2, cw // LANES, LANES, LANES), F32)
    b3 = lambda a, n: a.reshape(bsz, n, cw)
    b4 = lambda a, n: a.reshape(2, bsz, n, cw)
    ocf, ocb, s_c = _wkv_scan(b3(r_c, lc), b3(v_c, lc), b3(kk_c, lc), b4(lw_c, lc), b4(kd_c, lc),
                              b4(bb_c, lc), s0)
    olf, olb, _ = _wkv_scan(b3(r_l, t), b3(v_l, t), b3(kk_l, t), b4(lw_l, t), b4(kd_l, t), b4(bb_l, t), s_c)

    w_out0 = rw_out[0].astype(BF16)
    readout = functools.partial(_rwkv_readout, gn_g=rw_gn_g[0], gn_b=rw_gn_b[0], seg=seg64, w_out=w_out0)
    x1 = readout(olf.reshape(bsz * t, cw), olb.reshape(bsz * t, cw), bonus_l, g_l, x2,
                 lat(gate, 0), rows_per_group=t, tm=tm_stream)
    xc1 = readout(ocf.reshape(bsz * lc, cw), ocb.reshape(bsz * lc, cw), bonus_c, g_c, xc2,
                  con(gate, 0), rows_per_group=bsz * lc, tm=tm_c)

    qw = DA_HEADS * 2 * DA_HEAD
    w_in1 = da_in[0].astype(BF16)
    seg_da = _segment_matrix(DA_HEAD)
    cos, sin_signed = _rope_tables(t // GRID_W)
    q_b, k_b, vt_b, g2_l = _attn_in(x1, lat(mult, 1), lat(shift, 1), w_in1, True, cos, sin_signed, da_qn[0],
                                    da_kn[0], seg_da, qw, seq_len=t, rows_per_group=t, tm=tm_l)
    kc_b, vtc_b = _attn_in(xc1, con(mult, 1), con(shift, 1), w_in1[:, qw:3 * qw], False, None, None, None,
                           da_kn[0], seg_da, qw, seq_len=lc, rows_per_group=bsz * lc, tm=tm_c)
    lam_init = 0.8 - 0.6 * math.exp(-0.3 * 1)
    attn = functools.partial(_diff_attn, lam_vecs=da_lam[0], subln=da_subln[0], lam_init=lam_init,
                             tq=min(ATTN_Q_TILE, t), tk=min(ATTN_KEY_CHUNK, t))
    score_bound = DA_HEAD ** 0.5 * jnp.max(jnp.abs(da_qn[0])) * jnp.max(jnp.abs(da_kn[0]))
    y_att = lax.cond(score_bound < MAX_UNSHIFTED_SCORE,
                     functools.partial(attn, bounded=True), functools.partial(attn, bounded=False),
                     q_b.reshape(bsz, t, qw), kc_b.reshape(bsz, lc, qw), k_b.reshape(bsz, t, qw), vtc_b, vt_b)
    out = _gated_out(y_att.reshape(bsz * t, qw), g2_l, x1, lat(gate, 1), da_out[0].astype(BF16),
                     rows_per_group=t, tm=tm_stream)
    return out.reshape(bsz, t, d)
```

```python
import functools
import math

import jax
import jax.numpy as jnp
from jax import lax
from jax.experimental import pallas as pl
from jax.experimental.pallas import tpu as pltpu

F32 = jnp.float32
BF16 = jnp.bfloat16

GRID_W = 64
RMS_EPS = 1e-6
RW_HEAD = 64
W_LORA = 64
A_LORA = 64
GN_EPS = 64e-5
DA_HEADS = 8
DA_HEAD = 64
DA_VHEAD = 2 * DA_HEAD
SUBLN_EPS = 1e-5
ROPE_THETA = 10000.0
ROPE_FREQS = DA_HEAD // 4

LANES = 128
SUBLANES = 8
SCAN_CHUNK = 64
SCAN_CHUNKS_PER_STEP = 2
VMEM_LIMIT = 56 * 1024 * 1024
MAX_UNSHIFTED_SCORE = 40.0
PROJ_ROW_TILE = 256
STREAM_ROW_TILE = 512
ATTN_Q_TILE = 1024
ATTN_KEY_CHUNK = 1024


def _cparams(sem):
    return pltpu.CompilerParams(dimension_semantics=sem, vmem_limit_bytes=VMEM_LIMIT)


def _dot(a, b):
    return jnp.dot(a, b, preferred_element_type=F32)


def _dot_nt(a, b):
    return lax.dot_general(a, b, (((1,), (1,)), ((), ())), preferred_element_type=F32)


def _dot_tn(a, b):
    return lax.dot_general(a, b, (((0,), (0,)), ((), ())), preferred_element_type=F32)


def _split2(x):
    hi = x.astype(BF16)
    lo = (x - hi.astype(F32)).astype(BF16)
    return hi, lo


def _seg_sum(x, seg2):
    out = []
    for j in range(x.shape[-1] // LANES):
        hi, lo = _split2(x[:, j * LANES:(j + 1) * LANES])
        out.append(_dot(jnp.concatenate([hi, lo], axis=1), seg2))
    return jnp.concatenate(out, axis=1)


def _sigmoid(x):
    return 1.0 / (1.0 + jnp.exp(-x))


def _silu(x):
    return x * _sigmoid(x)


def _ada_kernel(c_ref, w_ref, b_ref, o_ref):
    s = _silu(c_ref[...])
    o_ref[...] = jnp.dot(s, w_ref[...], preferred_element_type=F32,
                         precision=lax.Precision.HIGHEST) + b_ref[...]


def _ada_mod(rows, ada_w, ada_b):
    depth, d, n = ada_w.shape
    tn = 768
    return pl.pallas_call(
        _ada_kernel,
        grid=(depth, n // tn),
        in_specs=[pl.BlockSpec((SUBLANES, d), lambda i, j: (0, 0)),
                  pl.BlockSpec((None, d, tn), lambda i, j: (i, 0, j)),
                  pl.BlockSpec((None, 1, tn), lambda i, j: (i, 0, j))],
        out_specs=pl.BlockSpec((None, SUBLANES, tn), lambda i, j: (i, 0, j)),
        out_shape=jax.ShapeDtypeStruct((depth, SUBLANES, n), F32),
        compiler_params=_cparams(("arbitrary", "arbitrary")),
        name="ada_mod",
    )(rows, ada_w, ada_b.reshape(depth, 1, n))


def _normmod(x, mult, shift):
    y = x * lax.rsqrt(jnp.mean(x * x, axis=-1, keepdims=True) + RMS_EPS)
    return (y * mult + shift).astype(BF16)


def _token_shift(u, prev8, next8, mu, first, last):
    tm = u.shape[0]
    row8 = lax.broadcasted_iota(jnp.int32, (SUBLANES, u.shape[1]), 0)
    prow = jnp.where(first, 0.0, prev8[SUBLANES - 1:SUBLANES, :])
    nrow = jnp.where(last, 0.0, next8[0:1, :])
    nbr = pltpu.roll(u, 1, 0) + pltpu.roll(u, tm - 1, 0)
    top = nbr[:SUBLANES] + jnp.where(row8 == 0, prow - u[tm - 1:tm], 0.0)
    bot = nbr[tm - SUBLANES:] + jnp.where(row8 == SUBLANES - 1, nrow - u[0:1], 0.0)
    nbr = jnp.concatenate([top, nbr[SUBLANES:tm - SUBLANES], bot], axis=0)
    return (1.0 - mu) * u + (0.5 * mu) * nbr


def _rwkv_in_kernel(x_ref, xp_ref, xn_ref, mult_ref, shift_ref, w_ref,
                    mua_ref, mub_ref, w2_ref, w0_ref, a2_ref, a0_ref, kk_ref, ka_ref, rk_ref, seg_ref,
                    r_ref, v_ref, kkn_ref, g_ref, bonus_ref, lw_ref, kd_ref, bb_ref, *, tiles_per_seq, c):
    i = pl.program_id(0)
    first = (i % tiles_per_seq) == 0
    last = (i % tiles_per_seq) == tiles_per_seq - 1
    tm = x_ref.shape[0]
    lo, hi = SUBLANES, SUBLANES + tm
    x_ext = jnp.concatenate([xp_ref[...], x_ref[...], xn_ref[...]], axis=0)
    h = _normmod(x_ext, mult_ref[...], shift_ref[...])

    def project(c0, c1):
        return _dot(h, w_ref[:, c0:c1])

    def shifted(p, mu):
        return _token_shift(p[lo:hi], p[:lo], p[hi:], mu, first, last)

    p_lora = project(4 * c, w_ref.shape[1])
    p_k = project(c, 2 * c)
    mb = shifted(p_lora, mub_ref[...])
    nl = 2 * W_LORA
    w_raw = w0_ref[...] + _dot(jnp.tanh(mb[:, :nl]).astype(BF16), w2_ref[...])
    a = _sigmoid(a0_ref[...] + _dot(mb[:, nl:].astype(BF16), a2_ref[...]))
    logw = -math.exp(-0.5) / (1.0 + jnp.exp(-w_raw))
    for d in range(2):
        lw_ref[d] = logw[:, d * c:(d + 1) * c]

    p_r = project(0, c)
    k = shifted(p_k, mua_ref[:, c:2 * c])
    kk = k * kk_ref[...]
    kkn = kk * lax.rsqrt(_seg_sum(kk * kk, seg_ref[...]) + 1e-12)
    kkn_ref[...] = kkn
    kka = k * ka_ref[...]
    k_rest = k - kka
    kd = [k_rest + kka * a[:, d * c:(d + 1) * c] for d in range(2)]
    for d in range(2):
        kd_ref[d] = kd[d]
        bb_ref[d] = kkn * a[:, d * c:(d + 1) * c]

    p_v = project(2 * c, 3 * c)
    r = shifted(p_r, mua_ref[:, :c])
    r_ref[...] = r
    p_g = project(3 * c, 4 * c)
    v = shifted(p_v, mua_ref[:, 2 * c:3 * c])
    v_ref[...] = v
    bonus_ref[...] = _seg_sum(r * (kd[0] + kd[1]) * rk_ref[...], seg_ref[...]) * v
    g_ref[...] = p_g[lo:hi]


def _rwkv_in(x2d, mult, shift, w_in, seq_len, rows_per_group, mu, w0, w2, a0, a2, k_k, k_a, r_k, seg, tm):
    r_rows, dm = x2d.shape
    ncols = w_in.shape[1]
    c = k_k.shape[-1]
    nb = ncols - 4 * c
    assert nb == 2 * (W_LORA + A_LORA) and seq_len % tm == 0 and rows_per_group % tm == 0
    tps = seq_len // tm
    tpg = rows_per_group // tm
    t8 = tm // SUBLANES
    nblk8 = r_rows // SUBLANES
    mua = mu[:3 * c].reshape(1, 3 * c)
    mub = mu[3 * c:].reshape(1, nb)
    zw = jnp.zeros((W_LORA, c), F32)
    w2cat = jnp.concatenate([jnp.concatenate([w2[0], zw], 1), jnp.concatenate([zw, w2[1]], 1)], 0).astype(BF16)
    a2cat = jnp.concatenate([jnp.concatenate([a2[0], zw], 1), jnp.concatenate([zw, a2[1]], 1)], 0).astype(BF16)
    w0cat = w0.reshape(1, 2 * c)
    a0cat = a0.reshape(1, 2 * c)
    full = lambda shape: pl.BlockSpec(shape, lambda i: (0,) * len(shape))
    prev_idx = lambda i: jnp.maximum(i * t8 - 1, 0)
    next_idx = lambda i: jnp.minimum((i + 1) * t8, nblk8 - 1)
    row_out = pl.BlockSpec((tm, c), lambda i: (i, 0))
    dir_out = pl.BlockSpec((2, tm, c), lambda i: (0, i, 0))
    group = pl.BlockSpec((None, 1, dm), lambda i: (i // tpg, 0, 0))
    kern = functools.partial(_rwkv_in_kernel, tiles_per_seq=tps, c=c)
    return pl.pallas_call(
        kern,
        grid=(r_rows // tm,),
        in_specs=[pl.BlockSpec((tm, dm), lambda i: (i, 0)),
                  pl.BlockSpec((SUBLANES, dm), lambda i: (prev_idx(i), 0)),
                  pl.BlockSpec((SUBLANES, dm), lambda i: (next_idx(i), 0)),
                  group, group, full((dm, ncols)),
                  full((1, 3 * c)), full((1, nb)),
                  full((2 * W_LORA, 2 * c)), full((1, 2 * c)),
                  full((2 * A_LORA, 2 * c)), full((1, 2 * c)),
                  full((1, c)), full((1, c)), full((1, c)), full((2 * LANES, LANES))],
        out_specs=[row_out] * 5 + [dir_out] * 3,
        out_shape=[jax.ShapeDtypeStruct((r_rows, c), F32)] * 5
                  + [jax.ShapeDtypeStruct((2, r_rows, c), F32)] * 3,
        compiler_params=_cparams(("arbitrary",)),
        name="rwkv_in",
    )(x2d, x2d, x2d, mult, shift, w_in, mua, mub, w2cat, w0cat, a2cat, a0cat,
      k_k.reshape(1, c), k_a.reshape(1, c), r_k.reshape(1, c), seg)


def _head_pair_stack(xp, lane_a):
    return jnp.concatenate([jnp.where(lane_a, xp, 0.0), jnp.where(lane_a, 0.0, xp)], axis=0)


def _wkv_chunk_operands(r, v, kk, lw, kd, bb, reverse, L):
    ti = lax.broadcasted_iota(jnp.int32, (L, L), 0)
    si = lax.broadcasted_iota(jnp.int32, (L, L), 1)
    tri = ((ti <= si) if reverse else (ti >= si)).astype(BF16)
    hi = lw.astype(BF16)
    r1 = lw - hi.astype(F32)
    mid = r1.astype(BF16)
    lo = (r1 - mid.astype(F32)).astype(BF16)
    cum = _dot(tri, hi) + _dot(tri, mid) + _dot(tri, lo)
    tot = jnp.sum(lw, axis=0, keepdims=True)
    ginv = jnp.exp(-cum)
    gend = jnp.exp(tot) * ginv
    return dict(kt=kk * jnp.exp(cum - lw), rt=r * jnp.exp(cum), bt=bb * ginv,
                kdt=kd * ginv, khat=kd * gend, bhat=bb * gend, v=v, tot=tot)


def _wkv_scan_kernel(rf_ref, vf_ref, kkf_ref, lwf_ref, kdf_ref, bbf_ref,
                     rb_ref, vb_ref, kkb_ref, lwb_ref, kdb_ref, bbb_ref, s0_ref,
                     of_ref, ob_ref, sfin_ref, h_scr, *, chunk, n_pairs):
    L = chunk
    ci = pl.program_id(0)
    nc = pl.num_programs(0)
    nb = rf_ref.shape[0]

    @pl.when(ci == 0)
    def _():
        h_scr[...] = s0_ref[...]

    in_refs = ((rf_ref, vf_ref, kkf_ref, lwf_ref, kdf_ref, bbf_ref),
               (rb_ref, vb_ref, kkb_ref, lwb_ref, kdb_ref, bbb_ref))
    o_refs = (of_ref, ob_ref)
    cps = rf_ref.shape[1] // L
    rows_of = lambda d, j: slice((cps - 1 - j if d else j) * L, (cps - j if d else j + 1) * L)
    ops = {(b, d, j): _wkv_chunk_operands(*[ref[b, rows_of(d, j), :] for ref in in_refs[d]], d == 1, L)
           for b in range(nb) for d in range(2) for j in range(cps)}

    lane_a = lax.broadcasted_iota(jnp.int32, (L, LANES), 1) < RW_HEAD
    dts = (lax.broadcasted_iota(jnp.int32, (L, LANES), 0)
           - (lax.broadcasted_iota(jnp.int32, (L, LANES), 1) & (L - 1)))
    strict = (dts > 0, dts < 0)
    incl = (dts >= 0, dts <= 0)
    eye = (dts == 0).astype(F32)
    same_head = ((lax.broadcasted_iota(jnp.int32, (LANES, LANES), 0) < RW_HEAD)
                 == (lax.broadcasted_iota(jnp.int32, (LANES, LANES), 1) < RW_HEAD))
    bd = lambda x: _head_pair_stack(x, lane_a).astype(BF16)
    lanes_of = lambda p: slice(p * LANES, (p + 1) * LANES)
    cat = lambda name, b, d, j, p: ops[b, d, j][name][:, lanes_of(p)]
    msk = lambda m, x: jnp.where(m, x, 0.0)

    chains = [(b, d, j, p) for p in range(n_pairs) for j in range(cps) for b in range(nb) for d in range(2)]
    kt_c = {ch: cat("kt", *ch).astype(BF16) for ch in chains}
    rt_c = {ch: cat("rt", *ch).astype(BF16) for ch in chains}
    v_c = {ch: cat("v", *ch) for ch in chains}
    v_s = {ch: bd(v_c[ch]) for ch in chains}
    sc = {ch: _dot_nt(jnp.concatenate([kt_c[ch], rt_c[ch]], axis=0),
                      jnp.concatenate([bd(cat("bt", *ch)), bd(cat("kdt", *ch))], axis=0)) for ch in chains}
    pw = {ch: -msk(strict[ch[1]], sc[ch][:L, :LANES]) for ch in chains}
    a_k = {ch: msk(strict[ch[1]], sc[ch][:L, LANES:]).astype(BF16) for ch in chains}
    pkb = {ch: jnp.concatenate([msk(incl[ch[1]], sc[ch][L:, LANES:]), -msk(incl[ch[1]], sc[ch][L:, :LANES])],
                               axis=1).astype(BF16) for ch in chains}
    tinv = {ch: eye + pw[ch] for ch in chains}
    pw = {ch: _dot(pw[ch].astype(BF16), bd(pw[ch])) for ch in chains}
    for step in range(1, int(math.log2(L))):
        if step < int(math.log2(L)) - 1:
            w = {ch: _dot(jnp.concatenate([tinv[ch], pw[ch]], axis=0).astype(BF16), bd(pw[ch])) for ch in chains}
            tinv = {ch: tinv[ch] + w[ch][:L] for ch in chains}
            pw = {ch: w[ch][L:] for ch in chains}
        else:
            tinv = {ch: (tinv[ch] + _dot(tinv[ch].astype(BF16), bd(pw[ch]))).astype(BF16) for ch in chains}

    h = {(b, d, p): h_scr[b, d, p] for b in range(nb) for d in range(2) for p in range(n_pairs)}
    for j in range(cps):
        now = [ch for ch in chains if ch[2] == j]
        h_b = {ch: h[ch[0], ch[1], ch[3]].astype(BF16) for ch in now}
        u = {ch: _dot(jnp.concatenate([kt_c[ch], a_k[ch]], axis=1), jnp.concatenate([h_b[ch], v_s[ch]], axis=0))
             for ch in now}
        u = {ch: _dot(tinv[ch], bd(u[ch])) for ch in now}
        for ch in now:
            b, d, _, p = ch
            lhs = jnp.concatenate([rt_c[ch], pkb[ch]], axis=1)
            rhs = jnp.concatenate([h_b[ch], v_s[ch], bd(u[ch])], axis=0)
            o_refs[d][b, rows_of(d, j), lanes_of(p)] = _dot(lhs, rhs)
        for ch in now:
            b, d, _, p = ch
            kb = jnp.concatenate([cat("khat", *ch), -cat("bhat", *ch)], axis=0).astype(BF16)
            vu = jnp.concatenate([v_c[ch], u[ch]], axis=0).astype(BF16)
            tot_col = jnp.broadcast_to(ops[b, d, j]["tot"][:, lanes_of(p)], (LANES, LANES)).T
            h[b, d, p] = jnp.exp(tot_col) * h[b, d, p] + jnp.where(same_head, _dot_tn(kb, vu), 0.0)
    for (b, d, p), val in h.items():
        h_scr[b, d, p] = val

    @pl.when(ci == nc - 1)
    def _():
        sfin_ref[...] = h_scr[...]


def _wkv_scan(r, v, kkn, logw, kd, bb, s0, chunk=SCAN_CHUNK):
    b, t, c = r.shape
    n_pairs = c // LANES
    rows = chunk * SCAN_CHUNKS_PER_STEP
    assert t % rows == 0 and 2 * RW_HEAD == LANES
    nc = t // rows
    fwd = lambda ci: ci
    bwd = lambda ci: nc - 1 - ci
    shared = lambda cmap: pl.BlockSpec((b, rows, c), lambda ci: (0, cmap(ci), 0))
    perdir = lambda d, cmap: pl.BlockSpec((None, b, rows, c), lambda ci: (d, 0, cmap(ci), 0))
    state = pl.BlockSpec((b, 2, n_pairs, LANES, LANES), lambda ci: (0, 0, 0, 0, 0))
    kern = functools.partial(_wkv_scan_kernel, chunk=chunk, n_pairs=n_pairs)
    return pl.pallas_call(
        kern,
        grid=(nc,),
        in_specs=[shared(fwd)] * 3 + [perdir(0, fwd)] * 3 + [shared(bwd)] * 3 + [perdir(1, bwd)] * 3 + [state],
        out_specs=[shared(fwd), shared(bwd), state],
        out_shape=[jax.ShapeDtypeStruct((b, t, c), F32), jax.ShapeDtypeStruct((b, t, c), F32),
                   jax.ShapeDtypeStruct((b, 2, n_pairs, LANES, LANES), F32)],
        scratch_shapes=[pltpu.VMEM((b, 2, n_pairs, LANES, LANES), F32)],
        compiler_params=_cparams(("arbitrary",)),
        name="wkv_scan",
    )(r, v, kkn, logw, kd, bb, r, v, kkn, logw, kd, bb, s0)


def _rwkv_readout_kernel(of_ref, ob_ref, bonus_ref, g_ref, x_ref, gate_ref,
                         gng_ref, gnb_ref, seg_ref, w_ref, out_ref):
    seg = seg_ref[...]
    inv_n = 1.0 / RW_HEAD
    o = of_ref[...] + ob_ref[...]
    mean = _seg_sum(o, seg) * inv_n
    dlt = o - mean
    var = _seg_sum(dlt * dlt, seg) * inv_n
    y = dlt * lax.rsqrt(var + GN_EPS) * gng_ref[...] + gnb_ref[...]
    y = (y + bonus_ref[...]) * _silu(g_ref[...])
    out_ref[...] = x_ref[...] + gate_ref[...] * _dot(y.astype(BF16), w_ref[...])


def _rwkv_readout(o_f, o_b, bonus, g, x2d, gate, gn_g, gn_b, seg, w_out, rows_per_group, tm):
    r_rows, c = g.shape
    d = x2d.shape[1]
    tpg = rows_per_group // tm
    full = lambda shape: pl.BlockSpec(shape, lambda i: (0,) * len(shape))
    row = lambda n: pl.BlockSpec((tm, n), lambda i: (i, 0))
    return pl.pallas_call(
        _rwkv_readout_kernel,
        grid=(r_rows // tm,),
        in_specs=[row(c), row(c), row(c), row(c), row(d),
                  pl.BlockSpec((None, 1, d), lambda i: (i // tpg, 0, 0)),
                  full((1, c)), full((1, c)), full((2 * LANES, LANES)), full((c, d))],
        out_specs=row(d),
        out_shape=jax.ShapeDtypeStruct((r_rows, d), F32),
        compiler_params=_cparams(("arbitrary",)),
        name="rwkv_readout",
    )(o_f, o_b, bonus, g, x2d, gate, gn_g.reshape(1, c), gn_b.reshape(1, c), seg, w_out)


def _head_norm(x, gain, seg):
    ms = _seg_sum(x * x, seg) * (1.0 / DA_HEAD)
    return x * lax.rsqrt(ms + RMS_EPS) * gain


def _rope(x, cos, sin_signed):
    n = x.shape[-1]
    lane = lax.broadcasted_iota(jnp.int32, x.shape, 1)
    half0 = (lane & (2 * ROPE_FREQS - 1)) < ROPE_FREQS
    partner = jnp.where(half0, pltpu.roll(x, n - ROPE_FREQS, 1), pltpu.roll(x, ROPE_FREQS, 1))
    reps = n // cos.shape[-1]
    wide = lambda tab: jnp.concatenate([tab] * reps, axis=1)
    return x * wide(cos) + partner * wide(sin_signed)


def _attn_in_kernel(*refs, latent, q_scale):
    it = iter(refs)
    x_ref, mult_ref, shift_ref, w_ref = next(it), next(it), next(it), next(it)
    cos_ref = next(it) if latent else None
    sin_ref = next(it) if latent else None
    qn_ref = next(it) if latent else None
    kn_ref, seg_ref = next(it), next(it)
    qo_ref = next(it) if latent else None
    ko_ref, vo_ref = next(it), next(it)
    go_ref = next(it) if latent else None
    h = _normmod(x_ref[...], mult_ref[...], shift_ref[...])
    w = ko_ref.shape[-1]
    cols = lambda j0, j1: _dot(h, w_ref[:, j0 * w:j1 * w])
    seg = seg_ref[...]
    if latent:
        p_k = cols(1, 2)
        p_q = cols(0, 1)
        k = _rope(_head_norm(p_k, kn_ref[...], seg), cos_ref[...], sin_ref[...])
        ko_ref[...] = k.astype(BF16)
        p_vg = cols(2, 4)
        q = _rope(_head_norm(p_q, qn_ref[...], seg), cos_ref[...], sin_ref[...])
        qo_ref[...] = (q * q_scale).astype(BF16)
        vo_ref[...] = p_vg[:, :w].T.astype(BF16)
        go_ref[...] = _silu(p_vg[:, w:]).astype(BF16)
    else:
        p_kv = cols(0, 2)
        ko_ref[...] = _head_norm(p_kv[:, :w], kn_ref[...], seg).astype(BF16)
        vo_ref[...] = p_kv[:, w:].T.astype(BF16)


def _attn_in(x2d, mult, shift, w_in, latent, cos, sin_signed, qn, kn, seg, w, seq_len, rows_per_group, tm):
    r_rows, dm = x2d.shape
    tps = seq_len // tm
    tpg = rows_per_group // tm
    full = lambda shape: pl.BlockSpec(shape, lambda i: (0,) * len(shape))
    group = pl.BlockSpec((None, 1, dm), lambda i: (i // tpg, 0, 0))
    tab = pl.BlockSpec((tm, LANES), lambda i: (i % tps, 0))
    tile_gain = lambda g: jnp.tile(g, w // g.shape[0]).reshape(1, w)
    in_specs = [pl.BlockSpec((tm, dm), lambda i: (i, 0)), group, group, full(w_in.shape)]
    args = [x2d, mult, shift, w_in]
    if latent:
        in_specs += [tab, tab, full((1, w))]
        args += [cos, sin_signed, tile_gain(qn)]
    in_specs += [full((1, w)), full((2 * LANES, LANES))]
    args += [tile_gain(kn), seg]
    rows_bf = (pl.BlockSpec((tm, w), lambda i: (i, 0)), jax.ShapeDtypeStruct((r_rows, w), BF16))
    v_t = (pl.BlockSpec((None, w, tm), lambda i: (i // tps, 0, i % tps)),
           jax.ShapeDtypeStruct((r_rows // seq_len, w, seq_len), BF16))
    outs = [rows_bf, rows_bf, v_t, rows_bf] if latent else [rows_bf, v_t]
    kern = functools.partial(_attn_in_kernel, latent=latent, q_scale=DA_HEAD ** -0.5)
    return pl.pallas_call(
        kern,
        grid=(r_rows // tm,),
        in_specs=in_specs,
        out_specs=[o[0] for o in outs],
        out_shape=[o[1] for o in outs],
        compiler_params=_cparams(("arbitrary",)),
        name="attn_in_lat" if latent else "attn_in_ctx",
    )(*args)


def _diff_attn_kernel(q_ref, kc_ref, kl_ref, vtc_ref, vtl_ref, lam_ref, sub_ref, o_ref, m_scr, l_scr, acc_scr,
                      *, lam_init, tk, bounded):
    q = q_ref[...]
    lane = lax.broadcasted_iota(jnp.int32, q.shape, 1)
    zero = jnp.zeros_like(q)
    qm = (jnp.where(lane < DA_HEAD, q, zero), jnp.where(lane < DA_HEAD, zero, q))
    if not bounded:
        m_scr[...] = jnp.full(m_scr.shape, -jnp.inf, F32)
        l_scr[...] = jnp.zeros(l_scr.shape, F32)
        acc_scr[...] = jnp.zeros(acc_scr.shape, F32)

    def accumulate(k, vt):
        for c in range(2):
            s = _dot_nt(k, qm[c])
            m_prev = m_scr[c]
            m_new = jnp.maximum(m_prev, jnp.max(s, axis=0, keepdims=True))
            alpha = jnp.exp(m_prev - m_new)
            p = jnp.exp(s - m_new)
            l_scr[c] = alpha * l_scr[c] + jnp.sum(p, axis=0, keepdims=True)
            acc_scr[c] = alpha * acc_scr[c] + _dot(vt, p.astype(BF16))
            m_scr[c] = m_new

    def body(j, carry):
        off = pl.multiple_of(j * tk, tk)
        accumulate(kl_ref[pl.ds(off, tk), :], vtl_ref[:, pl.ds(off, tk)])
        return carry

    if bounded:
        n_lat = kl_ref.shape[0] // tk
        keys = [kc_ref] + [kl_ref.at[pl.ds(j * tk, tk), :] for j in range(n_lat)]
        vals = [vtc_ref] + [vtl_ref.at[:, pl.ds(j * tk, tk)] for j in range(n_lat)]
        probs = lambda c, j: jnp.exp(_dot_nt(keys[j][...], qm[c]))

        def add_values(c, j, p):
            l_new = jnp.sum(p, axis=0, keepdims=True)
            acc_new = _dot(vals[j][...], p.astype(BF16))
            l_scr[c] = l_new if j == 0 else l_scr[c] + l_new
            acc_scr[c] = acc_new if j == 0 else acc_scr[c] + acc_new

        p = [probs(0, 0), probs(1, 0)]
        for j in range(len(keys)):
            more = j + 1 < len(keys)
            add_values(0, j, p[0])
            if more:
                p[0] = probs(0, j + 1)
            add_values(1, j, p[1])
            if more:
                p[1] = probs(1, j + 1)
    else:
        accumulate(kc_ref[...], vtc_ref[...])
        lax.fori_loop(0, kl_ref.shape[0] // tk, body, 0)

    lv = lam_ref[...]
    lam = (jnp.exp(jnp.sum(lv[0:1] * lv[1:2], axis=-1, keepdims=True))
           - jnp.exp(jnp.sum(lv[2:3] * lv[3:4], axis=-1, keepdims=True)) + lam_init)
    o = acc_scr[0] * (1.0 / l_scr[0]) - acc_scr[1] * (lam / l_scr[1])
    y = o * lax.rsqrt(jnp.mean(o * o, axis=0, keepdims=True) + SUBLN_EPS)
    o_ref[...] = (y * (sub_ref[...] * (1.0 - lam_init))).T.astype(BF16)


def _diff_attn(q, k_ctx, k_lat, vt_ctx, vt_lat, lam_vecs, subln, lam_init, tq, tk, bounded):
    b, t, w = q.shape
    lc = k_ctx.shape[1]
    nh = w // DA_VHEAD
    assert t % tq == 0 and t % tk == 0
    kern = functools.partial(_diff_attn_kernel, lam_init=lam_init, tk=tk, bounded=bounded)
    return pl.pallas_call(
        kern,
        grid=(b, nh, t // tq),
        in_specs=[pl.BlockSpec((None, tq, DA_VHEAD), lambda bi, h, i: (bi, i, h)),
                  pl.BlockSpec((None, lc, DA_VHEAD), lambda bi, h, i: (bi, 0, h)),
                  pl.BlockSpec((None, t, DA_VHEAD), lambda bi, h, i: (bi, 0, h)),
                  pl.BlockSpec((None, DA_VHEAD, lc), lambda bi, h, i: (bi, h, 0)),
                  pl.BlockSpec((None, DA_VHEAD, t), lambda bi, h, i: (bi, h, 0)),
                  pl.BlockSpec((4, DA_HEAD), lambda bi, h, i: (0, 0)),
                  pl.BlockSpec((DA_VHEAD, 1), lambda bi, h, i: (0, 0))],
        out_specs=pl.BlockSpec((None, tq, DA_VHEAD), lambda bi, h, i: (bi, i, h)),
        out_shape=jax.ShapeDtypeStruct((b, t, w), BF16),
        scratch_shapes=[pltpu.VMEM((2, 1, tq), F32), pltpu.VMEM((2, 1, tq), F32),
                        pltpu.VMEM((2, DA_VHEAD, tq), F32)],
        compiler_params=_cparams(("arbitrary", "arbitrary", "arbitrary")),
        name="diff_attn_bounded" if bounded else "diff_attn",
    )(q, k_ctx, k_lat, vt_ctx, vt_lat, lam_vecs, subln.reshape(DA_VHEAD, 1))


def _gated_out_kernel(y_ref, g_ref, x_ref, gate_ref, w_ref, out_ref):
    y = y_ref[...].astype(F32) * g_ref[...].astype(F32)
    out_ref[...] = x_ref[...] + gate_ref[...] * _dot(y.astype(BF16), w_ref[...])


def _gated_out(y, g, x2d, gate, w_out, rows_per_group, tm):
    r_rows, w = y.shape
    d = x2d.shape[1]
    tpg = rows_per_group // tm
    row = lambda n: pl.BlockSpec((tm, n), lambda i: (i, 0))
    return pl.pallas_call(
        _gated_out_kernel,
        grid=(r_rows // tm,),
        in_specs=[row(w), row(w), row(d),
                  pl.BlockSpec((None, 1, d), lambda i: (i // tpg, 0, 0)),
                  pl.BlockSpec((w, d), lambda i: (0, 0))],
        out_specs=row(d),
        out_shape=jax.ShapeDtypeStruct((r_rows, d), F32),
        compiler_params=_cparams(("arbitrary",)),
        name="gated_out",
    )(y, g, x2d, gate, w_out)


def _segment_matrix(width):
    i = (lax.broadcasted_iota(jnp.int32, (2 * LANES, LANES), 0) % LANES) // width
    j = lax.broadcasted_iota(jnp.int32, (2 * LANES, LANES), 1) // width
    return (i == j).astype(BF16)


def _rope_tables(rows):
    row_ids = jnp.repeat(jnp.arange(rows), GRID_W).astype(F32)
    col_ids = jnp.tile(jnp.arange(GRID_W), rows).astype(F32)
    inv_freq = ROPE_THETA ** (-jnp.arange(ROPE_FREQS, dtype=F32) / ROPE_FREQS)
    ang_r = row_ids[:, None] * inv_freq
    ang_c = col_ids[:, None] * inv_freq
    cr, sr, cc, sc = jnp.cos(ang_r), jnp.sin(ang_r), jnp.cos(ang_c), jnp.sin(ang_c)
    cos = jnp.concatenate([cr, cr, cc, cc], axis=-1)
    sin_signed = jnp.concatenate([-sr, sr, -sc, sc], axis=-1)
    return jnp.tile(cos, (1, 2)), jnp.tile(sin_signed, (1, 2))


def kernel(x, c, ctx, c_ctx, ada_w, ada_b, norm_g, rw_in, rw_mu, rw_w0, rw_w2, rw_a0, rw_a2, rw_kk, rw_ka, rw_rk, rw_gn_g, rw_gn_b, rw_out, da_in, da_qn, da_kn, da_lam, da_subln, da_out):
    bsz, t, d = x.shape
    lc = ctx.shape[1]
    depth = ada_w.shape[0]
    assert depth == 2 and bsz + 1 <= SUBLANES
    cw = rw_kk.shape[-1]
    x2 = x.reshape(bsz * t, d)
    xc2 = ctx.reshape(bsz * lc, d)

    rows = jnp.zeros((SUBLANES, d), F32).at[:bsz].set(c).at[bsz].set(c_ctx)
    mod = _ada_mod(rows, ada_w, ada_b)
    shift, scale, gate = mod[:, :, :d], mod[:, :, d:2 * d], mod[:, :, 2 * d:]
    mult = norm_g[:, None, :] * (1.0 + scale)
    lat = lambda a, i: a[i, :bsz].reshape(bsz, 1, d)
    con = lambda a, i: a[i, bsz:bsz + 1].reshape(1, 1, d)

    tm_l = min(PROJ_ROW_TILE, t)
    tm_c = min(PROJ_ROW_TILE, lc)
    tm_stream = min(STREAM_ROW_TILE, t)

    w_in0 = rw_in[0].astype(BF16)
    seg64 = _segment_matrix(RW_HEAD)
    rwkv_in = functools.partial(_rwkv_in, w_in=w_in0, mu=rw_mu[0], w0=rw_w0[0], w2=rw_w2[0], a0=rw_a0[0],
                                a2=rw_a2[0], k_k=rw_kk[0], k_a=rw_ka[0], r_k=rw_rk[0].reshape(-1), seg=seg64)
    r_l, v_l, kk_l, g_l, bonus_l, lw_l, kd_l, bb_l = rwkv_in(x2, lat(mult, 0), lat(shift, 0), seq_len=t,
                                                             rows_per_group=t, tm=tm_l)
    r_c, v_c, kk_c, g_c, bonus_c, lw_c, kd_c, bb_c = rwkv_in(xc2, con(mult, 0), con(shift, 0), seq_len=lc,
                                                             rows_per_group=bsz * lc, tm=tm_c)

    s0 = jnp.zeros((bsz, 2, cw // LANES, LANES, LANES), F32)
    b3 = lambda a, n: a.reshape(bsz, n, cw)
    b4 = lambda a, n: a.reshape(2, bsz, n, cw)
    ocf, ocb, s_c = _wkv_scan(b3(r_c, lc), b3(v_c, lc), b3(kk_c, lc), b4(lw_c, lc), b4(kd_c, lc),
                              b4(bb_c, lc), s0)
    olf, olb, _ = _wkv_scan(b3(r_l, t), b3(v_l, t), b3(kk_l, t), b4(lw_l, t), b4(kd_l, t), b4(bb_l, t), s_c)

    w_out0 = rw_out[0].astype(BF16)
    readout = functools.partial(_rwkv_readout, gn_g=rw_gn_g[0], gn_b=rw_gn_b[0], seg=seg64, w_out=w_out0)
    x1 = readout(olf.reshape(bsz * t, cw), olb.reshape(bsz * t, cw), bonus_l, g_l, x2,
                 lat(gate, 0), rows_per_group=t, tm=tm_stream)
    xc1 = readout(ocf.reshape(bsz * lc, cw), ocb.reshape(bsz * lc, cw), bonus_c, g_c, xc2,
                  con(gate, 0), rows_per_group=bsz * lc, tm=tm_c)

    qw = DA_HEADS * 2 * DA_HEAD
    w_in1 = da_in[0].astype(BF16)
    seg_da = _segment_matrix(DA_HEAD)
    cos, sin_signed = _rope_tables(t // GRID_W)
    q_b, k_b, vt_b, g2_l = _attn_in(x1, lat(mult, 1), lat(shift, 1), w_in1, True, cos, sin_signed, da_qn[0],
                                    da_kn[0], seg_da, qw, seq_len=t, rows_per_group=t, tm=tm_l)
    kc_b, vtc_b = _attn_in(xc1, con(mult, 1), con(shift, 1), w_in1[:, qw:3 * qw], False, None, None, None,
                           da_kn[0], seg_da, qw, seq_len=lc, rows_per_group=bsz * lc, tm=tm_c)
    lam_init = 0.8 - 0.6 * math.exp(-0.3 * 1)
    attn = functools.partial(_diff_attn, lam_vecs=da_lam[0], subln=da_subln[0], lam_init=lam_init,
                             tq=min(ATTN_Q_TILE, t), tk=min(ATTN_KEY_CHUNK, t))
    score_bound = DA_HEAD ** 0.5 * jnp.max(jnp.abs(da_qn[0])) * jnp.max(jnp.abs(da_kn[0]))
    y_att = lax.cond(score_bound < MAX_UNSHIFTED_SCORE,
                     functools.partial(attn, bounded=True), functools.partial(attn, bounded=False),
                     q_b.reshape(bsz, t, qw), kc_b.reshape(bsz, lc, qw), k_b.reshape(bsz, t, qw), vtc_b, vt_b)
    out = _gated_out(y_att.reshape(bsz * t, qw), g2_l, x1, lat(gate, 1), da_out[0].astype(BF16),
                     rows_per_group=t, tm=tm_stream)
    return out.reshape(bsz, t, d)
```

```python
import functools
import math

import jax
import jax.numpy as jnp
from jax import lax
from jax.experimental import pallas as pl
from jax.experimental.pallas import tpu as pltpu

F32 = jnp.float32
BF16 = jnp.bfloat16

GRID_W = 64
RMS_EPS = 1e-6
RW_HEAD = 64
W_LORA = 64
A_LORA = 64
GN_EPS = 64e-5
DA_HEADS = 8
DA_HEAD = 64
DA_VHEAD = 2 * DA_HEAD
SUBLN_EPS = 1e-5
ROPE_THETA = 10000.0
ROPE_FREQS = DA_HEAD // 4

LANES = 128
SUBLANES = 8
SCAN_CHUNK = 64
SCAN_CHUNKS_PER_STEP = 1
VMEM_LIMIT = 56 * 1024 * 1024
MAX_UNSHIFTED_SCORE = 40.0
PROJ_ROW_TILE = 256
STREAM_ROW_TILE = 512
ATTN_Q_TILE = 1024
ATTN_KEY_CHUNK = 4096


def _cparams(sem):
    return pltpu.CompilerParams(dimension_semantics=sem, vmem_limit_bytes=VMEM_LIMIT)


def _dot(a, b):
    return jnp.dot(a, b, preferred_element_type=F32)


def _dot_nt(a, b):
    return lax.dot_general(a, b, (((1,), (1,)), ((), ())), preferred_element_type=F32)


def _dot_tn(a, b):
    return lax.dot_general(a, b, (((0,), (0,)), ((), ())), preferred_element_type=F32)


def _split2(x):
    hi = x.astype(BF16)
    lo = (x - hi.astype(F32)).astype(BF16)
    return hi, lo


def _seg_sum(x, seg2):
    out = []
    for j in range(x.shape[-1] // LANES):
        hi, lo = _split2(x[:, j * LANES:(j + 1) * LANES])
        out.append(_dot(jnp.concatenate([hi, lo], axis=1), seg2))
    return jnp.concatenate(out, axis=1)


def _sigmoid(x):
    return 1.0 / (1.0 + jnp.exp(-x))


def _silu(x):
    return x * _sigmoid(x)


def _ada_kernel(c_ref, w_ref, b_ref, o_ref):
    s = _silu(c_ref[...])
    o_ref[...] = jnp.dot(s, w_ref[...], preferred_element_type=F32,
                         precision=lax.Precision.HIGHEST) + b_ref[...]


def _ada_mod(rows, ada_w, ada_b):
    depth, d, n = ada_w.shape
    tn = 768
    return pl.pallas_call(
        _ada_kernel,
        grid=(depth, n // tn),
        in_specs=[pl.BlockSpec((SUBLANES, d), lambda i, j: (0, 0)),
                  pl.BlockSpec((None, d, tn), lambda i, j: (i, 0, j)),
                  pl.BlockSpec((None, 1, tn), lambda i, j: (i, 0, j))],
        out_specs=pl.BlockSpec((None, SUBLANES, tn), lambda i, j: (i, 0, j)),
        out_shape=jax.ShapeDtypeStruct((depth, SUBLANES, n), F32),
        compiler_params=_cparams(("arbitrary", "arbitrary")),
        name="ada_mod",
    )(rows, ada_w, ada_b.reshape(depth, 1, n))


def _normmod(x, mult, shift):
    y = x * lax.rsqrt(jnp.mean(x * x, axis=-1, keepdims=True) + RMS_EPS)
    return (y * mult + shift).astype(BF16)


def _token_shift(u, prev8, next8, mu, first, last):
    tm = u.shape[0]
    row8 = lax.broadcasted_iota(jnp.int32, (SUBLANES, u.shape[1]), 0)
    prow = jnp.where(first, 0.0, prev8[SUBLANES - 1:SUBLANES, :])
    nrow = jnp.where(last, 0.0, next8[0:1, :])
    nbr = pltpu.roll(u, 1, 0) + pltpu.roll(u, tm - 1, 0)
    top = nbr[:SUBLANES] + jnp.where(row8 == 0, prow - u[tm - 1:tm], 0.0)
    bot = nbr[tm - SUBLANES:] + jnp.where(row8 == SUBLANES - 1, nrow - u[0:1], 0.0)
    nbr = jnp.concatenate([top, nbr[SUBLANES:tm - SUBLANES], bot], axis=0)
    return (1.0 - mu) * u + (0.5 * mu) * nbr


def _rwkv_in_kernel(x_ref, xp_ref, xn_ref, mult_ref, shift_ref, w_ref,
                    mua_ref, mub_ref, w2_ref, w0_ref, a2_ref, a0_ref, kk_ref, ka_ref, rk_ref, seg_ref,
                    r_ref, v_ref, kkn_ref, g_ref, bonus_ref, lw_ref, kd_ref, bb_ref, *, tiles_per_seq, c):
    i = pl.program_id(0)
    first = (i % tiles_per_seq) == 0
    last = (i % tiles_per_seq) == tiles_per_seq - 1
    tm = x_ref.shape[0]
    lo, hi = SUBLANES, SUBLANES + tm
    x_ext = jnp.concatenate([xp_ref[...], x_ref[...], xn_ref[...]], axis=0)
    h = _normmod(x_ext, mult_ref[...], shift_ref[...])

    def project(c0, c1):
        return _dot(h, w_ref[:, c0:c1])

    def shifted(p, mu):
        return _token_shift(p[lo:hi], p[:lo], p[hi:], mu, first, last)

    p_lora = project(4 * c, w_ref.shape[1])
    p_k = project(c, 2 * c)
    mb = shifted(p_lora, mub_ref[...])
    nl = 2 * W_LORA
    w_raw = w0_ref[...] + _dot(jnp.tanh(mb[:, :nl]).astype(BF16), w2_ref[...])
    a = _sigmoid(a0_ref[...] + _dot(mb[:, nl:].astype(BF16), a2_ref[...]))
    logw = -math.exp(-0.5) / (1.0 + jnp.exp(-w_raw))
    for d in range(2):
        lw_ref[d] = logw[:, d * c:(d + 1) * c]

    p_r = project(0, c)
    k = shifted(p_k, mua_ref[:, c:2 * c])
    kk = k * kk_ref[...]
    kkn = kk * lax.rsqrt(_seg_sum(kk * kk, seg_ref[...]) + 1e-12)
    kkn_ref[...] = kkn
    kka = k * ka_ref[...]
    k_rest = k - kka
    kd = [k_rest + kka * a[:, d * c:(d + 1) * c] for d in range(2)]
    for d in range(2):
        kd_ref[d] = kd[d]
        bb_ref[d] = kkn * a[:, d * c:(d + 1) * c]

    p_v = project(2 * c, 3 * c)
    r = shifted(p_r, mua_ref[:, :c])
    r_ref[...] = r
    p_g = project(3 * c, 4 * c)
    v = shifted(p_v, mua_ref[:, 2 * c:3 * c])
    v_ref[...] = v
    bonus_ref[...] = _seg_sum(r * (kd[0] + kd[1]) * rk_ref[...], seg_ref[...]) * v
    g_ref[...] = p_g[lo:hi]


def _rwkv_in(x2d, mult, shift, w_in, seq_len, rows_per_group, mu, w0, w2, a0, a2, k_k, k_a, r_k, seg, tm):
    r_rows, dm = x2d.shape
    ncols = w_in.shape[1]
    c = k_k.shape[-1]
    nb = ncols - 4 * c
    assert nb == 2 * (W_LORA + A_LORA) and seq_len % tm == 0 and rows_per_group % tm == 0
    tps = seq_len // tm
    tpg = rows_per_group // tm
    t8 = tm // SUBLANES
    nblk8 = r_rows // SUBLANES
    mua = mu[:3 * c].reshape(1, 3 * c)
    mub = mu[3 * c:].reshape(1, nb)
    zw = jnp.zeros((W_LORA, c), F32)
    w2cat = jnp.concatenate([jnp.concatenate([w2[0], zw], 1), jnp.concatenate([zw, w2[1]], 1)], 0).astype(BF16)
    a2cat = jnp.concatenate([jnp.concatenate([a2[0], zw], 1), jnp.concatenate([zw, a2[1]], 1)], 0).astype(BF16)
    w0cat = w0.reshape(1, 2 * c)
    a0cat = a0.reshape(1, 2 * c)
    full = lambda shape: pl.BlockSpec(shape, lambda i: (0,) * len(shape))
    prev_idx = lambda i: jnp.maximum(i * t8 - 1, 0)
    next_idx = lambda i: jnp.minimum((i + 1) * t8, nblk8 - 1)
    row_out = pl.BlockSpec((tm, c), lambda i: (i, 0))
    dir_out = pl.BlockSpec((2, tm, c), lambda i: (0, i, 0))
    group = pl.BlockSpec((None, 1, dm), lambda i: (i // tpg, 0, 0))
    kern = functools.partial(_rwkv_in_kernel, tiles_per_seq=tps, c=c)
    return pl.pallas_call(
        kern,
        grid=(r_rows // tm,),
        in_specs=[pl.BlockSpec((tm, dm), lambda i: (i, 0)),
                  pl.BlockSpec((SUBLANES, dm), lambda i: (prev_idx(i), 0)),
                  pl.BlockSpec((SUBLANES, dm), lambda i: (next_idx(i), 0)),
                  group, group, full((dm, ncols)),
                  full((1, 3 * c)), full((1, nb)),
                  full((2 * W_LORA, 2 * c)), full((1, 2 * c)),
                  full((2 * A_LORA, 2 * c)), full((1, 2 * c)),
                  full((1, c)), full((1, c)), full((1, c)), full((2 * LANES, LANES))],
        out_specs=[row_out] * 5 + [dir_out] * 3,
        out_shape=[jax.ShapeDtypeStruct((r_rows, c), F32)] * 5
                  + [jax.ShapeDtypeStruct((2, r_rows, c), F32)] * 3,
        compiler_params=_cparams(("arbitrary",)),
        name="rwkv_in",
    )(x2d, x2d, x2d, mult, shift, w_in, mua, mub, w2cat, w0cat, a2cat, a0cat,
      k_k.reshape(1, c), k_a.reshape(1, c), r_k.reshape(1, c), seg)


def _head_pair_stack(xp, lane_a):
    return jnp.concatenate([jnp.where(lane_a, xp, 0.0), jnp.where(lane_a, 0.0, xp)], axis=0)


def _wkv_chunk_operands(r, v, kk, lw, kd, bb, reverse, L):
    ti = lax.broadcasted_iota(jnp.int32, (L, L), 0)
    si = lax.broadcasted_iota(jnp.int32, (L, L), 1)
    tri = ((ti <= si) if reverse else (ti >= si)).astype(BF16)
    hi = lw.astype(BF16)
    r1 = lw - hi.astype(F32)
    mid = r1.astype(BF16)
    lo = (r1 - mid.astype(F32)).astype(BF16)
    cum = _dot(tri, hi) + _dot(tri, mid) + _dot(tri, lo)
    tot = jnp.sum(lw, axis=0, keepdims=True)
    ginv = jnp.exp(-cum)
    gend = jnp.exp(tot) * ginv
    return dict(kt=kk * jnp.exp(cum - lw), rt=r * jnp.exp(cum), bt=bb * ginv,
                kdt=kd * ginv, khat=kd * gend, bhat=bb * gend, v=v, tot=tot)


def _wkv_scan_kernel(rf_ref, vf_ref, kkf_ref, lwf_ref, kdf_ref, bbf_ref,
                     rb_ref, vb_ref, kkb_ref, lwb_ref, kdb_ref, bbb_ref, s0_ref,
                     of_ref, ob_ref, sfin_ref, h_scr, *, chunk, n_pairs):
    L = chunk
    ci = pl.program_id(0)
    nc = pl.num_programs(0)
    nb = rf_ref.shape[0]

    @pl.when(ci == 0)
    def _():
        h_scr[...] = s0_ref[...]

    in_refs = ((rf_ref, vf_ref, kkf_ref, lwf_ref, kdf_ref, bbf_ref),
               (rb_ref, vb_ref, kkb_ref, lwb_ref, kdb_ref, bbb_ref))
    o_refs = (of_ref, ob_ref)
    cps = rf_ref.shape[1] // L
    rows_of = lambda d, j: slice((cps - 1 - j if d else j) * L, (cps - j if d else j + 1) * L)
    ops = {(b, d, j): _wkv_chunk_operands(*[ref[b, rows_of(d, j), :] for ref in in_refs[d]], d == 1, L)
           for b in range(nb) for d in range(2) for j in range(cps)}

    lane_a = lax.broadcasted_iota(jnp.int32, (L, LANES), 1) < RW_HEAD
    dts = (lax.broadcasted_iota(jnp.int32, (L, LANES), 0)
           - (lax.broadcasted_iota(jnp.int32, (L, LANES), 1) & (L - 1)))
    strict = (dts > 0, dts < 0)
    incl = (dts >= 0, dts <= 0)
    eye = (dts == 0).astype(F32)
    same_head = ((lax.broadcasted_iota(jnp.int32, (LANES, LANES), 0) < RW_HEAD)
                 == (lax.broadcasted_iota(jnp.int32, (LANES, LANES), 1) < RW_HEAD))
    bd = lambda x: _head_pair_stack(x, lane_a).astype(BF16)
    lanes_of = lambda p: slice(p * LANES, (p + 1) * LANES)
    cat = lambda name, b, d, j, p: ops[b, d, j][name][:, lanes_of(p)]
    msk = lambda m, x: jnp.where(m, x, 0.0)

    chains = [(b, d, j, p) for p in range(n_pairs) for j in range(cps) for b in range(nb) for d in range(2)]
    kt_c = {ch: cat("kt", *ch).astype(BF16) for ch in chains}
    rt_c = {ch: cat("rt", *ch).astype(BF16) for ch in chains}
    v_c = {ch: cat("v", *ch) for ch in chains}
    v_s = {ch: bd(v_c[ch]) for ch in chains}
    sc = {ch: _dot_nt(jnp.concatenate([kt_c[ch], rt_c[ch]], axis=0),
                      jnp.concatenate([bd(cat("bt", *ch)), bd(cat("kdt", *ch))], axis=0)) for ch in chains}
    pw = {ch: -msk(strict[ch[1]], sc[ch][:L, :LANES]) for ch in chains}
    a_k = {ch: msk(strict[ch[1]], sc[ch][:L, LANES:]).astype(BF16) for ch in chains}
    pkb = {ch: jnp.concatenate([msk(incl[ch[1]], sc[ch][L:, LANES:]), -msk(incl[ch[1]], sc[ch][L:, :LANES])],
                               axis=1).astype(BF16) for ch in chains}
    tinv = {ch: eye + pw[ch] for ch in chains}
    pw = {ch: _dot(pw[ch].astype(BF16), bd(pw[ch])) for ch in chains}
    for step in range(1, int(math.log2(L))):
        if step < int(math.log2(L)) - 1:
            w = {ch: _dot(jnp.concatenate([tinv[ch], pw[ch]], axis=0).astype(BF16), bd(pw[ch])) for ch in chains}
            tinv = {ch: tinv[ch] + w[ch][:L] for ch in chains}
            pw = {ch: w[ch][L:] for ch in chains}
        else:
            tinv = {ch: (tinv[ch] + _dot(tinv[ch].astype(BF16), bd(pw[ch]))).astype(BF16) for ch in chains}

    h = {(b, d, p): h_scr[b, d, p] for b in range(nb) for d in range(2) for p in range(n_pairs)}
    for j in range(cps):
        now = [ch for ch in chains if ch[2] == j]
        h_b = {ch: h[ch[0], ch[1], ch[3]].astype(BF16) for ch in now}
        u = {ch: _dot(jnp.concatenate([kt_c[ch], a_k[ch]], axis=1), jnp.concatenate([h_b[ch], v_s[ch]], axis=0))
             for ch in now}
        u = {ch: _dot(tinv[ch], bd(u[ch])) for ch in now}
        for ch in now:
            b, d, _, p = ch
            lhs = jnp.concatenate([rt_c[ch], pkb[ch]], axis=1)
            rhs = jnp.concatenate([h_b[ch], v_s[ch], bd(u[ch])], axis=0)
            o_refs[d][b, rows_of(d, j), lanes_of(p)] = _dot(lhs, rhs)
        for ch in now:
            b, d, _, p = ch
            kb = jnp.concatenate([cat("khat", *ch), -cat("bhat", *ch)], axis=0).astype(BF16)
            vu = jnp.concatenate([v_c[ch], u[ch]], axis=0).astype(BF16)
            tot_col = jnp.broadcast_to(ops[b, d, j]["tot"][:, lanes_of(p)], (LANES, LANES)).T
            h[b, d, p] = jnp.exp(tot_col) * h[b, d, p] + jnp.where(same_head, _dot_tn(kb, vu), 0.0)
    for (b, d, p), val in h.items():
        h_scr[b, d, p] = val

    @pl.when(ci == nc - 1)
    def _():
        sfin_ref[...] = h_scr[...]


def _wkv_scan(r, v, kkn, logw, kd, bb, s0, chunk=SCAN_CHUNK):
    b, t, c = r.shape
    n_pairs = c // LANES
    rows = chunk * SCAN_CHUNKS_PER_STEP
    assert t % rows == 0 and 2 * RW_HEAD == LANES
    nc = t // rows
    fwd = lambda ci: ci
    bwd = lambda ci: nc - 1 - ci
    shared = lambda cmap: pl.BlockSpec((b, rows, c), lambda ci: (0, cmap(ci), 0))
    perdir = lambda d, cmap: pl.BlockSpec((None, b, rows, c), lambda ci: (d, 0, cmap(ci), 0))
    state = pl.BlockSpec((b, 2, n_pairs, LANES, LANES), lambda ci: (0, 0, 0, 0, 0))
    kern = functools.partial(_wkv_scan_kernel, chunk=chunk, n_pairs=n_pairs)
    return pl.pallas_call(
        kern,
        grid=(nc,),
        in_specs=[shared(fwd)] * 3 + [perdir(0, fwd)] * 3 + [shared(bwd)] * 3 + [perdir(1, bwd)] * 3 + [state],
        out_specs=[shared(fwd), shared(bwd), state],
        out_shape=[jax.ShapeDtypeStruct((b, t, c), F32), jax.ShapeDtypeStruct((b, t, c), F32),
                   jax.ShapeDtypeStruct((b, 2, n_pairs, LANES, LANES), F32)],
        scratch_shapes=[pltpu.VMEM((b, 2, n_pairs, LANES, LANES), F32)],
        compiler_params=_cparams(("arbitrary",)),
        name="wkv_scan",
    )(r, v, kkn, logw, kd, bb, r, v, kkn, logw, kd, bb, s0)


def _rwkv_readout_kernel(of_ref, ob_ref, bonus_ref, g_ref, x_ref, gate_ref,
                         gng_ref, gnb_ref, seg_ref, w_ref, out_ref):
    seg = seg_ref[...]
    inv_n = 1.0 / RW_HEAD
    o = of_ref[...] + ob_ref[...]
    mean = _seg_sum(o, seg) * inv_n
    dlt = o - mean
    var = _seg_sum(dlt * dlt, seg) * inv_n
    y = dlt * lax.rsqrt(var + GN_EPS) * gng_ref[...] + gnb_ref[...]
    y = (y + bonus_ref[...]) * _silu(g_ref[...])
    out_ref[...] = x_ref[...] + gate_ref[...] * _dot(y.astype(BF16), w_ref[...])


def _rwkv_readout(o_f, o_b, bonus, g, x2d, gate, gn_g, gn_b, seg, w_out, rows_per_group, tm):
    r_rows, c = g.shape
    d = x2d.shape[1]
    tpg = rows_per_group // tm
    full = lambda shape: pl.BlockSpec(shape, lambda i: (0,) * len(shape))
    row = lambda n: pl.BlockSpec((tm, n), lambda i: (i, 0))
    return pl.pallas_call(
        _rwkv_readout_kernel,
        grid=(r_rows // tm,),
        in_specs=[row(c), row(c), row(c), row(c), row(d),
                  pl.BlockSpec((None, 1, d), lambda i: (i // tpg, 0, 0)),
                  full((1, c)), full((1, c)), full((2 * LANES, LANES)), full((c, d))],
        out_specs=row(d),
        out_shape=jax.ShapeDtypeStruct((r_rows, d), F32),
        compiler_params=_cparams(("arbitrary",)),
        name="rwkv_readout",
    )(o_f, o_b, bonus, g, x2d, gate, gn_g.reshape(1, c), gn_b.reshape(1, c), seg, w_out)


def _head_norm(x, gain, seg):
    ms = _seg_sum(x * x, seg) * (1.0 / DA_HEAD)
    return x * lax.rsqrt(ms + RMS_EPS) * gain


def _rope(x, cos, sin_signed):
    n = x.shape[-1]
    lane = lax.broadcasted_iota(jnp.int32, x.shape, 1)
    half0 = (lane & (2 * ROPE_FREQS - 1)) < ROPE_FREQS
    partner = jnp.where(half0, pltpu.roll(x, n - ROPE_FREQS, 1), pltpu.roll(x, ROPE_FREQS, 1))
    reps = n // cos.shape[-1]
    wide = lambda tab: jnp.concatenate([tab] * reps, axis=1)
    return x * wide(cos) + partner * wide(sin_signed)


def _attn_in_kernel(*refs, latent, q_scale):
    it = iter(refs)
    x_ref, mult_ref, shift_ref, w_ref = next(it), next(it), next(it), next(it)
    cos_ref = next(it) if latent else None
    sin_ref = next(it) if latent else None
    qn_ref = next(it) if latent else None
    kn_ref, seg_ref = next(it), next(it)
    qo_ref = next(it) if latent else None
    ko_ref, vo_ref = next(it), next(it)
    go_ref = next(it) if latent else None
    h = _normmod(x_ref[...], mult_ref[...], shift_ref[...])
    w = ko_ref.shape[-1]
    cols = lambda j0, j1: _dot(h, w_ref[:, j0 * w:j1 * w])
    seg = seg_ref[...]
    if latent:
        p_k = cols(1, 2)
        p_q = cols(0, 1)
        k = _rope(_head_norm(p_k, kn_ref[...], seg), cos_ref[...], sin_ref[...])
        ko_ref[...] = k.astype(BF16)
        p_vg = cols(2, 4)
        q = _rope(_head_norm(p_q, qn_ref[...], seg), cos_ref[...], sin_ref[...])
        qo_ref[...] = (q * q_scale).astype(BF16)
        vo_ref[...] = p_vg[:, :w].T.astype(BF16)
        go_ref[...] = _silu(p_vg[:, w:]).astype(BF16)
    else:
        p_kv = cols(0, 2)
        ko_ref[...] = _head_norm(p_kv[:, :w], kn_ref[...], seg).astype(BF16)
        vo_ref[...] = p_kv[:, w:].T.astype(BF16)


def _attn_in(x2d, mult, shift, w_in, latent, cos, sin_signed, qn, kn, seg, w, seq_len, rows_per_group, tm):
    r_rows, dm = x2d.shape
    tps = seq_len // tm
    tpg = rows_per_group // tm
    full = lambda shape: pl.BlockSpec(shape, lambda i: (0,) * len(shape))
    group = pl.BlockSpec((None, 1, dm), lambda i: (i // tpg, 0, 0))
    tab = pl.BlockSpec((tm, LANES), lambda i: (i % tps, 0))
    tile_gain = lambda g: jnp.tile(g, w // g.shape[0]).reshape(1, w)
    in_specs = [pl.BlockSpec((tm, dm), lambda i: (i, 0)), group, group, full(w_in.shape)]
    args = [x2d, mult, shift, w_in]
    if latent:
        in_specs += [tab, tab, full((1, w))]
        args += [cos, sin_signed, tile_gain(qn)]
    in_specs += [full((1, w)), full((2 * LANES, LANES))]
    args += [tile_gain(kn), seg]
    rows_bf = (pl.BlockSpec((tm, w), lambda i: (i, 0)), jax.ShapeDtypeStruct((r_rows, w), BF16))
    v_t = (pl.BlockSpec((None, w, tm), lambda i: (i // tps, 0, i % tps)),
           jax.ShapeDtypeStruct((r_rows // seq_len, w, seq_len), BF16))
    outs = [rows_bf, rows_bf, v_t, rows_bf] if latent else [rows_bf, v_t]
    kern = functools.partial(_attn_in_kernel, latent=latent, q_scale=DA_HEAD ** -0.5)
    return pl.pallas_call(
        kern,
        grid=(r_rows // tm,),
        in_specs=in_specs,
        out_specs=[o[0] for o in outs],
        out_shape=[o[1] for o in outs],
        compiler_params=_cparams(("arbitrary",)),
        name="attn_in_lat" if latent else "attn_in_ctx",
    )(*args)


def _diff_attn_kernel(q_ref, kc_ref, kl_ref, vtc_ref, vtl_ref, lam_ref, sub_ref, o_ref, m_scr, l_scr, acc_scr,
                      *, lam_init, tk, bounded):
    q = q_ref[...]
    lane = lax.broadcasted_iota(jnp.int32, q.shape, 1)
    zero = jnp.zeros_like(q)
    qm = (jnp.where(lane < DA_HEAD, q, zero), jnp.where(lane < DA_HEAD, zero, q))
    if not bounded:
        m_scr[...] = jnp.full(m_scr.shape, -jnp.inf, F32)
        l_scr[...] = jnp.zeros(l_scr.shape, F32)
        acc_scr[...] = jnp.zeros(acc_scr.shape, F32)

    def accumulate(k, vt):
        for c in range(2):
            s = _dot_nt(k, qm[c])
            m_prev = m_scr[c]
            m_new = jnp.maximum(m_prev, jnp.max(s, axis=0, keepdims=True))
            alpha = jnp.exp(m_prev - m_new)
            p = jnp.exp(s - m_new)
            l_scr[c] = alpha * l_scr[c] + jnp.sum(p, axis=0, keepdims=True)
            acc_scr[c] = alpha * acc_scr[c] + _dot(vt, p.astype(BF16))
            m_scr[c] = m_new

    def body(j, carry):
        off = pl.multiple_of(j * tk, tk)
        accumulate(kl_ref[pl.ds(off, tk), :], vtl_ref[:, pl.ds(off, tk)])
        return carry

    if bounded:
        n_lat = kl_ref.shape[0] // tk
        keys = [kc_ref] + [kl_ref.at[pl.ds(j * tk, tk), :] for j in range(n_lat)]
        vals = [vtc_ref] + [vtl_ref.at[:, pl.ds(j * tk, tk)] for j in range(n_lat)]
        probs = lambda c, j: jnp.exp(_dot_nt(keys[j][...], qm[c]))

        def add_values(c, j, p):
            l_new = jnp.sum(p, axis=0, keepdims=True)
            acc_new = _dot(vals[j][...], p.astype(BF16))
            l_scr[c] = l_new if j == 0 else l_scr[c] + l_new
            acc_scr[c] = acc_new if j == 0 else acc_scr[c] + acc_new

        p = [probs(0, 0), probs(1, 0)]
        for j in range(len(keys)):
            more = j + 1 < len(keys)
            add_values(0, j, p[0])
            if more:
                p[0] = probs(0, j + 1)
            add_values(1, j, p[1])
            if more:
                p[1] = probs(1, j + 1)
    else:
        accumulate(kc_ref[...], vtc_ref[...])
        lax.fori_loop(0, kl_ref.shape[0] // tk, body, 0)

    lv = lam_ref[...]
    lam = (jnp.exp(jnp.sum(lv[0:1] * lv[1:2], axis=-1, keepdims=True))
           - jnp.exp(jnp.sum(lv[2:3] * lv[3:4], axis=-1, keepdims=True)) + lam_init)
    o = acc_scr[0] * (1.0 / l_scr[0]) - acc_scr[1] * (lam / l_scr[1])
    y = o * lax.rsqrt(jnp.mean(o * o, axis=0, keepdims=True) + SUBLN_EPS)
    o_ref[...] = (y * (sub_ref[...] * (1.0 - lam_init))).T.astype(BF16)


def _diff_attn(q, k_ctx, k_lat, vt_ctx, vt_lat, lam_vecs, subln, lam_init, tq, tk, bounded):
    b, t, w = q.shape
    lc = k_ctx.shape[1]
    nh = w // DA_VHEAD
    assert t % tq == 0 and t % tk == 0
    kern = functools.partial(_diff_attn_kernel, lam_init=lam_init, tk=tk, bounded=bounded)
    return pl.pallas_call(
        kern,
        grid=(b, nh, t // tq),
        in_specs=[pl.BlockSpec((None, tq, DA_VHEAD), lambda bi, h, i: (bi, i, h)),
                  pl.BlockSpec((None, lc, DA_VHEAD), lambda bi, h, i: (bi, 0, h)),
                  pl.BlockSpec((None, t, DA_VHEAD), lambda bi, h, i: (bi, 0, h)),
                  pl.BlockSpec((None, DA_VHEAD, lc), lambda bi, h, i: (bi, h, 0)),
                  pl.BlockSpec((None, DA_VHEAD, t), lambda bi, h, i: (bi, h, 0)),
                  pl.BlockSpec((4, DA_HEAD), lambda bi, h, i: (0, 0)),
                  pl.BlockSpec((DA_VHEAD, 1), lambda bi, h, i: (0, 0))],
        out_specs=pl.BlockSpec((None, tq, DA_VHEAD), lambda bi, h, i: (bi, i, h)),
        out_shape=jax.ShapeDtypeStruct((b, t, w), BF16),
        scratch_shapes=[pltpu.VMEM((2, 1, tq), F32), pltpu.VMEM((2, 1, tq), F32),
                        pltpu.VMEM((2, DA_VHEAD, tq), F32)],
        compiler_params=_cparams(("arbitrary", "arbitrary", "arbitrary")),
        name="diff_attn_bounded" if bounded else "diff_attn",
    )(q, k_ctx, k_lat, vt_ctx, vt_lat, lam_vecs, subln.reshape(DA_VHEAD, 1))


def _gated_out_kernel(y_ref, g_ref, x_ref, gate_ref, w_ref, out_ref):
    y = y_ref[...].astype(F32) * g_ref[...].astype(F32)
    out_ref[...] = x_ref[...] + gate_ref[...] * _dot(y.astype(BF16), w_ref[...])


def _gated_out(y, g, x2d, gate, w_out, rows_per_group, tm):
    r_rows, w = y.shape
    d = x2d.shape[1]
    tpg = rows_per_group // tm
    row = lambda n: pl.BlockSpec((tm, n), lambda i: (i, 0))
    return pl.pallas_call(
        _gated_out_kernel,
        grid=(r_rows // tm,),
        in_specs=[row(w), row(w), row(d),
                  pl.BlockSpec((None, 1, d), lambda i: (i // tpg, 0, 0)),
                  pl.BlockSpec((w, d), lambda i: (0, 0))],
        out_specs=row(d),
        out_shape=jax.ShapeDtypeStruct((r_rows, d), F32),
        compiler_params=_cparams(("arbitrary",)),
        name="gated_out",
    )(y, g, x2d, gate, w_out)


def _segment_matrix(width):
    i = (lax.broadcasted_iota(jnp.int32, (2 * LANES, LANES), 0) % LANES) // width
    j = lax.broadcasted_iota(jnp.int32, (2 * LANES, LANES), 1) // width
    return (i == j).astype(BF16)


def _rope_tables(rows):
    row_ids = jnp.repeat(jnp.arange(rows), GRID_W).astype(F32)
    col_ids = jnp.tile(jnp.arange(GRID_W), rows).astype(F32)
    inv_freq = ROPE_THETA ** (-jnp.arange(ROPE_FREQS, dtype=F32) / ROPE_FREQS)
    ang_r = row_ids[:, None] * inv_freq
    ang_c = col_ids[:, None] * inv_freq
    cr, sr, cc, sc = jnp.cos(ang_r), jnp.sin(ang_r), jnp.cos(ang_c), jnp.sin(ang_c)
    cos = jnp.concatenate([cr, cr, cc, cc], axis=-1)
    sin_signed = jnp.concatenate([-sr, sr, -sc, sc], axis=-1)
    return jnp.tile(cos, (1, 2)), jnp.tile(sin_signed, (1, 2))


def kernel(x, c, ctx, c_ctx, ada_w, ada_b, norm_g, rw_in, rw_mu, rw_w0, rw_w2, rw_a0, rw_a2, rw_kk, rw_ka, rw_rk, rw_gn_g, rw_gn_b, rw_out, da_in, da_qn, da_kn, da_lam, da_subln, da_out):
    bsz, t, d = x.shape
    lc = ctx.shape[1]
    depth = ada_w.shape[0]
    assert depth == 2 and bsz + 1 <= SUBLANES
    cw = rw_kk.shape[-1]
    x2 = x.reshape(bsz * t, d)
    xc2 = ctx.reshape(bsz * lc, d)

    rows = jnp.zeros((SUBLANES, d), F32).at[:bsz].set(c).at[bsz].set(c_ctx)
    mod = _ada_mod(rows, ada_w, ada_b)
    shift, scale, gate = mod[:, :, :d], mod[:, :, d:2 * d], mod[:, :, 2 * d:]
    mult = norm_g[:, None, :] * (1.0 + scale)
    lat = lambda a, i: a[i, :bsz].reshape(bsz, 1, d)
    con = lambda a, i: a[i, bsz:bsz + 1].reshape(1, 1, d)

    tm_l = min(PROJ_ROW_TILE, t)
    tm_c = min(PROJ_ROW_TILE, lc)
    tm_stream = min(STREAM_ROW_TILE, t)

    w_in0 = rw_in[0].astype(BF16)
    seg64 = _segment_matrix(RW_HEAD)
    rwkv_in = functools.partial(_rwkv_in, w_in=w_in0, mu=rw_mu[0], w0=rw_w0[0], w2=rw_w2[0], a0=rw_a0[0],
                                a2=rw_a2[0], k_k=rw_kk[0], k_a=rw_ka[0], r_k=rw_rk[0].reshape(-1), seg=seg64)
    r_l, v_l, kk_l, g_l, bonus_l, lw_l, kd_l, bb_l = rwkv_in(x2, lat(mult, 0), lat(shift, 0), seq_len=t,
                                                             rows_per_group=t, tm=tm_l)
    r_c, v_c, kk_c, g_c, bonus_c, lw_c, kd_c, bb_c = rwkv_in(xc2, con(mult, 0), con(shift, 0), seq_len=lc,
                                                             rows_per_group=bsz * lc, tm=tm_c)

    s0 = jnp.zeros((bsz, 2, cw // LANES, LANES, LANES), F32)
    b3 = lambda a, n: a.reshape(bsz, n, cw)
    b4 = lambda a, n: a.reshape(2, bsz, n, cw)
    ocf, ocb, s_c = _wkv_scan(b3(r_c, lc), b3(v_c, lc), b3(kk_c, lc), b4(lw_c, lc), b4(kd_c, lc),
                              b4(bb_c, lc), s0)
    olf, olb, _ = _wkv_scan(b3(r_l, t), b3(v_l, t), b3(kk_l, t), b4(lw_l, t), b4(kd_l, t), b4(bb_l, t), s_c)

    w_out0 = rw_out[0].astype(BF16)
    readout = functools.partial(_rwkv_readout, gn_g=rw_gn_g[0], gn_b=rw_gn_b[0], seg=seg64, w_out=w_out0)
    x1 = readout(olf.reshape(bsz * t, cw), olb.reshape(bsz * t, cw), bonus_l, g_l, x2,
                 lat(gate, 0), rows_per_group=t, tm=tm_stream)
    xc1 = readout(ocf.reshape(bsz * lc, cw), ocb.reshape(bsz * lc, cw), bonus_c, g_c, xc2,
                  con(gate, 0), rows_per_group=bsz * lc, tm=tm_c)

    qw = DA_HEADS * 2 * DA_HEAD
    w_in1 = da_in[0].astype(BF16)
    seg_da = _segment_matrix(DA_HEAD)
    cos, sin_signed = _rope_tables(t // GRID_W)
    q_b, k_b, vt_b, g2_l = _attn_in(x1, lat(mult, 1), lat(shift, 1), w_in1, True, cos, sin_signed, da_qn[0],
                                    da_kn[0], seg_da, qw, seq_len=t, rows_per_group=t, tm=tm_l)
    kc_b, vtc_b = _attn_in(xc1, con(mult, 1), con(shift, 1), w_in1[:, qw:3 * qw], False, None, None, None,
                           da_kn[0], seg_da, qw, seq_len=lc, rows_per_group=bsz * lc, tm=tm_c)
    lam_init = 0.8 - 0.6 * math.exp(-0.3 * 1)
    attn = functools.partial(_diff_attn, lam_vecs=da_lam[0], subln=da_subln[0], lam_init=lam_init,
                             tq=min(ATTN_Q_TILE, t), tk=min(ATTN_KEY_CHUNK, t))
    score_bound = DA_HEAD ** 0.5 * jnp.max(jnp.abs(da_qn[0])) * jnp.max(jnp.abs(da_kn[0]))
    y_att = lax.cond(score_bound < MAX_UNSHIFTED_SCORE,
                     functools.partial(attn, bounded=True), functools.partial(attn, bounded=False),
                     q_b.reshape(bsz, t, qw), kc_b.reshape(bsz, lc, qw), k_b.reshape(bsz, t, qw), vtc_b, vt_b)
    out = _gated_out(y_att.reshape(bsz * t, qw), g2_l, x1, lat(gate, 1), da_out[0].astype(BF16),
                     rows_per_group=t, tm=tm_stream)
    return out.reshape(bsz, t, d)
```

```python
import functools
import math

import jax
import jax.numpy as jnp
from jax import lax
from jax.experimental import pallas as pl
from jax.experimental.pallas import tpu as pltpu

F32 = jnp.float32
BF16 = jnp.bfloat16

GRID_W = 64
RMS_EPS = 1e-6
RW_HEAD = 64
W_LORA = 64
A_LORA = 64
GN_EPS = 64e-5
DA_HEADS = 8
DA_HEAD = 64
DA_VHEAD = 2 * DA_HEAD
SUBLN_EPS = 1e-5
ROPE_THETA = 10000.0
ROPE_FREQS = DA_HEAD // 4

LANES = 128
SUBLANES = 8
SCAN_CHUNK = 64
SCAN_CHUNKS_PER_STEP = 1
VMEM_LIMIT = 56 * 1024 * 1024
MAX_UNSHIFTED_SCORE = 40.0
PROJ_ROW_TILE = 256
STREAM_ROW_TILE = 512
ATTN_Q_TILE = 1024
ATTN_KEY_CHUNK = 4096


def _cparams(sem):
    return pltpu.CompilerParams(dimension_semantics=sem, vmem_limit_bytes=VMEM_LIMIT)


def _dot(a, b):
    return jnp.dot(a, b, preferred_element_type=F32)


def _dot_nt(a, b):
    return lax.dot_general(a, b, (((1,), (1,)), ((), ())), preferred_element_type=F32)


def _dot_tn(a, b):
    return lax.dot_general(a, b, (((0,), (0,)), ((), ())), preferred_element_type=F32)


def _split2(x):
    hi = x.astype(BF16)
    lo = (x - hi.astype(F32)).astype(BF16)
    return hi, lo


def _seg_sum(x, seg2):
    out = []
    for j in range(x.shape[-1] // LANES):
        hi, lo = _split2(x[:, j * LANES:(j + 1) * LANES])
        out.append(_dot(jnp.concatenate([hi, lo], axis=1), seg2))
    return jnp.concatenate(out, axis=1)


def _sigmoid(x):
    return 1.0 / (1.0 + jnp.exp(-x))


def _silu(x):
    return x * _sigmoid(x)


def _ada_kernel(c_ref, w_ref, b_ref, o_ref):
    s = _silu(c_ref[...])
    o_ref[...] = jnp.dot(s, w_ref[...], preferred_element_type=F32,
                         precision=lax.Precision.HIGHEST) + b_ref[...]


def _ada_mod(rows, ada_w, ada_b):
    depth, d, n = ada_w.shape
    tn = 768
    return pl.pallas_call(
        _ada_kernel,
        grid=(depth, n // tn),
        in_specs=[pl.BlockSpec((SUBLANES, d), lambda i, j: (0, 0)),
                  pl.BlockSpec((None, d, tn), lambda i, j: (i, 0, j)),
                  pl.BlockSpec((None, 1, tn), lambda i, j: (i, 0, j))],
        out_specs=pl.BlockSpec((None, SUBLANES, tn), lambda i, j: (i, 0, j)),
        out_shape=jax.ShapeDtypeStruct((depth, SUBLANES, n), F32),
        compiler_params=_cparams(("arbitrary", "arbitrary")),
        name="ada_mod",
    )(rows, ada_w, ada_b.reshape(depth, 1, n))


def _normmod(x, mult, shift):
    y = x * lax.rsqrt(jnp.mean(x * x, axis=-1, keepdims=True) + RMS_EPS)
    return (y * mult + shift).astype(BF16)


def _token_shift(u, prev8, next8, mu, first, last):
    tm = u.shape[0]
    row8 = lax.broadcasted_iota(jnp.int32, (SUBLANES, u.shape[1]), 0)
    prow = jnp.where(first, 0.0, prev8[SUBLANES - 1:SUBLANES, :])
    nrow = jnp.where(last, 0.0, next8[0:1, :])
    nbr = pltpu.roll(u, 1, 0) + pltpu.roll(u, tm - 1, 0)
    top = nbr[:SUBLANES] + jnp.where(row8 == 0, prow - u[tm - 1:tm], 0.0)
    bot = nbr[tm - SUBLANES:] + jnp.where(row8 == SUBLANES - 1, nrow - u[0:1], 0.0)
    nbr = jnp.concatenate([top, nbr[SUBLANES:tm - SUBLANES], bot], axis=0)
    return (1.0 - mu) * u + (0.5 * mu) * nbr


def _rwkv_in_kernel(x_ref, xp_ref, xn_ref, mult_ref, shift_ref, w_ref,
                    mua_ref, mub_ref, w2_ref, w0_ref, a2_ref, a0_ref, kk_ref, ka_ref, rk_ref, seg_ref,
                    r_ref, v_ref, kkn_ref, g_ref, bonus_ref, lw_ref, kd_ref, bb_ref, *, tiles_per_seq, c):
    i = pl.program_id(0)
    first = (i % tiles_per_seq) == 0
    last = (i % tiles_per_seq) == tiles_per_seq - 1
    tm = x_ref.shape[0]
    lo, hi = SUBLANES, SUBLANES + tm
    x_ext = jnp.concatenate([xp_ref[...], x_ref[...], xn_ref[...]], axis=0)
    h = _normmod(x_ext, mult_ref[...], shift_ref[...])

    def project(c0, c1):
        return _dot(h, w_ref[:, c0:c1])

    def shifted(p, mu):
        return _token_shift(p[lo:hi], p[:lo], p[hi:], mu, first, last)

    p_lora = project(4 * c, w_ref.shape[1])
    p_k = project(c, 2 * c)
    mb = shifted(p_lora, mub_ref[...])
    nl = 2 * W_LORA
    w_raw = w0_ref[...] + _dot(jnp.tanh(mb[:, :nl]).astype(BF16), w2_ref[...])
    a = _sigmoid(a0_ref[...] + _dot(mb[:, nl:].astype(BF16), a2_ref[...]))
    logw = -math.exp(-0.5) / (1.0 + jnp.exp(-w_raw))
    for d in range(2):
        lw_ref[d] = logw[:, d * c:(d + 1) * c]

    p_r = project(0, c)
    k = shifted(p_k, mua_ref[:, c:2 * c])
    kk = k * kk_ref[...]
    kkn = kk * lax.rsqrt(_seg_sum(kk * kk, seg_ref[...]) + 1e-12)
    kkn_ref[...] = kkn
    kka = k * ka_ref[...]
    k_rest = k - kka
    kd = [k_rest + kka * a[:, d * c:(d + 1) * c] for d in range(2)]
    for d in range(2):
        kd_ref[d] = kd[d]
        bb_ref[d] = kkn * a[:, d * c:(d + 1) * c]

    p_v = project(2 * c, 3 * c)
    r = shifted(p_r, mua_ref[:, :c])
    r_ref[...] = r
    p_g = project(3 * c, 4 * c)
    v = shifted(p_v, mua_ref[:, 2 * c:3 * c])
    v_ref[...] = v
    bonus_ref[...] = _seg_sum(r * (kd[0] + kd[1]) * rk_ref[...], seg_ref[...]) * v
    g_ref[...] = p_g[lo:hi]


def _rwkv_in(x2d, mult, shift, w_in, seq_len, rows_per_group, mu, w0, w2, a0, a2, k_k, k_a, r_k, seg, tm):
    r_rows, dm = x2d.shape
    ncols = w_in.shape[1]
    c = k_k.shape[-1]
    nb = ncols - 4 * c
    assert nb == 2 * (W_LORA + A_LORA) and seq_len % tm == 0 and rows_per_group % tm == 0
    tps = seq_len // tm
    tpg = rows_per_group // tm
    t8 = tm // SUBLANES
    nblk8 = r_rows // SUBLANES
    mua = mu[:3 * c].reshape(1, 3 * c)
    mub = mu[3 * c:].reshape(1, nb)
    zw = jnp.zeros((W_LORA, c), F32)
    w2cat = jnp.concatenate([jnp.concatenate([w2[0], zw], 1), jnp.concatenate([zw, w2[1]], 1)], 0).astype(BF16)
    a2cat = jnp.concatenate([jnp.concatenate([a2[0], zw], 1), jnp.concatenate([zw, a2[1]], 1)], 0).astype(BF16)
    w0cat = w0.reshape(1, 2 * c)
    a0cat = a0.reshape(1, 2 * c)
    full = lambda shape: pl.BlockSpec(shape, lambda i: (0,) * len(shape))
    prev_idx = lambda i: jnp.maximum(i * t8 - 1, 0)
    next_idx = lambda i: jnp.minimum((i + 1) * t8, nblk8 - 1)
    row_out = pl.BlockSpec((tm, c), lambda i: (i, 0))
    dir_out = pl.BlockSpec((2, tm, c), lambda i: (0, i, 0))
    group = pl.BlockSpec((None, 1, dm), lambda i: (i // tpg, 0, 0))
    kern = functools.partial(_rwkv_in_kernel, tiles_per_seq=tps, c=c)
    return pl.pallas_call(
        kern,
        grid=(r_rows // tm,),
        in_specs=[pl.BlockSpec((tm, dm), lambda i: (i, 0)),
                  pl.BlockSpec((SUBLANES, dm), lambda i: (prev_idx(i), 0)),
                  pl.BlockSpec((SUBLANES, dm), lambda i: (next_idx(i), 0)),
                  group, group, full((dm, ncols)),
                  full((1, 3 * c)), full((1, nb)),
                  full((2 * W_LORA, 2 * c)), full((1, 2 * c)),
                  full((2 * A_LORA, 2 * c)), full((1, 2 * c)),
                  full((1, c)), full((1, c)), full((1, c)), full((2 * LANES, LANES))],
        out_specs=[row_out] * 5 + [dir_out] * 3,
        out_shape=[jax.ShapeDtypeStruct((r_rows, c), F32)] * 5
                  + [jax.ShapeDtypeStruct((2, r_rows, c), F32)] * 3,
        compiler_params=_cparams(("arbitrary",)),
        name="rwkv_in",
    )(x2d, x2d, x2d, mult, shift, w_in, mua, mub, w2cat, w0cat, a2cat, a0cat,
      k_k.reshape(1, c), k_a.reshape(1, c), r_k.reshape(1, c), seg)


def _head_pair_stack(xp, lane_a):
    return jnp.concatenate([jnp.where(lane_a, xp, 0.0), jnp.where(lane_a, 0.0, xp)], axis=0)


def _wkv_chunk_operands(r, v, kk, lw, kd, bb, reverse, L):
    ti = lax.broadcasted_iota(jnp.int32, (L, L), 0)
    si = lax.broadcasted_iota(jnp.int32, (L, L), 1)
    tri = ((ti <= si) if reverse else (ti >= si)).astype(BF16)
    hi = lw.astype(BF16)
    r1 = lw - hi.astype(F32)
    mid = r1.astype(BF16)
    lo = (r1 - mid.astype(F32)).astype(BF16)
    cum = _dot(jnp.concatenate([tri, tri, tri], axis=1), jnp.concatenate([hi, mid, lo], axis=0))
    tot = jnp.sum(lw, axis=0, keepdims=True)
    ginv = jnp.exp(-cum)
    gend = jnp.exp(tot) * ginv
    return dict(kt=kk * jnp.exp(cum - lw), rt=r * jnp.exp(cum), bt=bb * ginv,
                kdt=kd * ginv, khat=kd * gend, bhat=bb * gend, v=v, tot=tot)


def _wkv_scan_kernel(rf_ref, vf_ref, kkf_ref, lwf_ref, kdf_ref, bbf_ref,
                     rb_ref, vb_ref, kkb_ref, lwb_ref, kdb_ref, bbb_ref, s0_ref,
                     of_ref, ob_ref, sfin_ref, h_scr, *, chunk, n_pairs):
    L = chunk
    ci = pl.program_id(0)
    nc = pl.num_programs(0)
    nb = rf_ref.shape[0]

    @pl.when(ci == 0)
    def _():
        h_scr[...] = s0_ref[...]

    in_refs = ((rf_ref, vf_ref, kkf_ref, lwf_ref, kdf_ref, bbf_ref),
               (rb_ref, vb_ref, kkb_ref, lwb_ref, kdb_ref, bbb_ref))
    o_refs = (of_ref, ob_ref)
    cps = rf_ref.shape[1] // L
    rows_of = lambda d, j: slice((cps - 1 - j if d else j) * L, (cps - j if d else j + 1) * L)
    ops = {(b, d, j): _wkv_chunk_operands(*[ref[b, rows_of(d, j), :] for ref in in_refs[d]], d == 1, L)
           for b in range(nb) for d in range(2) for j in range(cps)}

    lane_a = lax.broadcasted_iota(jnp.int32, (L, LANES), 1) < RW_HEAD
    dts = (lax.broadcasted_iota(jnp.int32, (L, LANES), 0)
           - (lax.broadcasted_iota(jnp.int32, (L, LANES), 1) & (L - 1)))
    strict = (dts > 0, dts < 0)
    incl = (dts >= 0, dts <= 0)
    eye = (dts == 0).astype(F32)
    same_head = ((lax.broadcasted_iota(jnp.int32, (LANES, LANES), 0) < RW_HEAD)
                 == (lax.broadcasted_iota(jnp.int32, (LANES, LANES), 1) < RW_HEAD))
    bd = lambda x: _head_pair_stack(x, lane_a).astype(BF16)
    lanes_of = lambda p: slice(p * LANES, (p + 1) * LANES)
    cat = lambda name, b, d, j, p: ops[b, d, j][name][:, lanes_of(p)]
    msk = lambda m, x: jnp.where(m, x, 0.0)

    chains = [(b, d, j, p) for p in range(n_pairs) for j in range(cps) for b in range(nb) for d in range(2)]
    kt_c = {ch: cat("kt", *ch).astype(BF16) for ch in chains}
    rt_c = {ch: cat("rt", *ch).astype(BF16) for ch in chains}
    v_c = {ch: cat("v", *ch) for ch in chains}
    v_s = {ch: bd(v_c[ch]) for ch in chains}
    sc = {ch: _dot_nt(jnp.concatenate([kt_c[ch], rt_c[ch]], axis=0),
                      jnp.concatenate([bd(cat("bt", *ch)), bd(cat("kdt", *ch))], axis=0)) for ch in chains}
    pw = {ch: -msk(strict[ch[1]], sc[ch][:L, :LANES]) for ch in chains}
    a_k = {ch: msk(strict[ch[1]], sc[ch][:L, LANES:]).astype(BF16) for ch in chains}
    pkb = {ch: jnp.concatenate([msk(incl[ch[1]], sc[ch][L:, LANES:]), -msk(incl[ch[1]], sc[ch][L:, :LANES])],
                               axis=1).astype(BF16) for ch in chains}
    tinv = {ch: eye + pw[ch] for ch in chains}
    pw = {ch: _dot(pw[ch].astype(BF16), bd(pw[ch])) for ch in chains}
    for step in range(1, int(math.log2(L))):
        if step < int(math.log2(L)) - 1:
            w = {ch: _dot(jnp.concatenate([tinv[ch], pw[ch]], axis=0).astype(BF16), bd(pw[ch])) for ch in chains}
            tinv = {ch: tinv[ch] + w[ch][:L] for ch in chains}
            pw = {ch: w[ch][L:] for ch in chains}
        else:
            tinv = {ch: (tinv[ch] + _dot(tinv[ch].astype(BF16), bd(pw[ch]))).astype(BF16) for ch in chains}

    h = {(b, d, p): h_scr[b, d, p] for b in range(nb) for d in range(2) for p in range(n_pairs)}
    for j in range(cps):
        now = [ch for ch in chains if ch[2] == j]
        h_b = {ch: h[ch[0], ch[1], ch[3]].astype(BF16) for ch in now}
        u = {ch: _dot(jnp.concatenate([kt_c[ch], a_k[ch]], axis=1), jnp.concatenate([h_b[ch], v_s[ch]], axis=0))
             for ch in now}
        u = {ch: _dot(tinv[ch], bd(u[ch])) for ch in now}
        for ch in now:
            b, d, _, p = ch
            lhs = jnp.concatenate([rt_c[ch], pkb[ch]], axis=1)
            rhs = jnp.concatenate([h_b[ch], v_s[ch], bd(u[ch])], axis=0)
            o_refs[d][b, rows_of(d, j), lanes_of(p)] = _dot(lhs, rhs)
        for ch in now:
            b, d, _, p = ch
            kb = jnp.concatenate([cat("khat", *ch), -cat("bhat", *ch)], axis=0).astype(BF16)
            vu = jnp.concatenate([v_c[ch], u[ch]], axis=0).astype(BF16)
            tot_col = jnp.broadcast_to(ops[b, d, j]["tot"][:, lanes_of(p)], (LANES, LANES)).T
            h[b, d, p] = jnp.exp(tot_col) * h[b, d, p] + jnp.where(same_head, _dot_tn(kb, vu), 0.0)
    for (b, d, p), val in h.items():
        h_scr[b, d, p] = val

    @pl.when(ci == nc - 1)
    def _():
        sfin_ref[...] = h_scr[...]


def _wkv_scan(r, v, kkn, logw, kd, bb, s0, chunk=SCAN_CHUNK):
    b, t, c = r.shape
    n_pairs = c // LANES
    rows = chunk * SCAN_CHUNKS_PER_STEP
    assert t % rows == 0 and 2 * RW_HEAD == LANES
    nc = t // rows
    fwd = lambda ci: ci
    bwd = lambda ci: nc - 1 - ci
    shared = lambda cmap: pl.BlockSpec((b, rows, c), lambda ci: (0, cmap(ci), 0))
    perdir = lambda d, cmap: pl.BlockSpec((None, b, rows, c), lambda ci: (d, 0, cmap(ci), 0))
    state = pl.BlockSpec((b, 2, n_pairs, LANES, LANES), lambda ci: (0, 0, 0, 0, 0))
    kern = functools.partial(_wkv_scan_kernel, chunk=chunk, n_pairs=n_pairs)
    return pl.pallas_call(
        kern,
        grid=(nc,),
        in_specs=[shared(fwd)] * 3 + [perdir(0, fwd)] * 3 + [shared(bwd)] * 3 + [perdir(1, bwd)] * 3 + [state],
        out_specs=[shared(fwd), shared(bwd), state],
        out_shape=[jax.ShapeDtypeStruct((b, t, c), F32), jax.ShapeDtypeStruct((b, t, c), F32),
                   jax.ShapeDtypeStruct((b, 2, n_pairs, LANES, LANES), F32)],
        scratch_shapes=[pltpu.VMEM((b, 2, n_pairs, LANES, LANES), F32)],
        compiler_params=_cparams(("arbitrary",)),
        name="wkv_scan",
    )(r, v, kkn, logw, kd, bb, r, v, kkn, logw, kd, bb, s0)


def _rwkv_readout_kernel(of_ref, ob_ref, bonus_ref, g_ref, x_ref, gate_ref,
                         gng_ref, gnb_ref, seg_ref, w_ref, out_ref):
    seg = seg_ref[...]
    inv_n = 1.0 / RW_HEAD
    o = of_ref[...] + ob_ref[...]
    mean = _seg_sum(o, seg) * inv_n
    dlt = o - mean
    var = _seg_sum(dlt * dlt, seg) * inv_n
    y = dlt * lax.rsqrt(var + GN_EPS) * gng_ref[...] + gnb_ref[...]
    y = (y + bonus_ref[...]) * _silu(g_ref[...])
    out_ref[...] = x_ref[...] + gate_ref[...] * _dot(y.astype(BF16), w_ref[...])


def _rwkv_readout(o_f, o_b, bonus, g, x2d, gate, gn_g, gn_b, seg, w_out, rows_per_group, tm):
    r_rows, c = g.shape
    d = x2d.shape[1]
    tpg = rows_per_group // tm
    full = lambda shape: pl.BlockSpec(shape, lambda i: (0,) * len(shape))
    row = lambda n: pl.BlockSpec((tm, n), lambda i: (i, 0))
    return pl.pallas_call(
        _rwkv_readout_kernel,
        grid=(r_rows // tm,),
        in_specs=[row(c), row(c), row(c), row(c), row(d),
                  pl.BlockSpec((None, 1, d), lambda i: (i // tpg, 0, 0)),
                  full((1, c)), full((1, c)), full((2 * LANES, LANES)), full((c, d))],
        out_specs=row(d),
        out_shape=jax.ShapeDtypeStruct((r_rows, d), F32),
        compiler_params=_cparams(("arbitrary",)),
        name="rwkv_readout",
    )(o_f, o_b, bonus, g, x2d, gate, gn_g.reshape(1, c), gn_b.reshape(1, c), seg, w_out)


def _head_norm(x, gain, seg):
    ms = _seg_sum(x * x, seg) * (1.0 / DA_HEAD)
    return x * lax.rsqrt(ms + RMS_EPS) * gain


def _rope(x, cos, sin_signed):
    n = x.shape[-1]
    lane = lax.broadcasted_iota(jnp.int32, x.shape, 1)
    half0 = (lane & (2 * ROPE_FREQS - 1)) < ROPE_FREQS
    partner = jnp.where(half0, pltpu.roll(x, n - ROPE_FREQS, 1), pltpu.roll(x, ROPE_FREQS, 1))
    reps = n // cos.shape[-1]
    wide = lambda tab: jnp.concatenate([tab] * reps, axis=1)
    return x * wide(cos) + partner * wide(sin_signed)


def _attn_in_kernel(*refs, latent, q_scale):
    it = iter(refs)
    x_ref, mult_ref, shift_ref, w_ref = next(it), next(it), next(it), next(it)
    cos_ref = next(it) if latent else None
    sin_ref = next(it) if latent else None
    qn_ref = next(it) if latent else None
    kn_ref, seg_ref = next(it), next(it)
    qo_ref = next(it) if latent else None
    ko_ref, vo_ref = next(it), next(it)
    go_ref = next(it) if latent else None
    h = _normmod(x_ref[...], mult_ref[...], shift_ref[...])
    w = ko_ref.shape[-1]
    cols = lambda j0, j1: _dot(h, w_ref[:, j0 * w:j1 * w])
    seg = seg_ref[...]
    if latent:
        p_k = cols(1, 2)
        p_q = cols(0, 1)
        k = _rope(_head_norm(p_k, kn_ref[...], seg), cos_ref[...], sin_ref[...])
        ko_ref[...] = k.astype(BF16)
        p_vg = cols(2, 4)
        q = _rope(_head_norm(p_q, qn_ref[...], seg), cos_ref[...], sin_ref[...])
        qo_ref[...] = (q * q_scale).astype(BF16)
        vo_ref[...] = p_vg[:, :w].T.astype(BF16)
        go_ref[...] = _silu(p_vg[:, w:]).astype(BF16)
    else:
        p_kv = cols(0, 2)
        ko_ref[...] = _head_norm(p_kv[:, :w], kn_ref[...], seg).astype(BF16)
        vo_ref[...] = p_kv[:, w:].T.astype(BF16)


def _attn_in(x2d, mult, shift, w_in, latent, cos, sin_signed, qn, kn, seg, w, seq_len, rows_per_group, tm):
    r_rows, dm = x2d.shape
    tps = seq_len // tm
    tpg = rows_per_group // tm
    full = lambda shape: pl.BlockSpec(shape, lambda i: (0,) * len(shape))
    group = pl.BlockSpec((None, 1, dm), lambda i: (i // tpg, 0, 0))
    tab = pl.BlockSpec((tm, LANES), lambda i: (i % tps, 0))
    tile_gain = lambda g: jnp.tile(g, w // g.shape[0]).reshape(1, w)
    in_specs = [pl.BlockSpec((tm, dm), lambda i: (i, 0)), group, group, full(w_in.shape)]
    args = [x2d, mult, shift, w_in]
    if latent:
        in_specs += [tab, tab, full((1, w))]
        args += [cos, sin_signed, tile_gain(qn)]
    in_specs += [full((1, w)), full((2 * LANES, LANES))]
    args += [tile_gain(kn), seg]
    rows_bf = (pl.BlockSpec((tm, w), lambda i: (i, 0)), jax.ShapeDtypeStruct((r_rows, w), BF16))
    v_t = (pl.BlockSpec((None, w, tm), lambda i: (i // tps, 0, i % tps)),
           jax.ShapeDtypeStruct((r_rows // seq_len, w, seq_len), BF16))
    outs = [rows_bf, rows_bf, v_t, rows_bf] if latent else [rows_bf, v_t]
    kern = functools.partial(_attn_in_kernel, latent=latent, q_scale=DA_HEAD ** -0.5)
    return pl.pallas_call(
        kern,
        grid=(r_rows // tm,),
        in_specs=in_specs,
        out_specs=[o[0] for o in outs],
        out_shape=[o[1] for o in outs],
        compiler_params=_cparams(("arbitrary",)),
        name="attn_in_lat" if latent else "attn_in_ctx",
    )(*args)


def _diff_attn_kernel(q_ref, kc_ref, kl_ref, vtc_ref, vtl_ref, lam_ref, sub_ref, o_ref, m_scr, l_scr, acc_scr,
                      *, lam_init, tk, bounded):
    q = q_ref[...]
    lane = lax.broadcasted_iota(jnp.int32, q.shape, 1)
    zero = jnp.zeros_like(q)
    qm = (jnp.where(lane < DA_HEAD, q, zero), jnp.where(lane < DA_HEAD, zero, q))
    if not bounded:
        m_scr[...] = jnp.full(m_scr.shape, -jnp.inf, F32)
        l_scr[...] = jnp.zeros(l_scr.shape, F32)
        acc_scr[...] = jnp.zeros(acc_scr.shape, F32)

    def accumulate(k, vt):
        for c in range(2):
            s = _dot_nt(k, qm[c])
            m_prev = m_scr[c]
            m_new = jnp.maximum(m_prev, jnp.max(s, axis=0, keepdims=True))
            alpha = jnp.exp(m_prev - m_new)
            p = jnp.exp(s - m_new)
            l_scr[c] = alpha * l_scr[c] + jnp.sum(p, axis=0, keepdims=True)
            acc_scr[c] = alpha * acc_scr[c] + _dot(vt, p.astype(BF16))
            m_scr[c] = m_new

    def body(j, carry):
        off = pl.multiple_of(j * tk, tk)
        accumulate(kl_ref[pl.ds(off, tk), :], vtl_ref[:, pl.ds(off, tk)])
        return carry

    if bounded:
        n_lat = kl_ref.shape[0] // tk
        keys = [kc_ref] + [kl_ref.at[pl.ds(j * tk, tk), :] for j in range(n_lat)]
        vals = [vtc_ref] + [vtl_ref.at[:, pl.ds(j * tk, tk)] for j in range(n_lat)]
        probs = lambda c, j: jnp.exp(_dot_nt(keys[j][...], qm[c]))

        def add_values(c, j, p):
            l_new = jnp.sum(p, axis=0, keepdims=True)
            acc_new = _dot(vals[j][...], p.astype(BF16))
            l_scr[c] = l_new if j == 0 else l_scr[c] + l_new
            acc_scr[c] = acc_new if j == 0 else acc_scr[c] + acc_new

        p = [probs(0, 0), probs(1, 0)]
        for j in range(len(keys)):
            more = j + 1 < len(keys)
            add_values(0, j, p[0])
            if more:
                p[0] = probs(0, j + 1)
            add_values(1, j, p[1])
            if more:
                p[1] = probs(1, j + 1)
    else:
        accumulate(kc_ref[...], vtc_ref[...])
        lax.fori_loop(0, kl_ref.shape[0] // tk, body, 0)

    lv = lam_ref[...]
    lam = (jnp.exp(jnp.sum(lv[0:1] * lv[1:2], axis=-1, keepdims=True))
           - jnp.exp(jnp.sum(lv[2:3] * lv[3:4], axis=-1, keepdims=True)) + lam_init)
    o = acc_scr[0] * (1.0 / l_scr[0]) - acc_scr[1] * (lam / l_scr[1])
    y = o * lax.rsqrt(jnp.mean(o * o, axis=0, keepdims=True) + SUBLN_EPS)
    o_ref[...] = (y * (sub_ref[...] * (1.0 - lam_init))).T.astype(BF16)


def _diff_attn(q, k_ctx, k_lat, vt_ctx, vt_lat, lam_vecs, subln, lam_init, tq, tk, bounded):
    b, t, w = q.shape
    lc = k_ctx.shape[1]
    nh = w // DA_VHEAD
    assert t % tq == 0 and t % tk == 0
    kern = functools.partial(_diff_attn_kernel, lam_init=lam_init, tk=tk, bounded=bounded)
    return pl.pallas_call(
        kern,
        grid=(b, nh, t // tq),
        in_specs=[pl.BlockSpec((None, tq, DA_VHEAD), lambda bi, h, i: (bi, i, h)),
                  pl.BlockSpec((None, lc, DA_VHEAD), lambda bi, h, i: (bi, 0, h)),
                  pl.BlockSpec((None, t, DA_VHEAD), lambda bi, h, i: (bi, 0, h)),
                  pl.BlockSpec((None, DA_VHEAD, lc), lambda bi, h, i: (bi, h, 0)),
                  pl.BlockSpec((None, DA_VHEAD, t), lambda bi, h, i: (bi, h, 0)),
                  pl.BlockSpec((4, DA_HEAD), lambda bi, h, i: (0, 0)),
                  pl.BlockSpec((DA_VHEAD, 1), lambda bi, h, i: (0, 0))],
        out_specs=pl.BlockSpec((None, tq, DA_VHEAD), lambda bi, h, i: (bi, i, h)),
        out_shape=jax.ShapeDtypeStruct((b, t, w), BF16),
        scratch_shapes=[pltpu.VMEM((2, 1, tq), F32), pltpu.VMEM((2, 1, tq), F32),
                        pltpu.VMEM((2, DA_VHEAD, tq), F32)],
        compiler_params=_cparams(("arbitrary", "arbitrary", "arbitrary")),
        name="diff_attn_bounded" if bounded else "diff_attn",
    )(q, k_ctx, k_lat, vt_ctx, vt_lat, lam_vecs, subln.reshape(DA_VHEAD, 1))


def _gated_out_kernel(y_ref, g_ref, x_ref, gate_ref, w_ref, out_ref):
    y = y_ref[...].astype(F32) * g_ref[...].astype(F32)
    out_ref[...] = x_ref[...] + gate_ref[...] * _dot(y.astype(BF16), w_ref[...])


def _gated_out(y, g, x2d, gate, w_out, rows_per_group, tm):
    r_rows, w = y.shape
    d = x2d.shape[1]
    tpg = rows_per_group // tm
    row = lambda n: pl.BlockSpec((tm, n), lambda i: (i, 0))
    return pl.pallas_call(
        _gated_out_kernel,
        grid=(r_rows // tm,),
        in_specs=[row(w), row(w), row(d),
                  pl.BlockSpec((None, 1, d), lambda i: (i // tpg, 0, 0)),
                  pl.BlockSpec((w, d), lambda i: (0, 0))],
        out_specs=row(d),
        out_shape=jax.ShapeDtypeStruct((r_rows, d), F32),
        compiler_params=_cparams(("arbitrary",)),
        name="gated_out",
    )(y, g, x2d, gate, w_out)


def _segment_matrix(width):
    i = (lax.broadcasted_iota(jnp.int32, (2 * LANES, LANES), 0) % LANES) // width
    j = lax.broadcasted_iota(jnp.int32, (2 * LANES, LANES), 1) // width
    return (i == j).astype(BF16)


def _rope_tables(rows):
    row_ids = jnp.repeat(jnp.arange(rows), GRID_W).astype(F32)
    col_ids = jnp.tile(jnp.arange(GRID_W), rows).astype(F32)
    inv_freq = ROPE_THETA ** (-jnp.arange(ROPE_FREQS, dtype=F32) / ROPE_FREQS)
    ang_r = row_ids[:, None] * inv_freq
    ang_c = col_ids[:, None] * inv_freq
    cr, sr, cc, sc = jnp.cos(ang_r), jnp.sin(ang_r), jnp.cos(ang_c), jnp.sin(ang_c)
    cos = jnp.concatenate([cr, cr, cc, cc], axis=-1)
    sin_signed = jnp.concatenate([-sr, sr, -sc, sc], axis=-1)
    return jnp.tile(cos, (1, 2)), jnp.tile(sin_signed, (1, 2))


def kernel(x, c, ctx, c_ctx, ada_w, ada_b, norm_g, rw_in, rw_mu, rw_w0, rw_w2, rw_a0, rw_a2, rw_kk, rw_ka, rw_rk, rw_gn_g, rw_gn_b, rw_out, da_in, da_qn, da_kn, da_lam, da_subln, da_out):
    bsz, t, d = x.shape
    lc = ctx.shape[1]
    depth = ada_w.shape[0]
    assert depth == 2 and bsz + 1 <= SUBLANES
    cw = rw_kk.shape[-1]
    x2 = x.reshape(bsz * t, d)
    xc2 = ctx.reshape(bsz * lc, d)

    rows = jnp.zeros((SUBLANES, d), F32).at[:bsz].set(c).at[bsz].set(c_ctx)
    mod = _ada_mod(rows, ada_w, ada_b)
    shift, scale, gate = mod[:, :, :d], mod[:, :, d:2 * d], mod[:, :, 2 * d:]
    mult = norm_g[:, None, :] * (1.0 + scale)
    lat = lambda a, i: a[i, :bsz].reshape(bsz, 1, d)
    con = lambda a, i: a[i, bsz:bsz + 1].reshape(1, 1, d)

    tm_l = min(PROJ_ROW_TILE, t)
    tm_c = min(PROJ_ROW_TILE, lc)
    tm_stream = min(STREAM_ROW_TILE, t)

    w_in0 = rw_in[0].astype(BF16)
    seg64 = _segment_matrix(RW_HEAD)
    rwkv_in = functools.partial(_rwkv_in, w_in=w_in0, mu=rw_mu[0], w0=rw_w0[0], w2=rw_w2[0], a0=rw_a0[0],
                                a2=rw_a2[0], k_k=rw_kk[0], k_a=rw_ka[0], r_k=rw_rk[0].reshape(-1), seg=seg64)
    r_l, v_l, kk_l, g_l, bonus_l, lw_l, kd_l, bb_l = rwkv_in(x2, lat(mult, 0), lat(shift, 0), seq_len=t,
                                                             rows_per_group=t, tm=tm_l)
    r_c, v_c, kk_c, g_c, bonus_c, lw_c, kd_c, bb_c = rwkv_in(xc2, con(mult, 0), con(shift, 0), seq_len=lc,
                                                             rows_per_group=bsz * lc, tm=tm_c)

    s0 = jnp.zeros((bsz, 2, cw // LANES, LANES, LANES), F32)
    b3 = lambda a, n: a.reshape(bsz, n, cw)
    b4 = lambda a, n: a.reshape(2, bsz, n, cw)
    ocf, ocb, s_c = _wkv_scan(b3(r_c, lc), b3(v_c, lc), b3(kk_c, lc), b4(lw_c, lc), b4(kd_c, lc),
                              b4(bb_c, lc), s0)
    olf, olb, _ = _wkv_scan(b3(r_l, t), b3(v_l, t), b3(kk_l, t), b4(lw_l, t), b4(kd_l, t), b4(bb_l, t), s_c)

    w_out0 = rw_out[0].astype(BF16)
    readout = functools.partial(_rwkv_readout, gn_g=rw_gn_g[0], gn_b=rw_gn_b[0], seg=seg64, w_out=w_out0)
    x1 = readout(olf.reshape(bsz * t, cw), olb.reshape(bsz * t, cw), bonus_l, g_l, x2,
                 lat(gate, 0), rows_per_group=t, tm=tm_stream)
    xc1 = readout(ocf.reshape(bsz * lc, cw), ocb.reshape(bsz * lc, cw), bonus_c, g_c, xc2,
                  con(gate, 0), rows_per_group=bsz * lc, tm=tm_c)

    qw = DA_HEADS * 2 * DA_HEAD
    w_in1 = da_in[0].astype(BF16)
    seg_da = _segment_matrix(DA_HEAD)
    cos, sin_signed = _rope_tables(t // GRID_W)
    q_b, k_b, vt_b, g2_l = _attn_in(x1, lat(mult, 1), lat(shift, 1), w_in1, True, cos, sin_signed, da_qn[0],
                                    da_kn[0], seg_da, qw, seq_len=t, rows_per_group=t, tm=tm_l)
    kc_b, vtc_b = _attn_in(xc1, con(mult, 1), con(shift, 1), w_in1[:, qw:3 * qw], False, None, None, None,
                           da_kn[0], seg_da, qw, seq_len=lc, rows_per_group=bsz * lc, tm=tm_c)
    lam_init = 0.8 - 0.6 * math.exp(-0.3 * 1)
    attn = functools.partial(_diff_attn, lam_vecs=da_lam[0], subln=da_subln[0], lam_init=lam_init,
                             tq=min(ATTN_Q_TILE, t), tk=min(ATTN_KEY_CHUNK, t))
    score_bound = DA_HEAD ** 0.5 * jnp.max(jnp.abs(da_qn[0])) * jnp.max(jnp.abs(da_kn[0]))
    y_att = lax.cond(score_bound < MAX_UNSHIFTED_SCORE,
                     functools.partial(attn, bounded=True), functools.partial(attn, bounded=False),
                     q_b.reshape(bsz, t, qw), kc_b.reshape(bsz, lc, qw), k_b.reshape(bsz, t, qw), vtc_b, vt_b)
    out = _gated_out(y_att.reshape(bsz * t, qw), g2_l, x1, lat(gate, 1), da_out[0].astype(BF16),
                     rows_per_group=t, tm=tm_stream)
    return out.reshape(bsz, t, d)
```
